```python
import jax, jax.numpy as jnp
from jax import lax
import numpy as np

D_MODEL = 2048
BATCH = 8
SEQ = 2048
DEPTH = 1

PLE_DIM = 256
D_FF = 5632
FOX_HEADS = 8
FOX_HEAD_DIM = 128
FOX_WIDTH = FOX_HEADS * FOX_HEAD_DIM
HGRN_HEADS = 8
HGRN_KEY_DIM = 128
HGRN_VAL_DIM = 128
HGRN_KEY_WIDTH = HGRN_HEADS * HGRN_KEY_DIM
HGRN_VAL_WIDTH = HGRN_HEADS * HGRN_VAL_DIM
Q_BLOCK = 128
CHUNK = 64
NORM_EPS = 1e-6
MACARON_SCALE = 0.5
SPLIT_SIZES = (FOX_WIDTH, FOX_WIDTH, FOX_WIDTH, FOX_HEADS,
               HGRN_KEY_WIDTH, HGRN_KEY_WIDTH, HGRN_VAL_WIDTH, HGRN_VAL_WIDTH,
               D_MODEL, D_MODEL)
IN_WIDTH = int(sum(SPLIT_SIZES))
SPLIT_POINTS = tuple(int(v) for v in np.cumsum(SPLIT_SIZES)[:-1])

kernel_name = 'hybrid_fox_hgrn2_macaron_sandwich_ple'


def rms_norm(x, g):
    xf = x.astype(jnp.float32)
    y = xf * lax.rsqrt(jnp.mean(xf * xf, axis=-1, keepdims=True) + NORM_EPS)
    return (y * g.astype(jnp.float32)).astype(x.dtype)


def swiglu(u, w_gate, w_up, w_down):
    return (jax.nn.silu(u @ w_gate) * (u @ w_up)) @ w_down


def fox_attention(q, k, v, log_f):
    b, s, h, dh = q.shape
    nb = s // Q_BLOCK
    c = jnp.cumsum(log_f, axis=1).transpose(0, 2, 1)
    qb = q.reshape(b, nb, Q_BLOCK, h, dh).transpose(1, 0, 2, 3, 4)
    cb = c.reshape(b, h, nb, Q_BLOCK).transpose(2, 0, 1, 3)
    kpos = jnp.arange(s)
    scale = FOX_HEAD_DIM ** -0.5

    def block(args):
        qi, ci, bi = args
        logits = jnp.einsum('bqhd,bkhd->bhqk', qi, k,
                            preferred_element_type=jnp.float32) * scale
        logits = logits + ci[:, :, :, None] - c[:, :, None, :]
        qpos = bi * Q_BLOCK + jnp.arange(Q_BLOCK)
        causal = kpos[None, :] <= qpos[:, None]
        logits = jnp.where(causal, logits, -jnp.inf)
        probs = jax.nn.softmax(logits, axis=-1)
        return jnp.einsum('bhqk,bkhd->bqhd', probs.astype(v.dtype), v)

    out = lax.map(block, (qb, cb, jnp.arange(nb)))
    return out.transpose(1, 0, 2, 3, 4).reshape(b, s, h, dh)


def hgrn2_recurrence(q, k, v, log_f):
    b, s, h, dk = q.shape
    dv = v.shape[-1]
    n = s // CHUNK

    def chunks(t):
        return t.astype(jnp.float32).reshape(b, n, CHUNK, h, t.shape[-1]).transpose(1, 0, 3, 2, 4)

    causal = jnp.tril(jnp.ones((CHUNK, CHUNK), dtype=bool))

    def step(state, inp):
        qc, kc, vc, lfc = inp
        cum = jnp.cumsum(lfc, axis=2)
        inter = jnp.einsum('bhtk,bhkv->bhtv', qc * jnp.exp(cum), state)
        rel = jnp.where(causal[:, :, None],
                        cum[:, :, :, None, :] - cum[:, :, None, :, :], -jnp.inf)
        scores = jnp.einsum('bhtk,bhsk,bhtsk->bhts', qc, kc, jnp.exp(rel))
        intra = jnp.einsum('bhts,bhsv->bhtv', scores, vc)
        last = cum[:, :, -1:, :]
        new_state = (state * jnp.exp(last[:, :, 0, :, None])
                     + jnp.einsum('bhsk,bhsv->bhkv', kc * jnp.exp(last - cum), vc))
        return new_state, inter + intra

    state0 = jnp.zeros((b, h, dk, dv), jnp.float32)
    _, out = lax.scan(step, state0, (chunks(q), chunks(k), chunks(v), chunks(log_f)))
    return out.transpose(1, 0, 3, 2, 4).reshape(b, s, h, dv)


def token_mixing(u, w_in, fox_f_bias, lower_bound, hgrn_norm_g, w_proj_fox, w_proj_hgrn, w_out):
    b, s, _ = u.shape
    proj = u @ w_in
    q_a, k_a, v_a, f_a, q_b, f_b, i_b, g_b, gate_a, gate_b = jnp.split(proj, SPLIT_POINTS, axis=-1)
    log_fa = jax.nn.log_sigmoid((f_a + fox_f_bias).astype(jnp.float32))
    y_a = fox_attention(q_a.reshape(b, s, FOX_HEADS, FOX_HEAD_DIM),
                        k_a.reshape(b, s, FOX_HEADS, FOX_HEAD_DIM),
                        v_a.reshape(b, s, FOX_HEADS, FOX_HEAD_DIM), log_fa)
    y_a = y_a.reshape(b, s, FOX_WIDTH) @ w_proj_fox
    f = lower_bound + (1.0 - lower_bound) * jax.nn.sigmoid(f_b.astype(jnp.float32))
    o_b = hgrn2_recurrence(jax.nn.silu(q_b).reshape(b, s, HGRN_HEADS, HGRN_KEY_DIM),
                           (1.0 - f).reshape(b, s, HGRN_HEADS, HGRN_KEY_DIM),
                           i_b.reshape(b, s, HGRN_HEADS, HGRN_VAL_DIM),
                           jnp.log(f).reshape(b, s, HGRN_HEADS, HGRN_KEY_DIM))
    o_b = rms_norm(o_b, hgrn_norm_g).astype(u.dtype) * jax.nn.silu(g_b.reshape(b, s, HGRN_HEADS, HGRN_VAL_DIM))
    y_b = o_b.reshape(b, s, HGRN_VAL_WIDTH) @ w_proj_hgrn
    merged = jax.nn.sigmoid(gate_a) * y_a + jax.nn.sigmoid(gate_b) * y_b
    return merged @ w_out


def setup_inputs(seed: int = 0) -> dict:
    key = jax.random.key(seed)
    ks = jax.random.split(key, 25)
    f32 = jnp.float32

    def w(k, shape, fan_in):
        return jax.random.normal(k, shape, f32) * (fan_in ** -0.5)

    def gain(k, shape):
        return 1.0 + 0.02 * jax.random.normal(k, shape, f32)

    L, D, F = DEPTH, D_MODEL, D_FF
    return {
        'x': jax.random.normal(ks[0], (BATCH, SEQ, D), f32),
        'p': jax.random.normal(ks[1], (L, BATCH, SEQ, PLE_DIM), f32),
        'ffn1_pre_g': gain(ks[2], (L, D)),
        'ffn1_post_g': gain(ks[3], (L, D)),
        'ffn1_w_gate': w(ks[4], (L, D, F), D),
        'ffn1_w_up': w(ks[5], (L, D, F), D),
        'ffn1_w_down': w(ks[6], (L, F, D), F),
        'mix_pre_g': gain(ks[7], (L, D)),
        'mix_post_g': gain(ks[8], (L, D)),
        'mix_w_in': w(ks[9], (L, D, IN_WIDTH), D),
        'fox_f_bias': 0.1 * jax.random.normal(ks[10], (L, FOX_HEADS), f32),
        'hgrn_lb_logits': 0.1 * jax.random.normal(ks[11], (L + 1, HGRN_KEY_WIDTH), f32),
        'hgrn_norm_g': gain(ks[12], (L, HGRN_VAL_DIM)),
        'mix_w_proj_fox': w(ks[13], (L, FOX_WIDTH, D), FOX_WIDTH),
        'mix_w_proj_hgrn': w(ks[14], (L, HGRN_VAL_WIDTH, D), HGRN_VAL_WIDTH),
        'mix_w_out': w(ks[15], (L, D, D), D),
        'ffn2_pre_g': gain(ks[16], (L, D)),
        'ffn2_post_g': gain(ks[17], (L, D)),
        'ffn2_w_gate': w(ks[18], (L, D, F), D),
        'ffn2_w_up': w(ks[19], (L, D, F), D),
        'ffn2_w_down': w(ks[20], (L, F, D), F),
        'ple_pre_g': gain(ks[21], (L, D)),
        'ple_post_g': gain(ks[22], (L, D)),
        'ple_w_gate': w(ks[23], (L, D, D), D),
        'ple_w_proj': w(ks[24], (L, PLE_DIM, D), PLE_DIM),
    }


def reference(x, p, ffn1_pre_g, ffn1_post_g, ffn1_w_gate, ffn1_w_up, ffn1_w_down,
              mix_pre_g, mix_post_g, mix_w_in, fox_f_bias, hgrn_lb_logits, hgrn_norm_g,
              mix_w_proj_fox, mix_w_proj_hgrn, mix_w_out,
              ffn2_pre_g, ffn2_post_g, ffn2_w_gate, ffn2_w_up, ffn2_w_down,
              ple_pre_g, ple_post_g, ple_w_gate, ple_w_proj):
    lower_bounds = jnp.cumsum(jax.nn.softmax(hgrn_lb_logits.astype(jnp.float32), axis=0), axis=0)
    h = x
    for i in range(DEPTH):
        h = h + MACARON_SCALE * rms_norm(
            swiglu(rms_norm(h, ffn1_pre_g[i]), ffn1_w_gate[i], ffn1_w_up[i], ffn1_w_down[i]),
            ffn1_post_g[i])
        h = h + rms_norm(
            token_mixing(rms_norm(h, mix_pre_g[i]), mix_w_in[i], fox_f_bias[i], lower_bounds[i],
                         hgrn_norm_g[i], mix_w_proj_fox[i], mix_w_proj_hgrn[i], mix_w_out[i]),
            mix_post_g[i])
        h = h + MACARON_SCALE * rms_norm(
            swiglu(rms_norm(h, ffn2_pre_g[i]), ffn2_w_gate[i], ffn2_w_up[i], ffn2_w_down[i]),
            ffn2_post_g[i])
        u = rms_norm(h, ple_pre_g[i])
        h = h + rms_norm(jax.nn.sigmoid(u @ ple_w_gate[i]) * (p[i] @ ple_w_proj[i]), ple_post_g[i])
    return h
```

```python
import functools

import jax
import jax.numpy as jnp
from jax import lax
from jax.experimental import pallas as pl
from jax.experimental.pallas import tpu as pltpu

NORM_EPS = 1e-6
MACARON_SCALE = 0.5
HEAD_DIM = 128
NUM_HEADS = 8
HGRN_CHUNK = 64
HGRN_SUB = 16
NEG_BIG = -1e30

VMEM_LIMIT_BYTES = 56 * 1024 * 1024

BF16 = jnp.bfloat16
F32 = jnp.float32


def _params(*semantics):
    return pltpu.CompilerParams(dimension_semantics=semantics,
                                vmem_limit_bytes=VMEM_LIMIT_BYTES)


def _rms(x, g):
    ms = jnp.mean(x * x, axis=-1, keepdims=True)
    return x * lax.rsqrt(ms + NORM_EPS) * g


def _dot(a, b):
    return jnp.dot(a, b, preferred_element_type=F32)


def _dot_nt(a, b):
    return lax.dot_general(a, b, (((1,), (1,)), ((), ())), preferred_element_type=F32)


def _split3(x):
    hi = x.astype(BF16)
    r = x - hi.astype(F32)
    mid = r.astype(BF16)
    lo = (r - mid.astype(F32)).astype(BF16)
    return hi, mid, lo


def _ffn_kernel(x_ref, gpre_ref, gpost_ref, wg_ref, wu_ref, wd_ref, o_ref, xn_ref, acc_ref):
    j = pl.program_id(1)

    @pl.when(j == 0)
    def _():
        xn_ref[...] = _rms(x_ref[...], gpre_ref[...]).astype(BF16)
        acc_ref[...] = jnp.zeros_like(acc_ref)

    xn = xn_ref[...]
    g = _dot(xn, wg_ref[...])
    u = _dot(xn, wu_ref[...])
    a = (g * jax.nn.sigmoid(g) * u).astype(BF16)
    acc_ref[...] += _dot(a, wd_ref[...])

    @pl.when(j == pl.num_programs(1) - 1)
    def _():
        o_ref[...] = x_ref[...] + MACARON_SCALE * _rms(acc_ref[...], gpost_ref[...])


def _ffn(h, g_pre, g_post, w_gate, w_up, w_down, *, tm, tf):
    t, d = h.shape
    f = w_gate.shape[1]
    return pl.pallas_call(
        _ffn_kernel,
        grid=(t // tm, f // tf),
        in_specs=[
            pl.BlockSpec((tm, d), lambda i, j: (i, 0)),
            pl.BlockSpec((1, d), lambda i, j: (0, 0)),
            pl.BlockSpec((1, d), lambda i, j: (0, 0)),
            pl.BlockSpec((d, tf), lambda i, j: (0, j)),
            pl.BlockSpec((d, tf), lambda i, j: (0, j)),
            pl.BlockSpec((tf, d), lambda i, j: (j, 0)),
        ],
        out_specs=pl.BlockSpec((tm, d), lambda i, j: (i, 0)),
        out_shape=jax.ShapeDtypeStruct((t, d), F32),
        scratch_shapes=[pltpu.VMEM((tm, d), BF16), pltpu.VMEM((tm, d), F32)],
        compiler_params=_params("parallel", "arbitrary"),
        name="ffn",
    )(h, g_pre.reshape(1, d), g_post.reshape(1, d), w_gate, w_up, w_down)


def _norm_proj_kernel(x_ref, g_ref, w_ref, o_ref, xn_ref):
    @pl.when(pl.program_id(1) == 0)
    def _():
        xn_ref[...] = _rms(x_ref[...], g_ref[...]).astype(BF16)

    o_ref[...] = _dot(xn_ref[...], w_ref[...]).astype(o_ref.dtype)


def _norm_proj(h, g, w, out_dtype, *, tm, tn, name):
    t, d = h.shape
    n = w.shape[1]
    return pl.pallas_call(
        _norm_proj_kernel,
        grid=(t // tm, n // tn),
        in_specs=[
            pl.BlockSpec((tm, d), lambda i, j: (i, 0)),
            pl.BlockSpec((1, d), lambda i, j: (0, 0)),
            pl.BlockSpec((d, tn), lambda i, j: (0, j)),
        ],
        out_specs=pl.BlockSpec((tm, tn), lambda i, j: (i, j)),
        out_shape=jax.ShapeDtypeStruct((t, n), out_dtype),
        scratch_shapes=[pltpu.VMEM((tm, d), BF16)],
        compiler_params=_params("parallel", "arbitrary"),
        name=name,
    )(h, g.reshape(1, d), w)


def _fox_gate_kernel(x_ref, g_ref, wt_ref, b_ref, o_ref, carry_ref, *, ts):
    @pl.when(pl.program_id(1) == 0)
    def _():
        carry_ref[...] = jnp.zeros_like(carry_ref)

    u = _rms(x_ref[...], g_ref[...]).astype(BF16)
    z = _dot_nt(wt_ref[...], u) + b_ref[...]
    lf = jnp.minimum(z, 0.0) - jnp.log1p(jnp.exp(-jnp.abs(z)))
    src = lax.broadcasted_iota(jnp.int32, (ts, ts), 0)
    dst = lax.broadcasted_iota(jnp.int32, (ts, ts), 1)
    tri = jnp.where(src <= dst, 1.0, 0.0).astype(BF16)
    hi, mid, lo = _split3(lf)
    c = _dot(hi, tri) + _dot(mid, tri) + _dot(lo, tri) + carry_ref[:, :1]
    o_ref[...] = c
    carry_ref[...] = jnp.broadcast_to(c[:, ts - 1:ts], carry_ref.shape)


def _fox_gate(h3, g, w_f_t, bias, *, ts):
    b, s, d = h3.shape
    nh = w_f_t.shape[0]
    return pl.pallas_call(
        functools.partial(_fox_gate_kernel, ts=ts),
        grid=(b, s // ts),
        in_specs=[
            pl.BlockSpec((None, ts, d), lambda i, j: (i, j, 0)),
            pl.BlockSpec((1, d), lambda i, j: (0, 0)),
            pl.BlockSpec((nh, d), lambda i, j: (0, 0)),
            pl.BlockSpec((nh, 1), lambda i, j: (0, 0)),
        ],
        out_specs=pl.BlockSpec((None, nh, ts), lambda i, j: (i, 0, j)),
        out_shape=jax.ShapeDtypeStruct((b, nh, s), F32),
        scratch_shapes=[pltpu.VMEM((nh, HEAD_DIM), F32)],
        compiler_params=_params("parallel", "arbitrary"),
        name="fox_gate",
    )(h3, g.reshape(1, d), w_f_t, bias.reshape(nh, 1))


def _fox_attn_kernel(q_ref, k_ref, v_ref, c_ref, o_ref, m_ref, l_ref, acc_ref, *, tq, scale):
    qi = pl.program_id(2)
    q = q_ref[...]
    q0 = pl.multiple_of(qi * tq, tq)
    c0 = jnp.max(c_ref[:, pl.ds(q0, tq)], axis=-1, keepdims=True)

    m_ref[...] = jnp.full_like(m_ref, NEG_BIG)
    l_ref[...] = jnp.zeros_like(l_ref)
    acc_ref[...] = jnp.zeros_like(acc_ref)

    def step(k0, diagonal):
        k = k_ref[pl.ds(k0, tq), :]
        v = v_ref[pl.ds(k0, tq), :]
        s = _dot_nt(q, k) * scale + (c0 - c_ref[:, pl.ds(k0, tq)])
        if diagonal:
            row = lax.broadcasted_iota(jnp.int32, (tq, tq), 0)
            col = lax.broadcasted_iota(jnp.int32, (tq, tq), 1)
            s = jnp.where(col <= row, s, NEG_BIG)
        m_prev = m_ref[...]
        m_next = jnp.maximum(m_prev, jnp.max(s, axis=-1, keepdims=True))
        p = jnp.exp(s - m_next)
        alpha = jnp.exp(m_prev - m_next)
        l_ref[...] = alpha * l_ref[...] + jnp.sum(p, axis=-1, keepdims=True)
        acc_ref[...] = alpha * acc_ref[...] + _dot(p.astype(BF16), v)
        m_ref[...] = m_next

    def body(ki, carry):
        step(pl.multiple_of(ki * tq, tq), False)
        return carry

    lax.fori_loop(0, qi, body, 0)
    step(q0, True)
    o_ref[...] = (acc_ref[...] / l_ref[...]).astype(o_ref.dtype)


def _fox_attn(qkv3, c4, *, tq):
    b, s, _ = qkv3.shape
    nh, dh = NUM_HEADS, HEAD_DIM
    return pl.pallas_call(
        functools.partial(_fox_attn_kernel, tq=tq, scale=dh ** -0.5),
        grid=(b, nh, s // tq),
        in_specs=[
            pl.BlockSpec((None, tq, dh), lambda i, h, j: (i, j, h)),
            pl.BlockSpec((None, s, dh), lambda i, h, j: (i, 0, nh + h)),
            pl.BlockSpec((None, s, dh), lambda i, h, j: (i, 0, 2 * nh + h)),
            pl.BlockSpec((None, None, 1, s), lambda i, h, j: (i, h, 0, 0)),
        ],
        out_specs=pl.BlockSpec((None, tq, dh), lambda i, h, j: (i, j, h)),
        out_shape=jax.ShapeDtypeStruct((b, s, nh * dh), BF16),
        scratch_shapes=[pltpu.VMEM((tq, 1), F32), pltpu.VMEM((tq, 1), F32),
                        pltpu.VMEM((tq, dh), F32)],
        compiler_params=_params("parallel", "parallel", "arbitrary"),
        name="fox_attn",
    )(qkv3, qkv3, qkv3, c4)


def _hgrn_kernel(q_ref, f_ref, i_ref, g_ref, lbl_ref, ng_ref, o_ref,
                 st_ref, cum_ref, kk_ref, v_ref, *, lc):
    cs, sb = HGRN_CHUNK, HGRN_SUB

    @pl.when(pl.program_id(2) == 0)
    def _():
        st_ref[...] = jnp.zeros_like(st_ref)

    logits = lbl_ref[...]
    e = jnp.exp(logits - jnp.max(logits, axis=0, keepdims=True))
    lb = e[0:1, :] / jnp.sum(e, axis=0, keepdims=True)
    ng = ng_ref[...]

    src = lax.broadcasted_iota(jnp.int32, (cs, cs), 1)
    dst = lax.broadcasted_iota(jnp.int32, (cs, cs), 0)
    tri = jnp.where(src <= dst, 1.0, 0.0).astype(BF16)
    sub_row = lax.broadcasted_iota(jnp.int32, (sb, HEAD_DIM), 0)
    blk_col = lax.broadcasted_iota(jnp.int32, (sb, cs), 1)

    def chunk(ci, carry):
        base = pl.multiple_of(ci * cs, cs)
        rows = pl.ds(base, cs)
        q_raw = q_ref[rows, :]
        q = q_raw * jax.nn.sigmoid(q_raw)
        f = lb + (1.0 - lb) * jax.nn.sigmoid(f_ref[rows, :])
        lf = jnp.log(f)
        kk = 1.0 - f
        v = i_ref[rows, :]
        hi, mid, lo = _split3(lf)
        cum = _dot(tri, hi) + _dot(tri, mid) + _dot(tri, lo)
        cum_ref[...] = cum
        kk_ref[...] = kk
        v_ref[...] = v
        v_bf = v.astype(BF16)

        st = st_ref[...]
        last = cum[cs - 1:cs, :]
        inter = _dot_nt((q * jnp.exp(cum)).astype(BF16), st.astype(BF16))
        kdec = (kk * jnp.exp(last - cum)).astype(BF16)
        st_ref[...] = st * jnp.exp(last) + _dot(v.T.astype(BF16), kdec)

        score_rows = [jnp.zeros((sb, cs), F32)]
        for bi in range(1, cs // sb):
            r0 = bi * sb
            ref = cum[r0 - 1:r0, :]
            qt = q[r0:r0 + sb, :] * jnp.exp(cum[r0:r0 + sb, :] - ref)
            kt = kk * jnp.exp(jnp.minimum(ref - cum, 0.0))
            sc = _dot_nt(qt.astype(BF16), kt.astype(BF16))
            score_rows.append(jnp.where(blk_col < r0, sc, 0.0))
        scores = jnp.concatenate(score_rows, axis=0)
        off = _dot(scores.astype(BF16), v_bf)

        diag_rows = []
        for bi in range(cs // sb):
            r0 = bi * sb
            qb = q[r0:r0 + sb, :]
            cb = cum[r0:r0 + sb, :]
            acc = jnp.zeros((sb, HEAD_DIM), F32)
            for si in range(sb):
                r = r0 + si
                d = jnp.where(sub_row >= si, cb - cum_ref[r:r + 1, :], NEG_BIG)
                w = qb * jnp.exp(d) * kk_ref[r:r + 1, :]
                acc = acc + jnp.sum(w, axis=-1, keepdims=True) * v_ref[r:r + 1, :]
            diag_rows.append(acc)
        o = inter + off + jnp.concatenate(diag_rows, axis=0)

        g_raw = g_ref[rows, :]
        o = _rms(o, ng) * (g_raw * jax.nn.sigmoid(g_raw))
        o_ref[rows, :] = o.astype(o_ref.dtype)
        return carry

    lax.fori_loop(0, lc // cs, chunk, 0)


def _hgrn(pb3, lb_logits, norm_g, *, lc):
    b, s, _ = pb3.shape
    nh, dh = NUM_HEADS, HEAD_DIM
    nl = lb_logits.shape[0]

    def col(group):
        return pl.BlockSpec((None, lc, dh), lambda i, h, j: (i, j, group * nh + h))

    return pl.pallas_call(
        functools.partial(_hgrn_kernel, lc=lc),
        grid=(b, nh, s // lc),
        in_specs=[
            col(0), col(1), col(2), col(3),
            pl.BlockSpec((nl, dh), lambda i, h, j: (0, h)),
            pl.BlockSpec((1, dh), lambda i, h, j: (0, 0)),
        ],
        out_specs=pl.BlockSpec((None, lc, dh), lambda i, h, j: (i, j, h)),
        out_shape=jax.ShapeDtypeStruct((b, s, nh * dh), BF16),
        scratch_shapes=[pltpu.VMEM((dh, dh), F32), pltpu.VMEM((HGRN_CHUNK, dh), F32),
                        pltpu.VMEM((HGRN_CHUNK, dh), F32), pltpu.VMEM((HGRN_CHUNK, dh), F32)],
        compiler_params=_params("parallel", "parallel", "arbitrary"),
        name="hgrn",
    )(pb3, pb3, pb3, pb3, lb_logits, norm_g.reshape(1, dh))


def _merge_kernel(h_ref, gpre_ref, gpost_ref, ya_ref, yb_ref, wpa_ref, wpb_ref,
                  wga_ref, wgb_ref, wo_ref, o_ref, u_ref, acc_ref):
    j = pl.program_id(1)

    @pl.when(j == 0)
    def _():
        u_ref[...] = _rms(h_ref[...], gpre_ref[...]).astype(BF16)
        acc_ref[...] = jnp.zeros_like(acc_ref)

    u = u_ref[...]
    y_a = _dot(ya_ref[...], wpa_ref[...])
    y_b = _dot(yb_ref[...], wpb_ref[...])
    gate_a = jax.nn.sigmoid(_dot(u, wga_ref[...]))
    gate_b = jax.nn.sigmoid(_dot(u, wgb_ref[...]))
    merged = (gate_a * y_a + gate_b * y_b).astype(BF16)
    acc_ref[...] += _dot(merged, wo_ref[...])

    @pl.when(j == pl.num_programs(1) - 1)
    def _():
        o_ref[...] = h_ref[...] + _rms(acc_ref[...], gpost_ref[...])


def _merge(h, g_pre, g_post, y_a, y_b, w_pa, w_pb, w_ga, w_gb, w_o, *, tm, tn):
    t, d = h.shape
    wa = y_a.shape[1]
    wb = y_b.shape[1]
    return pl.pallas_call(
        _merge_kernel,
        grid=(t // tm, d // tn),
        in_specs=[
            pl.BlockSpec((tm, d), lambda i, j: (i, 0)),
            pl.BlockSpec((1, d), lambda i, j: (0, 0)),
            pl.BlockSpec((1, d), lambda i, j: (0, 0)),
            pl.BlockSpec((tm, wa), lambda i, j: (i, 0)),
            pl.BlockSpec((tm, wb), lambda i, j: (i, 0)),
            pl.BlockSpec((wa, tn), lambda i, j: (0, j)),
            pl.BlockSpec((wb, tn), lambda i, j: (0, j)),
            pl.BlockSpec((d, tn), lambda i, j: (0, j)),
            pl.BlockSpec((d, tn), lambda i, j: (0, j)),
            pl.BlockSpec((tn, d), lambda i, j: (j, 0)),
        ],
        out_specs=pl.BlockSpec((tm, d), lambda i, j: (i, 0)),
        out_shape=jax.ShapeDtypeStruct((t, d), F32),
        scratch_shapes=[pltpu.VMEM((tm, d), BF16), pltpu.VMEM((tm, d), F32)],
        compiler_params=_params("parallel", "arbitrary"),
        name="merge",
    )(h, g_pre.reshape(1, d), g_post.reshape(1, d), y_a, y_b, w_pa, w_pb, w_ga, w_gb, w_o)


def _ple_kernel(h_ref, gpre_ref, gpost_ref, p_ref, wg_ref, wp_ref, o_ref):
    h = h_ref[...]
    u = _rms(h, gpre_ref[...]).astype(BF16)
    gate = jax.nn.sigmoid(_dot(u, wg_ref[...]))
    z = gate * _dot(p_ref[...].astype(BF16), wp_ref[...])
    o_ref[...] = h + _rms(z, gpost_ref[...])


def _ple(h, g_pre, g_post, p2, w_g, w_p, *, tm):
    t, d = h.shape
    dp = p2.shape[1]
    return pl.pallas_call(
        _ple_kernel,
        grid=(t // tm,),
        in_specs=[
            pl.BlockSpec((tm, d), lambda i: (i, 0)),
            pl.BlockSpec((1, d), lambda i: (0, 0)),
            pl.BlockSpec((1, d), lambda i: (0, 0)),
            pl.BlockSpec((tm, dp), lambda i: (i, 0)),
            pl.BlockSpec((d, d), lambda i: (0, 0)),
            pl.BlockSpec((dp, d), lambda i: (0, 0)),
        ],
        out_specs=pl.BlockSpec((tm, d), lambda i: (i, 0)),
        out_shape=jax.ShapeDtypeStruct((t, d), F32),
        compiler_params=_params("parallel"),
        name="ple",
    )(h, g_pre.reshape(1, d), g_post.reshape(1, d), p2, w_g, w_p)


def _tile(n, want):
    t = min(n, want)
    while n % t:
        t //= 2
    return t


def kernel(x, p, ffn1_pre_g, ffn1_post_g, ffn1_w_gate, ffn1_w_up, ffn1_w_down, mix_pre_g, mix_post_g, mix_w_in, fox_f_bias, hgrn_lb_logits, hgrn_norm_g, mix_w_proj_fox, mix_w_proj_hgrn, mix_w_out, ffn2_pre_g, ffn2_post_g, ffn2_w_gate, ffn2_w_up, ffn2_w_down, ple_pre_g, ple_post_g, ple_w_gate, ple_w_proj):
    b, s, d = x.shape
    t = b * s
    depth = ffn1_pre_g.shape[0]
    assert depth == 1, "the HGRN2 lower bound is evaluated for a single layer"
    nh, dh = NUM_HEADS, HEAD_DIM
    width = nh * dh
    assert mix_w_in.shape[-1] == 3 * width + nh + 4 * width + 2 * d

    tm = _tile(t, 512)
    h = x.reshape(t, d)
    for i in range(depth):
        h = _ffn(h, ffn1_pre_g[i], ffn1_post_g[i], ffn1_w_gate[i].astype(BF16),
                 ffn1_w_up[i].astype(BF16), ffn1_w_down[i].astype(BF16),
                 tm=tm, tf=_tile(ffn1_w_gate.shape[-1], 512))

        w_in = mix_w_in[i]
        o_f = 3 * width
        o_b = o_f + nh
        o_g = o_b + 4 * width
        w_qkv = w_in[:, :o_f].astype(BF16)
        w_f_t = w_in[:, o_f:o_b].T.astype(BF16)
        w_b = w_in[:, o_b:o_g].astype(BF16)
        w_ga = w_in[:, o_g:o_g + d].astype(BF16)
        w_gb = w_in[:, o_g + d:].astype(BF16)

        qkv = _norm_proj(h, mix_pre_g[i], w_qkv, BF16, tm=tm, tn=_tile(o_f, 1024), name="proj_fox")
        pb = _norm_proj(h, mix_pre_g[i], w_b, F32, tm=tm, tn=_tile(4 * width, 1024), name="proj_hgrn")
        c = _fox_gate(h.reshape(b, s, d), mix_pre_g[i], w_f_t, fox_f_bias[i], ts=_tile(s, 512))
        y_a = _fox_attn(qkv.reshape(b, s, o_f), c.reshape(b, nh, 1, s), tq=_tile(s, 512))
        y_b = _hgrn(pb.reshape(b, s, 4 * width), hgrn_lb_logits, hgrn_norm_g[i], lc=_tile(s, 256))
        h = _merge(h, mix_pre_g[i], mix_post_g[i], y_a.reshape(t, width), y_b.reshape(t, width),
                   mix_w_proj_fox[i].astype(BF16), mix_w_proj_hgrn[i].astype(BF16),
                   w_ga, w_gb, mix_w_out[i].astype(BF16), tm=tm, tn=_tile(d, 256))

        h = _ffn(h, ffn2_pre_g[i], ffn2_post_g[i], ffn2_w_gate[i].astype(BF16),
                 ffn2_w_up[i].astype(BF16), ffn2_w_down[i].astype(BF16),
                 tm=tm, tf=_tile(ffn2_w_gate.shape[-1], 512))
        h = _ple(h, ple_pre_g[i], ple_post_g[i], p[i].reshape(t, -1),
                 ple_w_gate[i].astype(BF16), ple_w_proj[i].astype(BF16), tm=_tile(t, 256))
    return h.reshape(b, s, d)
```

```python
import functools
import math

import jax
import jax.numpy as jnp
from jax import lax
from jax.experimental import pallas as pl
from jax.experimental.pallas import tpu as pltpu

NORM_EPS = 1e-6
MACARON_SCALE = 0.5
LOG2E = math.log2(math.e)
HEAD_DIM = 128
NUM_HEADS = 8
HGRN_CHUNK = 64
HGRN_SUB = 16
HGRN_BLOCK = 256
HGRN_GROUP = 4
HGRN_SAFE_LOG2 = 100.0
FFN_COL_GROUPS = 2
MERGE_COL_GROUPS = 2
PLE_ROW_GROUPS = 2
SUBLANES = 8
NEG_BIG = -1e30

VMEM_LIMIT_BYTES = 56 * 1024 * 1024

BF16 = jnp.bfloat16
F32 = jnp.float32


def _params(*semantics):
    return pltpu.CompilerParams(dimension_semantics=semantics,
                                vmem_limit_bytes=VMEM_LIMIT_BYTES)


def _rms(x, g):
    ms = jnp.mean(x * x, axis=-1, keepdims=True)
    return x * lax.rsqrt(ms + NORM_EPS) * g


def _sigmoid(x):
    return 0.5 + 0.5 * jnp.tanh(0.5 * x)


def _silu(x):
    h = 0.5 * x
    return h + h * jnp.tanh(h)


def _dot(a, b):
    return jnp.dot(a, b, preferred_element_type=F32)


def _dot_nt(a, b):
    return lax.dot_general(a, b, (((1,), (1,)), ((), ())), preferred_element_type=F32)


def _split3(x):
    hi = x.astype(BF16)
    r = x - hi.astype(F32)
    mid = r.astype(BF16)
    lo = (r - mid.astype(F32)).astype(BF16)
    return hi, mid, lo


def _ffn_kernel(x_ref, gpre_ref, gpost_ref, wg_ref, wu_ref, wd_ref, o_ref, xn_ref, acc_ref):
    j = pl.program_id(1)

    @pl.when(j == 0)
    def _():
        xn_ref[...] = _rms(x_ref[...], gpre_ref[...]).astype(BF16)
        acc_ref[...] = jnp.zeros_like(acc_ref)

    xn = xn_ref[...]
    tf = wg_ref.shape[1]
    groups = [slice(c0, c0 + tf // FFN_COL_GROUPS) for c0 in range(0, tf, tf // FFN_COL_GROUPS)]
    gu = [(_dot(xn, wg_ref[:, c]), _dot(xn, wu_ref[:, c])) for c in groups]
    acts = [(g * jax.nn.sigmoid(g) * u).astype(BF16) for g, u in gu]
    down = _dot(acts[0], wd_ref[groups[0], :])
    for a, c in zip(acts[1:], groups[1:]):
        down += _dot(a, wd_ref[c, :])
    acc_ref[...] += down

    @pl.when(j == pl.num_programs(1) - 1)
    def _():
        o_ref[...] = x_ref[...] + MACARON_SCALE * _rms(acc_ref[...], gpost_ref[...])


def _ffn(h, g_pre, g_post, w_gate, w_up, w_down, *, tm, tf):
    t, d = h.shape
    f = w_gate.shape[1]
    return pl.pallas_call(
        _ffn_kernel,
        grid=(t // tm, f // tf),
        in_specs=[
            pl.BlockSpec((tm, d), lambda i, j: (i, 0)),
            pl.BlockSpec((1, d), lambda i, j: (0, 0)),
            pl.BlockSpec((1, d), lambda i, j: (0, 0)),
            pl.BlockSpec((d, tf), lambda i, j: (0, j)),
            pl.BlockSpec((d, tf), lambda i, j: (0, j)),
            pl.BlockSpec((tf, d), lambda i, j: (j, 0)),
        ],
        out_specs=pl.BlockSpec((tm, d), lambda i, j: (i, 0)),
        out_shape=jax.ShapeDtypeStruct((t, d), F32),
        scratch_shapes=[pltpu.VMEM((tm, d), BF16), pltpu.VMEM((tm, d), F32)],
        compiler_params=_params("parallel", "arbitrary"),
        name="ffn",
    )(h, g_pre.reshape(1, d), g_post.reshape(1, d), w_gate, w_up, w_down)


def _proj_in_kernel(x_ref, g_ref, w_ref, cs_ref, wf_ref, o_ref, of_ref, xn_ref):
    j = pl.program_id(1)
    last = pl.num_programs(1) - 1

    @pl.when(j == 0)
    def _():
        xn_ref[...] = _rms(x_ref[...], g_ref[...]).astype(BF16)

    @pl.when(j < last)
    def _():
        o_ref[...] = (_dot(xn_ref[...], w_ref[...]) * cs_ref[...]).astype(o_ref.dtype)

    @pl.when(j == last)
    def _():
        of_ref[...] = _dot(xn_ref[...], wf_ref[...])


def _proj_in(h, g, w_main, col_scale, w_gate, *, tm, tn):
    t, d = h.shape
    n = w_main.shape[1]
    nf = w_gate.shape[1]
    nblk = n // tn

    def main_col(i, j):
        return (0, jnp.minimum(j, nblk - 1))

    return pl.pallas_call(
        _proj_in_kernel,
        grid=(t // tm, nblk + 1),
        in_specs=[
            pl.BlockSpec((tm, d), lambda i, j: (i, 0)),
            pl.BlockSpec((1, d), lambda i, j: (0, 0)),
            pl.BlockSpec((d, tn), main_col),
            pl.BlockSpec((1, tn), main_col),
            pl.BlockSpec((d, nf), lambda i, j: (0, 0)),
        ],
        out_specs=[
            pl.BlockSpec((tm, tn), lambda i, j: (i, jnp.minimum(j, nblk - 1))),
            pl.BlockSpec((tm, nf), lambda i, j: (i, 0)),
        ],
        out_shape=[jax.ShapeDtypeStruct((t, n), BF16), jax.ShapeDtypeStruct((t, nf), F32)],
        scratch_shapes=[pltpu.VMEM((tm, d), BF16)],
        compiler_params=_params("parallel", "arbitrary"),
        name="proj_in",
    )(h, g.reshape(1, d), w_main, col_scale, w_gate)


def _fox_gate_kernel(f_ref, b_ref, o_ref, carry_ref, *, ts):
    @pl.when(pl.program_id(1) == 0)
    def _():
        carry_ref[...] = jnp.zeros_like(carry_ref)

    z = f_ref[...].T[:NUM_HEADS, :] + b_ref[...]
    lf = (jnp.minimum(z, 0.0) - jnp.log1p(jnp.exp(-jnp.abs(z)))) * LOG2E
    src = lax.broadcasted_iota(jnp.int32, (ts, ts), 0)
    dst = lax.broadcasted_iota(jnp.int32, (ts, ts), 1)
    tri = jnp.where(src <= dst, 1.0, 0.0).astype(BF16)
    hi, mid, lo = _split3(lf)
    c = _dot(hi, tri) + _dot(mid, tri) + _dot(lo, tri) + carry_ref[:, :1]
    o_ref[...] = c
    carry_ref[...] = jnp.broadcast_to(c[:, ts - 1:ts], carry_ref.shape)


def _fox_gate(fg3, lane_block, bias, *, ts):
    b, s, _ = fg3.shape
    nh = NUM_HEADS
    return pl.pallas_call(
        functools.partial(_fox_gate_kernel, ts=ts),
        grid=(b, s // ts),
        in_specs=[
            pl.BlockSpec((None, ts, HEAD_DIM), lambda i, j: (i, j, lane_block)),
            pl.BlockSpec((nh, 1), lambda i, j: (0, 0)),
        ],
        out_specs=pl.BlockSpec((None, nh, ts), lambda i, j: (i, 0, j)),
        out_shape=jax.ShapeDtypeStruct((b, nh, s), F32),
        scratch_shapes=[pltpu.VMEM((nh, HEAD_DIM), F32)],
        compiler_params=_params("parallel", "arbitrary"),
        name="fox_gate",
    )(fg3, bias.reshape(nh, 1))


def _fox_attn_kernel(q_ref, k_ref, v_ref, c_ref, o_ref, m_ref, l_ref, acc_ref, *, tq):
    nq = q_ref.shape[0] // tq
    reps = tq // HEAD_DIM

    m_ref[...] = jnp.full_like(m_ref, NEG_BIG)
    l_ref[...] = jnp.zeros_like(l_ref)
    acc_ref[...] = jnp.zeros_like(acc_ref)

    def step(qi, ki):
        rows = slice(qi * tq, (qi + 1) * tq)
        cols = slice(ki * tq, (ki + 1) * tq)
        c0 = jnp.max(c_ref[:, rows], axis=-1, keepdims=True)
        s = _dot_nt(q_ref[rows, :], k_ref[cols, :]) + (c0 - c_ref[:, cols])
        if qi == ki:
            row = lax.broadcasted_iota(jnp.int32, (tq, tq), 0)
            col = lax.broadcasted_iota(jnp.int32, (tq, tq), 1)
            s = jnp.where(col <= row, s, NEG_BIG)
        m_prev = m_ref[rows, :]
        m_next = jnp.maximum(m_prev, jnp.max(s, axis=-1, keepdims=True))
        p = jnp.exp2(s - jnp.concatenate([m_next] * reps, axis=1))
        alpha = jnp.exp2(m_prev - m_next)
        l_ref[rows, :] = alpha * l_ref[rows, :] + jnp.sum(p, axis=-1, keepdims=True)
        acc_ref[rows, :] = alpha * acc_ref[rows, :] + _dot(p.astype(BF16), v_ref[cols, :])
        m_ref[rows, :] = m_next

    for diag in range(nq):
        for qi in range(diag, nq):
            step(qi, qi - diag)
    o_ref[...] = (acc_ref[...] / l_ref[...]).astype(o_ref.dtype)


def _fox_attn(main3, c4, *, tq):
    b, s, _ = main3.shape
    nh, dh = NUM_HEADS, HEAD_DIM
    return pl.pallas_call(
        functools.partial(_fox_attn_kernel, tq=tq),
        grid=(b, nh),
        in_specs=[
            pl.BlockSpec((None, s, dh), lambda i, h: (i, 0, h)),
            pl.BlockSpec((None, s, dh), lambda i, h: (i, 0, nh + h)),
            pl.BlockSpec((None, s, dh), lambda i, h: (i, 0, 2 * nh + h)),
            pl.BlockSpec((None, None, 1, s), lambda i, h: (i, h, 0, 0)),
        ],
        out_specs=pl.BlockSpec((None, s, dh), lambda i, h: (i, 0, h)),
        out_shape=jax.ShapeDtypeStruct((b, s, nh * dh), BF16),
        scratch_shapes=[pltpu.VMEM((s, dh), F32), pltpu.VMEM((s, dh), F32),
                        pltpu.VMEM((s, dh), F32)],
        compiler_params=_params("parallel", "parallel"),
        name="fox_attn",
    )(main3, main3, main3, c4)


def _hgrn_kernel(q_ref, f_ref, i_ref, g_ref, lbl_ref, ng_ref, o_ref,
                 st_ref, qe_ref, intra_ref, u_ref, dec_ref, a_ref, v_ref, *, lc):
    cs, sb, blk = HGRN_CHUNK, HGRN_SUB, HGRN_BLOCK
    half = sb // 2
    cpb = blk // cs

    @pl.when(pl.program_id(2) == 0)
    def _():
        st_ref[...] = jnp.zeros_like(st_ref)

    logits = lbl_ref[...]
    e = jnp.exp(logits - jnp.max(logits, axis=0, keepdims=True))
    lb = e[0:1, :] / jnp.sum(e, axis=0, keepdims=True)
    ng = ng_ref[...]

    src = lax.broadcasted_iota(jnp.int32, (blk, blk), 1)
    dst = lax.broadcasted_iota(jnp.int32, (blk, blk), 0)
    same_chunk = (src // cs) == (dst // cs)
    tri = jnp.where(same_chunk & (src <= dst), 1.0, 0.0).astype(BF16)
    half_row = lax.broadcasted_iota(jnp.int32, (half, HEAD_DIM), 0)
    blk_col = lax.broadcasted_iota(jnp.int32, (sb, cs), 1)
    chunk_row = lax.broadcasted_iota(jnp.int32, (cs, cs), 0)
    chunk_col = lax.broadcasted_iota(jnp.int32, (cs, cs), 1)

    def prep(rows):
        q = _silu(q_ref[rows, :].astype(F32))
        f = lb + (1.0 - lb) * _sigmoid(f_ref[rows, :])
        kk = 1.0 - f
        hi, mid, lo = _split3(jnp.log2(f))
        cum = _dot(tri, hi) + _dot(tri, mid) + _dot(tri, lo)
        lasts = [cum[c0 + cs - 1:c0 + cs, :] for c0 in range(0, blk, cs)]
        return q, kk, cum, lasts

    def state_increment(ci, kk, cum, lasts, v):
        c0 = ci * cs
        kdec = (kk[c0:c0 + cs, :] * jnp.exp2(lasts[ci] - cum[c0:c0 + cs, :])).astype(BF16)
        return _dot(v[c0:c0 + cs, :].T.astype(BF16), kdec)

    def finish(rows, outs):
        o = _rms(jnp.concatenate(outs, axis=0), ng) * _silu(g_ref[rows, :].astype(F32))
        o_ref[rows, :] = o.astype(o_ref.dtype)

    def precompute(gi, deepest):
        blocks = [gi * HGRN_GROUP + k for k in range(HGRN_GROUP)]
        rows = [pl.ds(pl.multiple_of(bi * blk, blk), blk) for bi in blocks]
        stage1 = [prep(r) for r in rows]
        stage2 = []
        for bi, r, (q, kk, cum, lasts) in zip(blocks, rows, stage1):
            v = i_ref[r, :]
            v32 = v.astype(F32)
            qe = (q * jnp.exp2(cum)).astype(BF16)
            kinv = (kk * jnp.exp2(-cum)).astype(BF16)
            qe_ref[r, :] = qe
            scores = [_dot_nt(qe[c0:c0 + cs, :], kinv[c0:c0 + cs, :]) for c0 in range(0, blk, cs)]
            for ci in range(cpb):
                u_ref[bi * cpb + ci] = state_increment(ci, kk, cum, lasts, v32)
                dec_ref[pl.ds(pl.multiple_of((bi * cpb + ci) * SUBLANES, SUBLANES), SUBLANES), :] = (
                    jnp.broadcast_to(jnp.exp2(lasts[ci]), (SUBLANES, HEAD_DIM)))
                deepest = jnp.maximum(deepest, -lasts[ci])
            stage2.append((v, scores))
        for r, (v, scores) in zip(rows, stage2):
            intra = [_dot(jnp.where(chunk_col <= chunk_row, sc, 0.0).astype(BF16), v[ci * cs:(ci + 1) * cs, :])
                     for ci, sc in enumerate(scores)]
            intra_ref[r, :] = jnp.concatenate(intra, axis=0)
        return deepest

    deepest = lax.fori_loop(0, lc // (blk * HGRN_GROUP), precompute, jnp.zeros((1, HEAD_DIM), F32))
    depth = jnp.max(deepest)

    @pl.when(depth <= HGRN_SAFE_LOG2)
    def _():
        def recur(bi, carry):
            base = pl.multiple_of(bi * blk, blk)
            st = st_ref[...]
            outs = []
            for ci in range(cpb):
                rows_c = pl.ds(base + ci * cs, cs)
                dec = dec_ref[pl.ds(pl.multiple_of((bi * cpb + ci) * SUBLANES, SUBLANES), SUBLANES), :]
                outs.append(_dot_nt(qe_ref[rows_c, :], st.astype(BF16)) + intra_ref[rows_c, :])
                st = st * dec[0:1, :] + u_ref[bi * cpb + ci]
            st_ref[...] = st
            finish(pl.ds(base, blk), outs)
            return carry

        lax.fori_loop(0, lc // blk, recur, 0, unroll=4)

    @pl.when(jnp.logical_not(depth <= HGRN_SAFE_LOG2))
    def _():
        def general(bi, carry):
            rows = pl.ds(pl.multiple_of(bi * blk, blk), blk)
            q, kk, cum, lasts = prep(rows)
            v = i_ref[rows, :]
            v32 = v.astype(F32)
            a_ref[...] = cum - jnp.log2(kk)
            v_ref[...] = v32
            st = st_ref[...]
            outs = []
            for ci in range(cpb):
                c0 = ci * cs
                cum_c = cum[c0:c0 + cs, :]
                q_c = q[c0:c0 + cs, :]
                kk_c = kk[c0:c0 + cs, :]
                inter = _dot_nt((q_c * jnp.exp2(cum_c)).astype(BF16), st.astype(BF16))
                st = st * jnp.exp2(lasts[ci]) + state_increment(ci, kk, cum, lasts, v32)

                score_rows = [jnp.zeros((sb, cs), F32)]
                for si in range(1, cs // sb):
                    r0 = si * sb
                    ref = cum_c[r0 - 1:r0, :]
                    qt = q_c[r0:r0 + sb, :] * jnp.exp2(cum_c[r0:r0 + sb, :] - ref)
                    kt = kk_c * jnp.exp2(jnp.minimum(ref - cum_c, 0.0))
                    sc = _dot_nt(qt.astype(BF16), kt.astype(BF16))
                    score_rows.append(jnp.where(blk_col < r0, sc, 0.0))
                off = _dot(jnp.concatenate(score_rows, axis=0).astype(BF16), v[c0:c0 + cs, :])

                diag_rows = []
                for si in range(cs // sb):
                    r0 = c0 + si * sb
                    q_lo, q_hi = q[r0:r0 + half, :], q[r0 + half:r0 + sb, :]
                    c_lo, c_hi = cum[r0:r0 + half, :], cum[r0 + half:r0 + sb, :]
                    acc_lo = jnp.zeros((half, HEAD_DIM), F32)
                    acc_hi = jnp.zeros((half, HEAD_DIM), F32)
                    for ti in range(sb):
                        a_s = a_ref[r0 + ti:r0 + ti + 1, :]
                        v_s = v_ref[r0 + ti:r0 + ti + 1, :]
                        if ti < half:
                            d = c_lo - a_s
                            if ti > 0:
                                d = jnp.where(half_row >= ti, d, NEG_BIG)
                            acc_lo = acc_lo + jnp.sum(q_lo * jnp.exp2(d), axis=-1, keepdims=True) * v_s
                            d = c_hi - a_s
                        else:
                            d = c_hi - a_s
                            if ti > half:
                                d = jnp.where(half_row >= ti - half, d, NEG_BIG)
                        acc_hi = acc_hi + jnp.sum(q_hi * jnp.exp2(d), axis=-1, keepdims=True) * v_s
                    diag_rows += [acc_lo, acc_hi]
                outs.append(inter + off + jnp.concatenate(diag_rows, axis=0))
            st_ref[...] = st
            finish(rows, outs)
            return carry

        lax.fori_loop(0, lc // blk, general, 0)


def _hgrn(main3, fg3, col_blocks, lb_logits, norm_g, *, lc):
    b, s, _ = main3.shape
    nh, dh = NUM_HEADS, HEAD_DIM
    nl = lb_logits.shape[0]
    assert lc % (HGRN_BLOCK * HGRN_GROUP) == 0
    q_blk, i_blk, g_blk = col_blocks

    def col(first):
        return pl.BlockSpec((None, lc, dh), lambda i, h, j: (i, j, first + h))

    return pl.pallas_call(
        functools.partial(_hgrn_kernel, lc=lc),
        grid=(b, nh, s // lc),
        in_specs=[
            col(q_blk), col(0), col(i_blk), col(g_blk),
            pl.BlockSpec((nl, dh), lambda i, h, j: (0, h)),
            pl.BlockSpec((1, dh), lambda i, h, j: (0, 0)),
        ],
        out_specs=pl.BlockSpec((None, lc, dh), lambda i, h, j: (i, j, h)),
        out_shape=jax.ShapeDtypeStruct((b, s, nh * dh), BF16),
        scratch_shapes=[
            pltpu.VMEM((dh, dh), F32),
            pltpu.VMEM((lc, dh), BF16),
            pltpu.VMEM((lc, dh), F32),
            pltpu.VMEM((lc // HGRN_CHUNK, dh, dh), F32),
            pltpu.VMEM((lc // HGRN_CHUNK * SUBLANES, dh), F32),
            pltpu.VMEM((HGRN_BLOCK, dh), F32),
            pltpu.VMEM((HGRN_BLOCK, dh), F32),
        ],
        compiler_params=_params("parallel", "parallel", "arbitrary"),
        name="hgrn",
    )(main3, fg3, main3, main3, lb_logits, norm_g.reshape(1, dh))


def _merge_kernel(h_ref, gpre_ref, gpost_ref, ya_ref, yb_ref, wpa_ref, wpb_ref,
                  wga_ref, wgb_ref, wo_ref, o_ref, u_ref, acc_ref):
    j = pl.program_id(1)

    @pl.when(j == 0)
    def _():
        u_ref[...] = _rms(h_ref[...], gpre_ref[...]).astype(BF16)
        acc_ref[...] = jnp.zeros_like(acc_ref)

    u = u_ref[...]
    tn = wo_ref.shape[0]
    groups = [slice(c0, c0 + tn // MERGE_COL_GROUPS) for c0 in range(0, tn, tn // MERGE_COL_GROUPS)]
    stage1 = [(_dot(ya_ref[...], wpa_ref[:, c]), _dot(yb_ref[...], wpb_ref[:, c]),
               _dot(u, wga_ref[:, c]), _dot(u, wgb_ref[:, c])) for c in groups]
    merged = [(jax.nn.sigmoid(ga) * y_a + jax.nn.sigmoid(gb) * y_b).astype(BF16)
              for y_a, y_b, ga, gb in stage1]
    out = _dot(merged[0], wo_ref[groups[0], :])
    for mg, c in zip(merged[1:], groups[1:]):
        out += _dot(mg, wo_ref[c, :])
    acc_ref[...] += out

    @pl.when(j == pl.num_programs(1) - 1)
    def _():
        o_ref[...] = h_ref[...] + _rms(acc_ref[...], gpost_ref[...])


def _merge(h, g_pre, g_post, y_a, y_b, w_pa, w_pb, w_ga, w_gb, w_o, *, tm, tn):
    t, d = h.shape
    wa = y_a.shape[1]
    wb = y_b.shape[1]
    return pl.pallas_call(
        _merge_kernel,
        grid=(t // tm, d // tn),
        in_specs=[
            pl.BlockSpec((tm, d), lambda i, j: (i, 0)),
            pl.BlockSpec((1, d), lambda i, j: (0, 0)),
            pl.BlockSpec((1, d), lambda i, j: (0, 0)),
            pl.BlockSpec((tm, wa), lambda i, j: (i, 0)),
            pl.BlockSpec((tm, wb), lambda i, j: (i, 0)),
            pl.BlockSpec((wa, tn), lambda i, j: (0, j)),
            pl.BlockSpec((wb, tn), lambda i, j: (0, j)),
            pl.BlockSpec((d, tn), lambda i, j: (0, j)),
            pl.BlockSpec((d, tn), lambda i, j: (0, j)),
            pl.BlockSpec((tn, d), lambda i, j: (j, 0)),
        ],
        out_specs=pl.BlockSpec((tm, d), lambda i, j: (i, 0)),
        out_shape=jax.ShapeDtypeStruct((t, d), F32),
        scratch_shapes=[pltpu.VMEM((tm, d), BF16), pltpu.VMEM((tm, d), F32)],
        compiler_params=_params("parallel", "arbitrary"),
        name="merge",
    )(h, g_pre.reshape(1, d), g_post.reshape(1, d), y_a, y_b, w_pa, w_pb, w_ga, w_gb, w_o)


def _ple_kernel(h_ref, gpre_ref, gpost_ref, p_ref, wg_ref, wp_ref, o_ref):
    tm = h_ref.shape[0]
    groups = [slice(r0, r0 + tm // PLE_ROW_GROUPS) for r0 in range(0, tm, tm // PLE_ROW_GROUPS)]
    emb = [_dot(p_ref[r, :].astype(BF16), wp_ref[...]) for r in groups]
    gates = [_dot(_rms(h_ref[r, :], gpre_ref[...]).astype(BF16), wg_ref[...]) for r in groups]
    for r, e, g in zip(groups, emb, gates):
        o_ref[r, :] = h_ref[r, :] + _rms(jax.nn.sigmoid(g) * e, gpost_ref[...])


def _ple(h, g_pre, g_post, p2, w_g, w_p, *, tm):
    t, d = h.shape
    dp = p2.shape[1]
    return pl.pallas_call(
        _ple_kernel,
        grid=(t // tm,),
        in_specs=[
            pl.BlockSpec((tm, d), lambda i: (i, 0)),
            pl.BlockSpec((1, d), lambda i: (0, 0)),
            pl.BlockSpec((1, d), lambda i: (0, 0)),
            pl.BlockSpec((tm, dp), lambda i: (i, 0)),
            pl.BlockSpec((d, d), lambda i: (0, 0)),
            pl.BlockSpec((dp, d), lambda i: (0, 0)),
        ],
        out_specs=pl.BlockSpec((tm, d), lambda i: (i, 0)),
        out_shape=jax.ShapeDtypeStruct((t, d), F32),
        compiler_params=_params("parallel"),
        name="ple",
    )(h, g_pre.reshape(1, d), g_post.reshape(1, d), p2, w_g, w_p)


def _tile(n, want):
    t = min(n, want)
    while n % t:
        t //= 2
    return t


def kernel(x, p, ffn1_pre_g, ffn1_post_g, ffn1_w_gate, ffn1_w_up, ffn1_w_down, mix_pre_g, mix_post_g, mix_w_in, fox_f_bias, hgrn_lb_logits, hgrn_norm_g, mix_w_proj_fox, mix_w_proj_hgrn, mix_w_out, ffn2_pre_g, ffn2_post_g, ffn2_w_gate, ffn2_w_up, ffn2_w_down, ple_pre_g, ple_post_g, ple_w_gate, ple_w_proj):
    b, s, d = x.shape
    t = b * s
    depth = ffn1_pre_g.shape[0]
    assert depth == 1, "the HGRN2 lower bound is evaluated for a single layer"
    nh, dh = NUM_HEADS, HEAD_DIM
    width = nh * dh
    assert mix_w_in.shape[-1] == 3 * width + nh + 4 * width + 2 * d

    tm = _tile(t, 512)
    h = x.reshape(t, d)
    for i in range(depth):
        h = _ffn(h, ffn1_pre_g[i], ffn1_post_g[i], ffn1_w_gate[i].astype(BF16),
                 ffn1_w_up[i].astype(BF16), ffn1_w_down[i].astype(BF16),
                 tm=tm, tf=_tile(ffn1_w_gate.shape[-1], 512))

        w_in = mix_w_in[i]
        o_f = 3 * width
        o_b = o_f + nh
        o_g = o_b + 4 * width
        w_hgrn = w_in[:, o_b:o_g].reshape(d, 4, width)
        w_main = jnp.concatenate(
            [w_in[:, :o_f], w_hgrn[:, 0], w_hgrn[:, 2], w_hgrn[:, 3]], axis=1).astype(BF16)
        w_gate = jnp.concatenate(
            [w_hgrn[:, 1], w_in[:, o_f:o_b], jnp.zeros((d, dh - nh), F32)], axis=1).astype(BF16)
        col_scale = jnp.concatenate(
            [jnp.full((1, width), dh ** -0.5 * LOG2E, F32), jnp.ones((1, 5 * width), F32)], axis=1)
        w_ga = w_in[:, o_g:o_g + d].astype(BF16)
        w_gb = w_in[:, o_g + d:].astype(BF16)

        main, fg = _proj_in(h, mix_pre_g[i], w_main, col_scale, w_gate, tm=tm, tn=_tile(6 * width, 1024))
        main3 = main.reshape(b, s, 6 * width)
        fg3 = fg.reshape(b, s, width + dh)
        c = _fox_gate(fg3, nh, fox_f_bias[i], ts=_tile(s, 512))
        y_a = _fox_attn(main3, c.reshape(b, nh, 1, s), tq=_tile(s, 512))
        y_b = _hgrn(main3, fg3, (3 * nh, 4 * nh, 5 * nh), hgrn_lb_logits, hgrn_norm_g[i], lc=_tile(s, 2048))
        h = _merge(h, mix_pre_g[i], mix_post_g[i], y_a.reshape(t, width), y_b.reshape(t, width),
                   mix_w_proj_fox[i].astype(BF16), mix_w_proj_hgrn[i].astype(BF16),
                   w_ga, w_gb, mix_w_out[i].astype(BF16), tm=tm, tn=_tile(d, 512))

        h = _ffn(h, ffn2_pre_g[i], ffn2_post_g[i], ffn2_w_gate[i].astype(BF16),
                 ffn2_w_up[i].astype(BF16), ffn2_w_down[i].astype(BF16),
                 tm=tm, tf=_tile(ffn2_w_gate.shape[-1], 512))
        h = _ple(h, ple_pre_g[i], ple_post_g[i], p[i].reshape(t, -1),
                 ple_w_gate[i].astype(BF16), ple_w_proj[i].astype(BF16), tm=_tile(t, 512))
    return h.reshape(b, s, d)
```

```python
import functools
import math

import jax
import jax.numpy as jnp
from jax import lax
from jax.experimental import pallas as pl
from jax.experimental.pallas import tpu as pltpu

NORM_EPS = 1e-6
MACARON_SCALE = 0.5
LOG2E = math.log2(math.e)
HEAD_DIM = 128
NUM_HEADS = 8
HGRN_CHUNK = 64
HGRN_SUB = 16
HGRN_BLOCK = 256
HGRN_GROUP = 4
HGRN_SAFE_LOG2 = 100.0
FFN_COL_GROUPS = 2
MERGE_COL_GROUPS = 2
PLE_ROW_GROUPS = 2
NORM_ROWS = 16
SUBLANES = 8
NEG_BIG = -1e30

VMEM_LIMIT_BYTES = 56 * 1024 * 1024

BF16 = jnp.bfloat16
F32 = jnp.float32


def _params(*semantics):
    return pltpu.CompilerParams(dimension_semantics=semantics,
                                vmem_limit_bytes=VMEM_LIMIT_BYTES)


def _rms(x, g):
    ms = jnp.mean(x * x, axis=-1, keepdims=True)
    return x * lax.rsqrt(ms + NORM_EPS) * g


def _postnorm_residual_to(o_ref, x_ref, y_ref, g_ref, scale):
    for r0 in range(0, x_ref.shape[0], NORM_ROWS):
        rows = slice(r0, r0 + NORM_ROWS)
        o_ref[rows, :] = x_ref[rows, :] + scale * _rms(y_ref[rows, :], g_ref[...])


def _sigmoid(x):
    return 0.5 + 0.5 * jnp.tanh(0.5 * x)


def _silu(x):
    h = 0.5 * x
    return h + h * jnp.tanh(h)


def _dot(a, b):
    return jnp.dot(a, b, preferred_element_type=F32)


def _dot_nt(a, b):
    return lax.dot_general(a, b, (((1,), (1,)), ((), ())), preferred_element_type=F32)


def _split3(x):
    hi = x.astype(BF16)
    r = x - hi.astype(F32)
    mid = r.astype(BF16)
    lo = (r - mid.astype(F32)).astype(BF16)
    return hi, mid, lo


def _ffn_kernel(x_ref, gpre_ref, gpost_ref, wg_ref, wu_ref, wd_ref, o_ref, xn_ref):
    j = pl.program_id(1)

    @pl.when(j == 0)
    def _():
        xn_ref[...] = _rms(x_ref[...], gpre_ref[...]).astype(BF16)
        o_ref[...] = jnp.zeros_like(o_ref)

    xn = xn_ref[...]
    tf = wg_ref.shape[1]
    groups = [slice(c0, c0 + tf // FFN_COL_GROUPS) for c0 in range(0, tf, tf // FFN_COL_GROUPS)]
    gu = [(_dot(xn, wg_ref[:, c]), _dot(xn, wu_ref[:, c])) for c in groups]
    acts = [(g * jax.nn.sigmoid(g) * u).astype(BF16) for g, u in gu]
    down = _dot(acts[0], wd_ref[groups[0], :])
    for a, c in zip(acts[1:], groups[1:]):
        down += _dot(a, wd_ref[c, :])
    o_ref[...] += down

    @pl.when(j == pl.num_programs(1) - 1)
    def _():
        _postnorm_residual_to(o_ref, x_ref, o_ref, gpost_ref, MACARON_SCALE)


def _ffn(h, g_pre, g_post, w_gate, w_up, w_down, *, tm, tf):
    t, d = h.shape
    f = w_gate.shape[1]
    return pl.pallas_call(
        _ffn_kernel,
        grid=(t // tm, f // tf),
        in_specs=[
            pl.BlockSpec((tm, d), lambda i, j: (i, 0)),
            pl.BlockSpec((1, d), lambda i, j: (0, 0)),
            pl.BlockSpec((1, d), lambda i, j: (0, 0)),
            pl.BlockSpec((d, tf), lambda i, j: (0, j)),
            pl.BlockSpec((d, tf), lambda i, j: (0, j)),
            pl.BlockSpec((tf, d), lambda i, j: (j, 0)),
        ],
        out_specs=pl.BlockSpec((tm, d), lambda i, j: (i, 0)),
        out_shape=jax.ShapeDtypeStruct((t, d), F32),
        scratch_shapes=[pltpu.VMEM((tm, d), BF16)],
        compiler_params=_params("parallel", "arbitrary"),
        name="ffn",
    )(h, g_pre.reshape(1, d), g_post.reshape(1, d), w_gate, w_up, w_down)


PROJ_FOX_STEPS = 3
PROJ_REST_BLOCK = (0, 0, 0, 0, 2, 3, 1)


def _proj_in_kernel(x_ref, g_ref, wa_ref, wr_ref, wfa_ref, o_ref, of_ref, xn_ref, *, q_scale):
    j = pl.program_id(1)
    last = len(PROJ_REST_BLOCK) - 1

    @pl.when(j == 0)
    def _():
        xn_ref[...] = _rms(x_ref[...], g_ref[...]).astype(BF16)

    @pl.when(j < PROJ_FOX_STEPS)
    def _():
        y = _dot(xn_ref[...], wa_ref[...])
        o_ref[...] = (y * jnp.where(j == 0, q_scale, 1.0)).astype(o_ref.dtype)

    @pl.when((j >= PROJ_FOX_STEPS) & (j < last))
    def _():
        o_ref[...] = _dot(xn_ref[...], wr_ref[...]).astype(o_ref.dtype)

    @pl.when(j == last)
    def _():
        width = wr_ref.shape[1]
        of_ref[:, :width] = _dot(xn_ref[...], wr_ref[...])
        of_ref[:, width:] = _dot(xn_ref[...], wfa_ref[...])


def _proj_in(h, g, w_fox, w_rest, w_fa, *, tm, width, q_scale):
    t, d = h.shape
    assert w_fox.shape[1] == PROJ_FOX_STEPS * width
    n_main = len(PROJ_REST_BLOCK) - 1
    nf = width + w_fa.shape[1]

    def rest_col(i, j):
        blk = jnp.int32(PROJ_REST_BLOCK[-1])
        for step in range(n_main):
            blk = jnp.where(j == step, PROJ_REST_BLOCK[step], blk)
        return (0, blk)

    return pl.pallas_call(
        functools.partial(_proj_in_kernel, q_scale=q_scale),
        grid=(t // tm, n_main + 1),
        in_specs=[
            pl.BlockSpec((tm, d), lambda i, j: (i, 0)),
            pl.BlockSpec((1, d), lambda i, j: (0, 0)),
            pl.BlockSpec((d, width), lambda i, j: (0, jnp.minimum(j, PROJ_FOX_STEPS - 1))),
            pl.BlockSpec((d, width), rest_col),
            pl.BlockSpec((d, w_fa.shape[1]), lambda i, j: (0, 0)),
        ],
        out_specs=[
            pl.BlockSpec((tm, width), lambda i, j: (i, jnp.minimum(j, n_main - 1))),
            pl.BlockSpec((tm, nf), lambda i, j: (i, 0)),
        ],
        out_shape=[jax.ShapeDtypeStruct((t, n_main * width), BF16), jax.ShapeDtypeStruct((t, nf), F32)],
        scratch_shapes=[pltpu.VMEM((tm, d), BF16)],
        compiler_params=_params("parallel", "arbitrary"),
        name="proj_in",
    )(h, g.reshape(1, d), w_fox, w_rest, w_fa)


def _fox_gate_kernel(f_ref, b_ref, o_ref, carry_ref, *, ts):
    @pl.when(pl.program_id(1) == 0)
    def _():
        carry_ref[...] = jnp.zeros_like(carry_ref)

    z = f_ref[...].T[:NUM_HEADS, :] + b_ref[...]
    lf = (jnp.minimum(z, 0.0) - jnp.log1p(jnp.exp(-jnp.abs(z)))) * LOG2E
    src = lax.broadcasted_iota(jnp.int32, (ts, ts), 0)
    dst = lax.broadcasted_iota(jnp.int32, (ts, ts), 1)
    tri = jnp.where(src <= dst, 1.0, 0.0).astype(BF16)
    hi, mid, lo = _split3(lf)
    c = _dot(hi, tri) + _dot(mid, tri) + _dot(lo, tri) + carry_ref[:, :1]
    o_ref[...] = c
    carry_ref[...] = jnp.broadcast_to(c[:, ts - 1:ts], carry_ref.shape)


def _fox_gate(fg3, lane_block, bias, *, ts):
    b, s, _ = fg3.shape
    nh = NUM_HEADS
    return pl.pallas_call(
        functools.partial(_fox_gate_kernel, ts=ts),
        grid=(b, s // ts),
        in_specs=[
            pl.BlockSpec((None, ts, HEAD_DIM), lambda i, j: (i, j, lane_block)),
            pl.BlockSpec((nh, 1), lambda i, j: (0, 0)),
        ],
        out_specs=pl.BlockSpec((None, nh, ts), lambda i, j: (i, 0, j)),
        out_shape=jax.ShapeDtypeStruct((b, nh, s), F32),
        scratch_shapes=[pltpu.VMEM((nh, HEAD_DIM), F32)],
        compiler_params=_params("parallel", "arbitrary"),
        name="fox_gate",
    )(fg3, bias.reshape(nh, 1))


def _fox_attn_kernel(q_ref, k_ref, v_ref, c_ref, o_ref, m_ref, l_ref, acc_ref, *, tq):
    nq = q_ref.shape[0] // tq
    reps = tq // HEAD_DIM

    m_ref[...] = jnp.full_like(m_ref, NEG_BIG)
    l_ref[...] = jnp.zeros_like(l_ref)
    acc_ref[...] = jnp.zeros_like(acc_ref)

    def step(qi, ki):
        rows = slice(qi * tq, (qi + 1) * tq)
        cols = slice(ki * tq, (ki + 1) * tq)
        c0 = jnp.max(c_ref[:, rows], axis=-1, keepdims=True)
        s = _dot_nt(q_ref[rows, :], k_ref[cols, :]) + (c0 - c_ref[:, cols])
        if qi == ki:
            row = lax.broadcasted_iota(jnp.int32, (tq, tq), 0)
            col = lax.broadcasted_iota(jnp.int32, (tq, tq), 1)
            s = jnp.where(col <= row, s, NEG_BIG)
        m_prev = m_ref[rows, :]
        m_next = jnp.maximum(m_prev, jnp.max(s, axis=-1, keepdims=True))
        p = jnp.exp2(s - jnp.concatenate([m_next] * reps, axis=1))
        alpha = jnp.exp2(m_prev - m_next)
        l_ref[rows, :] = alpha * l_ref[rows, :] + jnp.sum(p, axis=-1, keepdims=True)
        acc_ref[rows, :] = alpha * acc_ref[rows, :] + _dot(p.astype(BF16), v_ref[cols, :])
        m_ref[rows, :] = m_next

    for diag in range(nq):
        for qi in range(diag, nq):
            step(qi, qi - diag)
    o_ref[...] = (acc_ref[...] / l_ref[...]).astype(o_ref.dtype)


def _fox_attn(main3, c4, *, tq):
    b, s, _ = main3.shape
    nh, dh = NUM_HEADS, HEAD_DIM
    return pl.pallas_call(
        functools.partial(_fox_attn_kernel, tq=tq),
        grid=(b, nh),
        in_specs=[
            pl.BlockSpec((None, s, dh), lambda i, h: (i, 0, h)),
            pl.BlockSpec((None, s, dh), lambda i, h: (i, 0, nh + h)),
            pl.BlockSpec((None, s, dh), lambda i, h: (i, 0, 2 * nh + h)),
            pl.BlockSpec((None, None, 1, s), lambda i, h: (i, h, 0, 0)),
        ],
        out_specs=pl.BlockSpec((None, s, dh), lambda i, h: (i, 0, h)),
        out_shape=jax.ShapeDtypeStruct((b, s, nh * dh), BF16),
        scratch_shapes=[pltpu.VMEM((s, dh), F32), pltpu.VMEM((s, dh), F32),
                        pltpu.VMEM((s, dh), F32)],
        compiler_params=_params("parallel", "parallel"),
        name="fox_attn",
    )(main3, main3, main3, c4)


def _hgrn_kernel(q_ref, f_ref, i_ref, g_ref, lbl_ref, ng_ref, o_ref,
                 st_ref, qe_ref, intra_ref, u_ref, dec_ref, a_ref, v_ref, *, lc):
    cs, sb, blk = HGRN_CHUNK, HGRN_SUB, HGRN_BLOCK
    half = sb // 2
    cpb = blk // cs

    @pl.when(pl.program_id(2) == 0)
    def _():
        st_ref[...] = jnp.zeros_like(st_ref)

    logits = lbl_ref[...]
    e = jnp.exp(logits - jnp.max(logits, axis=0, keepdims=True))
    lb = e[0:1, :] / jnp.sum(e, axis=0, keepdims=True)
    ng = ng_ref[...]

    src = lax.broadcasted_iota(jnp.int32, (blk, blk), 1)
    dst = lax.broadcasted_iota(jnp.int32, (blk, blk), 0)
    same_chunk = (src // cs) == (dst // cs)
    tri = jnp.where(same_chunk & (src <= dst), 1.0, 0.0).astype(BF16)
    half_row = lax.broadcasted_iota(jnp.int32, (half, HEAD_DIM), 0)
    blk_col = lax.broadcasted_iota(jnp.int32, (sb, cs), 1)
    chunk_row = lax.broadcasted_iota(jnp.int32, (cs, cs), 0)
    chunk_col = lax.broadcasted_iota(jnp.int32, (cs, cs), 1)

    def prep(rows):
        q = _silu(q_ref[rows, :].astype(F32))
        f = lb + (1.0 - lb) * _sigmoid(f_ref[rows, :])
        kk = 1.0 - f
        hi, mid, lo = _split3(jnp.log2(f))
        cum = _dot(tri, hi) + _dot(tri, mid) + _dot(tri, lo)
        lasts = [cum[c0 + cs - 1:c0 + cs, :] for c0 in range(0, blk, cs)]
        return q, kk, cum, lasts

    def state_increment(ci, kk, cum, lasts, v):
        c0 = ci * cs
        kdec = (kk[c0:c0 + cs, :] * jnp.exp2(lasts[ci] - cum[c0:c0 + cs, :])).astype(BF16)
        return _dot(v[c0:c0 + cs, :].T.astype(BF16), kdec)

    def finish(rows, outs):
        o = _rms(jnp.concatenate(outs, axis=0), ng) * _silu(g_ref[rows, :].astype(F32))
        o_ref[rows, :] = o.astype(o_ref.dtype)

    def precompute(gi, deepest):
        blocks = [gi * HGRN_GROUP + k for k in range(HGRN_GROUP)]
        rows = [pl.ds(pl.multiple_of(bi * blk, blk), blk) for bi in blocks]
        stage1 = [prep(r) for r in rows]
        stage2 = []
        for bi, r, (q, kk, cum, lasts) in zip(blocks, rows, stage1):
            v = i_ref[r, :]
            v32 = v.astype(F32)
            qe = (q * jnp.exp2(cum)).astype(BF16)
            kinv = (kk * jnp.exp2(-cum)).astype(BF16)
            qe_ref[r, :] = qe
            scores = [_dot_nt(qe[c0:c0 + cs, :], kinv[c0:c0 + cs, :]) for c0 in range(0, blk, cs)]
            for ci in range(cpb):
                u_ref[bi * cpb + ci] = state_increment(ci, kk, cum, lasts, v32)
                dec_ref[pl.ds(pl.multiple_of((bi * cpb + ci) * SUBLANES, SUBLANES), SUBLANES), :] = (
                    jnp.broadcast_to(jnp.exp2(lasts[ci]), (SUBLANES, HEAD_DIM)))
                deepest = jnp.maximum(deepest, -lasts[ci])
            stage2.append((v, scores))
        for r, (v, scores) in zip(rows, stage2):
            intra = [_dot(jnp.where(chunk_col <= chunk_row, sc, 0.0).astype(BF16), v[ci * cs:(ci + 1) * cs, :])
                     for ci, sc in enumerate(scores)]
            intra_ref[r, :] = jnp.concatenate(intra, axis=0)
        return deepest

    deepest = lax.fori_loop(0, lc // (blk * HGRN_GROUP), precompute, jnp.zeros((1, HEAD_DIM), F32))
    depth = jnp.max(deepest)

    @pl.when(depth <= HGRN_SAFE_LOG2)
    def _():
        def recur(bi, carry):
            base = pl.multiple_of(bi * blk, blk)
            st = st_ref[...]
            outs = []
            for ci in range(cpb):
                rows_c = pl.ds(base + ci * cs, cs)
                dec = dec_ref[pl.ds(pl.multiple_of((bi * cpb + ci) * SUBLANES, SUBLANES), SUBLANES), :]
                outs.append(_dot_nt(qe_ref[rows_c, :], st.astype(BF16)) + intra_ref[rows_c, :])
                st = st * dec[0:1, :] + u_ref[bi * cpb + ci]
            st_ref[...] = st
            finish(pl.ds(base, blk), outs)
            return carry

        lax.fori_loop(0, lc // blk, recur, 0, unroll=4)

    @pl.when(jnp.logical_not(depth <= HGRN_SAFE_LOG2))
    def _():
        def general(bi, carry):
            rows = pl.ds(pl.multiple_of(bi * blk, blk), blk)
            q, kk, cum, lasts = prep(rows)
            v = i_ref[rows, :]
            v32 = v.astype(F32)
            a_ref[...] = cum - jnp.log2(kk)
            v_ref[...] = v32
            st = st_ref[...]
            outs = []
            for ci in range(cpb):
                c0 = ci * cs
                cum_c = cum[c0:c0 + cs, :]
                q_c = q[c0:c0 + cs, :]
                kk_c = kk[c0:c0 + cs, :]
                inter = _dot_nt((q_c * jnp.exp2(cum_c)).astype(BF16), st.astype(BF16))
                st = st * jnp.exp2(lasts[ci]) + state_increment(ci, kk, cum, lasts, v32)

                score_rows = [jnp.zeros((sb, cs), F32)]
                for si in range(1, cs // sb):
                    r0 = si * sb
                    ref = cum_c[r0 - 1:r0, :]
                    qt = q_c[r0:r0 + sb, :] * jnp.exp2(cum_c[r0:r0 + sb, :] - ref)
                    kt = kk_c * jnp.exp2(jnp.minimum(ref - cum_c, 0.0))
                    sc = _dot_nt(qt.astype(BF16), kt.astype(BF16))
                    score_rows.append(jnp.where(blk_col < r0, sc, 0.0))
                off = _dot(jnp.concatenate(score_rows, axis=0).astype(BF16), v[c0:c0 + cs, :])

                diag_rows = []
                for si in range(cs // sb):
                    r0 = c0 + si * sb
                    q_lo, q_hi = q[r0:r0 + half, :], q[r0 + half:r0 + sb, :]
                    c_lo, c_hi = cum[r0:r0 + half, :], cum[r0 + half:r0 + sb, :]
                    acc_lo = jnp.zeros((half, HEAD_DIM), F32)
                    acc_hi = jnp.zeros((half, HEAD_DIM), F32)
                    for ti in range(sb):
                        a_s = a_ref[r0 + ti:r0 + ti + 1, :]
                        v_s = v_ref[r0 + ti:r0 + ti + 1, :]
                        if ti < half:
                            d = c_lo - a_s
                            if ti > 0:
                                d = jnp.where(half_row >= ti, d, NEG_BIG)
                            acc_lo = acc_lo + jnp.sum(q_lo * jnp.exp2(d), axis=-1, keepdims=True) * v_s
                            d = c_hi - a_s
                        else:
                            d = c_hi - a_s
                            if ti > half:
                                d = jnp.where(half_row >= ti - half, d, NEG_BIG)
                        acc_hi = acc_hi + jnp.sum(q_hi * jnp.exp2(d), axis=-1, keepdims=True) * v_s
                    diag_rows += [acc_lo, acc_hi]
                outs.append(inter + off + jnp.concatenate(diag_rows, axis=0))
            st_ref[...] = st
            finish(rows, outs)
            return carry

        lax.fori_loop(0, lc // blk, general, 0)


def _hgrn(main3, fg3, col_blocks, lb_logits, norm_g, *, lc):
    b, s, _ = main3.shape
    nh, dh = NUM_HEADS, HEAD_DIM
    nl = lb_logits.shape[0]
    assert lc % (HGRN_BLOCK * HGRN_GROUP) == 0
    q_blk, i_blk, g_blk = col_blocks

    def col(first):
        return pl.BlockSpec((None, lc, dh), lambda i, h, j: (i, j, first + h))

    return pl.pallas_call(
        functools.partial(_hgrn_kernel, lc=lc),
        grid=(b, nh, s // lc),
        in_specs=[
            col(q_blk), col(0), col(i_blk), col(g_blk),
            pl.BlockSpec((nl, dh), lambda i, h, j: (0, h)),
            pl.BlockSpec((1, dh), lambda i, h, j: (0, 0)),
        ],
        out_specs=pl.BlockSpec((None, lc, dh), lambda i, h, j: (i, j, h)),
        out_shape=jax.ShapeDtypeStruct((b, s, nh * dh), BF16),
        scratch_shapes=[
            pltpu.VMEM((dh, dh), F32),
            pltpu.VMEM((lc, dh), BF16),
            pltpu.VMEM((lc, dh), F32),
            pltpu.VMEM((lc // HGRN_CHUNK, dh, dh), F32),
            pltpu.VMEM((lc // HGRN_CHUNK * SUBLANES, dh), F32),
            pltpu.VMEM((HGRN_BLOCK, dh), F32),
            pltpu.VMEM((HGRN_BLOCK, dh), F32),
        ],
        compiler_params=_params("parallel", "parallel", "arbitrary"),
        name="hgrn",
    )(main3, fg3, main3, main3, lb_logits, norm_g.reshape(1, dh))


def _merge_kernel(h_ref, gpre_ref, gpost_ref, ya_ref, yb_ref, wpa_ref, wpb_ref,
                  wga_ref, wgb_ref, wo_ref, o_ref, u_ref):
    j = pl.program_id(1)

    @pl.when(j == 0)
    def _():
        u_ref[...] = _rms(h_ref[...], gpre_ref[...]).astype(BF16)
        o_ref[...] = jnp.zeros_like(o_ref)

    u = u_ref[...]
    tn = wo_ref.shape[0]
    groups = [slice(c0, c0 + tn // MERGE_COL_GROUPS) for c0 in range(0, tn, tn // MERGE_COL_GROUPS)]
    stage1 = [(_dot(ya_ref[...], wpa_ref[:, c]), _dot(yb_ref[...], wpb_ref[:, c]),
               _dot(u, wga_ref[:, c]), _dot(u, wgb_ref[:, c])) for c in groups]
    merged = [(jax.nn.sigmoid(ga) * y_a + jax.nn.sigmoid(gb) * y_b).astype(BF16)
              for y_a, y_b, ga, gb in stage1]
    out = _dot(merged[0], wo_ref[groups[0], :])
    for mg, c in zip(merged[1:], groups[1:]):
        out += _dot(mg, wo_ref[c, :])
    o_ref[...] += out

    @pl.when(j == pl.num_programs(1) - 1)
    def _():
        _postnorm_residual_to(o_ref, h_ref, o_ref, gpost_ref, 1.0)


def _merge(h, g_pre, g_post, y_a, y_b, w_pa, w_pb, w_gates, gate_col, w_o, *, tm, tn):
    t, d = h.shape
    wa = y_a.shape[1]
    wb = y_b.shape[1]
    ga_blk = gate_col // tn
    gb_blk = (gate_col + d) // tn
    return pl.pallas_call(
        _merge_kernel,
        grid=(t // tm, d // tn),
        in_specs=[
            pl.BlockSpec((tm, d), lambda i, j: (i, 0)),
            pl.BlockSpec((1, d), lambda i, j: (0, 0)),
            pl.BlockSpec((1, d), lambda i, j: (0, 0)),
            pl.BlockSpec((tm, wa), lambda i, j: (i, 0)),
            pl.BlockSpec((tm, wb), lambda i, j: (i, 0)),
            pl.BlockSpec((wa, tn), lambda i, j: (0, j)),
            pl.BlockSpec((wb, tn), lambda i, j: (0, j)),
            pl.BlockSpec((d, tn), lambda i, j: (0, ga_blk + j)),
            pl.BlockSpec((d, tn), lambda i, j: (0, gb_blk + j)),
            pl.BlockSpec((tn, d), lambda i, j: (j, 0)),
        ],
        out_specs=pl.BlockSpec((tm, d), lambda i, j: (i, 0)),
        out_shape=jax.ShapeDtypeStruct((t, d), F32),
        scratch_shapes=[pltpu.VMEM((tm, d), BF16)],
        compiler_params=_params("parallel", "arbitrary"),
        name="merge",
    )(h, g_pre.reshape(1, d), g_post.reshape(1, d), y_a, y_b, w_pa, w_pb, w_gates, w_gates, w_o)


def _ple_kernel(h_ref, gpre_ref, gpost_ref, p_ref, wg_ref, wp_ref, o_ref):
    tm = h_ref.shape[0]
    groups = [slice(r0, r0 + tm // PLE_ROW_GROUPS) for r0 in range(0, tm, tm // PLE_ROW_GROUPS)]
    emb = [_dot(p_ref[r, :].astype(BF16), wp_ref[...]) for r in groups]
    gates = [_dot(_rms(h_ref[r, :], gpre_ref[...]).astype(BF16), wg_ref[...]) for r in groups]
    for r, e, g in zip(groups, emb, gates):
        o_ref[r, :] = h_ref[r, :] + _rms(jax.nn.sigmoid(g) * e, gpost_ref[...])


def _ple(h, g_pre, g_post, p2, w_g, w_p, *, tm):
    t, d = h.shape
    dp = p2.shape[1]
    return pl.pallas_call(
        _ple_kernel,
        grid=(t // tm,),
        in_specs=[
            pl.BlockSpec((tm, d), lambda i: (i, 0)),
            pl.BlockSpec((1, d), lambda i: (0, 0)),
            pl.BlockSpec((1, d), lambda i: (0, 0)),
            pl.BlockSpec((tm, dp), lambda i: (i, 0)),
            pl.BlockSpec((d, d), lambda i: (0, 0)),
            pl.BlockSpec((dp, d), lambda i: (0, 0)),
        ],
        out_specs=pl.BlockSpec((tm, d), lambda i: (i, 0)),
        out_shape=jax.ShapeDtypeStruct((t, d), F32),
        compiler_params=_params("parallel"),
        name="ple",
    )(h, g_pre.reshape(1, d), g_post.reshape(1, d), p2, w_g, w_p)


def _tile(n, want):
    t = min(n, want)
    while n % t:
        t //= 2
    return t


def kernel(x, p, ffn1_pre_g, ffn1_post_g, ffn1_w_gate, ffn1_w_up, ffn1_w_down, mix_pre_g, mix_post_g, mix_w_in, fox_f_bias, hgrn_lb_logits, hgrn_norm_g, mix_w_proj_fox, mix_w_proj_hgrn, mix_w_out, ffn2_pre_g, ffn2_post_g, ffn2_w_gate, ffn2_w_up, ffn2_w_down, ple_pre_g, ple_post_g, ple_w_gate, ple_w_proj):
    b, s, d = x.shape
    t = b * s
    depth = ffn1_pre_g.shape[0]
    assert depth == 1, "the HGRN2 lower bound is evaluated for a single layer"
    nh, dh = NUM_HEADS, HEAD_DIM
    width = nh * dh
    assert mix_w_in.shape[-1] == 3 * width + nh + 4 * width + 2 * d

    tm = _tile(t, 512)
    tm_ffn = _tile(t, 1024)
    h = x.reshape(t, d)
    for i in range(depth):
        h = _ffn(h, ffn1_pre_g[i], ffn1_post_g[i], ffn1_w_gate[i].astype(BF16),
                 ffn1_w_up[i].astype(BF16), ffn1_w_down[i].astype(BF16),
                 tm=tm_ffn, tf=_tile(ffn1_w_gate.shape[-1], 512))

        w_in = mix_w_in[i]
        o_f = 3 * width
        o_b = o_f + nh
        w_fox = w_in[:, :o_f].astype(BF16)
        w_fa = jnp.pad(w_in[:, o_f:o_b], ((0, 0), (0, dh - nh))).astype(BF16)
        w_rest = w_in[:, o_b:].astype(BF16)

        main, fg = _proj_in(h, mix_pre_g[i], w_fox, w_rest, w_fa, tm=tm_ffn, width=width,
                            q_scale=dh ** -0.5 * LOG2E)
        main3 = main.reshape(b, s, 6 * width)
        fg3 = fg.reshape(b, s, width + dh)
        c = _fox_gate(fg3, nh, fox_f_bias[i], ts=_tile(s, 512))
        y_a = _fox_attn(main3, c.reshape(b, nh, 1, s), tq=_tile(s, 512))
        y_b = _hgrn(main3, fg3, (3 * nh, 4 * nh, 5 * nh), hgrn_lb_logits, hgrn_norm_g[i], lc=_tile(s, 2048))
        h = _merge(h, mix_pre_g[i], mix_post_g[i], y_a.reshape(t, width), y_b.reshape(t, width),
                   mix_w_proj_fox[i].astype(BF16), mix_w_proj_hgrn[i].astype(BF16),
                   w_rest, 4 * width, mix_w_out[i].astype(BF16), tm=tm, tn=_tile(d, 512))

        h = _ffn(h, ffn2_pre_g[i], ffn2_post_g[i], ffn2_w_gate[i].astype(BF16),
                 ffn2_w_up[i].astype(BF16), ffn2_w_down[i].astype(BF16),
                 tm=tm_ffn, tf=_tile(ffn2_w_gate.shape[-1], 512))
        h = _ple(h, ple_pre_g[i], ple_post_g[i], p[i].reshape(t, -1),
                 ple_w_gate[i].astype(BF16), ple_w_proj[i].astype(BF16), tm=_tile(t, 512))
    return h.reshape(b, s, d)
```

```python
import functools
import math

import jax
import jax.numpy as jnp
from jax import lax
from jax.experimental import pallas as pl
from jax.experimental.pallas import tpu as pltpu

NORM_EPS = 1e-6
MACARON_SCALE = 0.5
LOG2E = math.log2(math.e)
HEAD_DIM = 128
NUM_HEADS = 8
HGRN_CHUNK = 64
HGRN_SUB = 16
HGRN_BLOCK = 256
HGRN_GROUP = 4
HGRN_SAFE_LOG2 = 100.0
FFN_COL_GROUPS = 2
MERGE_COL_GROUPS = 2
PLE_ROW_GROUPS = 2
NORM_ROWS = 16
SUBLANES = 8
NEG_BIG = -1e30

VMEM_LIMIT_BYTES = 60 * 1024 * 1024

BF16 = jnp.bfloat16
F32 = jnp.float32


def _params(*semantics):
    return pltpu.CompilerParams(dimension_semantics=semantics,
                                vmem_limit_bytes=VMEM_LIMIT_BYTES)


def _rms(x, g):
    ms = jnp.mean(x * x, axis=-1, keepdims=True)
    return x * lax.rsqrt(ms + NORM_EPS) * g


def _postnorm_residual_to(o_ref, x_ref, y_ref, g_ref, scale):
    g = g_ref[...] if scale == 1.0 else scale * g_ref[...]
    for r0 in range(0, x_ref.shape[0], NORM_ROWS):
        rows = slice(r0, r0 + NORM_ROWS)
        o_ref[rows, :] = x_ref[rows, :] + _rms(y_ref[rows, :], g)


def _sigmoid(x):
    return 0.5 + 0.5 * jnp.tanh(0.5 * x)


def _silu(x):
    h = 0.5 * x
    return h + h * jnp.tanh(h)


def _dot(a, b):
    return jnp.dot(a, b, preferred_element_type=F32)


def _dot_nt(a, b):
    return lax.dot_general(a, b, (((1,), (1,)), ((), ())), preferred_element_type=F32)


def _split3(x):
    hi = x.astype(BF16)
    r = x - hi.astype(F32)
    mid = r.astype(BF16)
    lo = (r - mid.astype(F32)).astype(BF16)
    return hi, mid, lo


def _ffn_kernel(x_ref, gpre_ref, gpost_ref, wg_ref, wu_ref, wd_ref, o_ref, xn_ref):
    j = pl.program_id(1)

    @pl.when(j == 0)
    def _():
        xn_ref[...] = _rms(x_ref[...], gpre_ref[...]).astype(BF16)
        o_ref[...] = jnp.zeros_like(o_ref)

    xn = xn_ref[...]
    tf = wg_ref.shape[1]
    groups = [slice(c0, c0 + tf // FFN_COL_GROUPS) for c0 in range(0, tf, tf // FFN_COL_GROUPS)]
    gu = [(_dot(xn, wg_ref[:, c]), _dot(xn, wu_ref[:, c])) for c in groups]
    acts = [(g * jax.nn.sigmoid(g) * u).astype(BF16) for g, u in gu]
    down = _dot(acts[0], wd_ref[groups[0], :])
    for a, c in zip(acts[1:], groups[1:]):
        down += _dot(a, wd_ref[c, :])
    o_ref[...] += down

    @pl.when(j == pl.num_programs(1) - 1)
    def _():
        _postnorm_residual_to(o_ref, x_ref, o_ref, gpost_ref, MACARON_SCALE)


def _ffn(h, g_pre, g_post, w_gate, w_up, w_down, *, tm, tf):
    t, d = h.shape
    f = w_gate.shape[1]
    return pl.pallas_call(
        _ffn_kernel,
        grid=(t // tm, f // tf),
        in_specs=[
            pl.BlockSpec((tm, d), lambda i, j: (i, 0)),
            pl.BlockSpec((1, d), lambda i, j: (0, 0)),
            pl.BlockSpec((1, d), lambda i, j: (0, 0)),
            pl.BlockSpec((d, tf), lambda i, j: (0, j)),
            pl.BlockSpec((d, tf), lambda i, j: (0, j)),
            pl.BlockSpec((tf, d), lambda i, j: (j, 0)),
        ],
        out_specs=pl.BlockSpec((tm, d), lambda i, j: (i, 0)),
        out_shape=jax.ShapeDtypeStruct((t, d), F32),
        scratch_shapes=[pltpu.VMEM((tm, d), BF16)],
        compiler_params=_params("parallel", "arbitrary"),
        name="ffn",
    )(h, g_pre.reshape(1, d), g_post.reshape(1, d), w_gate, w_up, w_down)


PROJ_FOX_STEPS = 3
PROJ_MAIN_STEPS = 6


def _proj_in_kernel(x_ref, g_ref, wa_ref, wr_ref, wfa_ref, o_ref, og_ref, of_ref, xn_ref, *, q_scale):
    j = pl.program_id(1)
    last = pl.num_programs(1) - 1

    @pl.when(j == 0)
    def _():
        xn_ref[...] = _rms(x_ref[...], g_ref[...]).astype(BF16)

    @pl.when(j < PROJ_FOX_STEPS)
    def _():
        y = _dot(xn_ref[...], wa_ref[...])
        o_ref[...] = (y * jnp.where(j == 0, q_scale, 1.0)).astype(o_ref.dtype)

    @pl.when((j >= PROJ_FOX_STEPS) & (j < PROJ_MAIN_STEPS))
    def _():
        o_ref[...] = _dot(xn_ref[...], wr_ref[...]).astype(o_ref.dtype)

    @pl.when((j >= PROJ_MAIN_STEPS) & (j < last))
    def _():
        og_ref[...] = _sigmoid(_dot(xn_ref[...], wr_ref[...])).astype(og_ref.dtype)

    @pl.when(j == last)
    def _():
        width = wr_ref.shape[1]
        of_ref[:, :width] = _dot(xn_ref[...], wr_ref[...])
        of_ref[:, width:] = _dot(xn_ref[...], wfa_ref[...])


def _proj_in(h, g, w_fox, w_rest, w_fa, *, tm, width, gate_width, q_scale):
    t, d = h.shape
    assert w_fox.shape[1] >= PROJ_FOX_STEPS * width
    assert w_rest.shape[1] == 4 * width + gate_width and gate_width % width == 0
    n_gate = gate_width // width
    last = PROJ_MAIN_STEPS + n_gate
    nf = width + w_fa.shape[1]

    def rest_col(i, j):
        return (0, jnp.where(j <= 3, 0, jnp.where(j == last, 1, j - 2)))

    def gate_step(j):
        return jnp.clip(j - PROJ_MAIN_STEPS, 0, n_gate - 1)

    return pl.pallas_call(
        functools.partial(_proj_in_kernel, q_scale=q_scale),
        grid=(t // tm, last + 1),
        in_specs=[
            pl.BlockSpec((tm, d), lambda i, j: (i, 0)),
            pl.BlockSpec((1, d), lambda i, j: (0, 0)),
            pl.BlockSpec((d, width), lambda i, j: (0, jnp.minimum(j, PROJ_FOX_STEPS - 1))),
            pl.BlockSpec((d, width), rest_col),
            pl.BlockSpec((d, w_fa.shape[1]), lambda i, j: (0, 0)),
        ],
        out_specs=[
            pl.BlockSpec((tm, width), lambda i, j: (i, jnp.minimum(j, PROJ_MAIN_STEPS - 1))),
            pl.BlockSpec((tm, width), lambda i, j: (i, gate_step(j))),
            pl.BlockSpec((tm, nf), lambda i, j: (i, 0)),
        ],
        out_shape=[jax.ShapeDtypeStruct((t, PROJ_MAIN_STEPS * width), BF16),
                   jax.ShapeDtypeStruct((t, gate_width), BF16),
                   jax.ShapeDtypeStruct((t, nf), F32)],
        scratch_shapes=[pltpu.VMEM((tm, d), BF16)],
        compiler_params=_params("parallel", "arbitrary"),
        name="proj_in",
    )(h, g.reshape(1, d), w_fox, w_rest, w_fa)


def _fox_gate_kernel(f_ref, b_ref, o_ref, carry_ref, *, ts):
    @pl.when(pl.program_id(1) == 0)
    def _():
        carry_ref[...] = jnp.zeros_like(carry_ref)

    z = f_ref[...].T[:NUM_HEADS, :] + b_ref[...]
    lf = (jnp.minimum(z, 0.0) - jnp.log1p(jnp.exp(-jnp.abs(z)))) * LOG2E
    src = lax.broadcasted_iota(jnp.int32, (ts, ts), 0)
    dst = lax.broadcasted_iota(jnp.int32, (ts, ts), 1)
    tri = jnp.where(src <= dst, 1.0, 0.0).astype(BF16)
    hi, mid, lo = _split3(lf)
    c = _dot(hi, tri) + _dot(mid, tri) + _dot(lo, tri) + carry_ref[:, :1]
    o_ref[...] = c
    carry_ref[...] = jnp.broadcast_to(c[:, ts - 1:ts], carry_ref.shape)


def _fox_gate(fg3, lane_block, bias, *, ts):
    b, s, _ = fg3.shape
    nh = NUM_HEADS
    return pl.pallas_call(
        functools.partial(_fox_gate_kernel, ts=ts),
        grid=(b, s // ts),
        in_specs=[
            pl.BlockSpec((None, ts, HEAD_DIM), lambda i, j: (i, j, lane_block)),
            pl.BlockSpec((nh, 1), lambda i, j: (0, 0)),
        ],
        out_specs=pl.BlockSpec((None, nh, ts), lambda i, j: (i, 0, j)),
        out_shape=jax.ShapeDtypeStruct((b, nh, s), F32),
        scratch_shapes=[pltpu.VMEM((nh, HEAD_DIM), F32)],
        compiler_params=_params("parallel", "arbitrary"),
        name="fox_gate",
    )(fg3, bias.reshape(nh, 1))


def _fox_attn_kernel(q_ref, k_ref, v_ref, c_ref, o_ref, m_ref, l_ref, acc_ref, *, tq):
    nq = q_ref.shape[0] // tq
    reps = tq // HEAD_DIM

    m_ref[...] = jnp.full_like(m_ref, NEG_BIG)
    l_ref[...] = jnp.zeros_like(l_ref)
    acc_ref[...] = jnp.zeros_like(acc_ref)

    def step(qi, ki):
        rows = slice(qi * tq, (qi + 1) * tq)
        cols = slice(ki * tq, (ki + 1) * tq)
        c0 = jnp.max(c_ref[:, rows], axis=-1, keepdims=True)
        s = _dot_nt(q_ref[rows, :], k_ref[cols, :]) + (c0 - c_ref[:, cols])
        if qi == ki:
            row = lax.broadcasted_iota(jnp.int32, (tq, tq), 0)
            col = lax.broadcasted_iota(jnp.int32, (tq, tq), 1)
            s = jnp.where(col <= row, s, NEG_BIG)
        m_prev = m_ref[rows, :]
        m_next = jnp.maximum(m_prev, jnp.max(s, axis=-1, keepdims=True))
        p = jnp.exp2(s - jnp.concatenate([m_next] * reps, axis=1))
        alpha = jnp.exp2(m_prev - m_next)
        l_ref[rows, :] = alpha * l_ref[rows, :] + jnp.sum(p, axis=-1, keepdims=True)
        acc_ref[rows, :] = alpha * acc_ref[rows, :] + _dot(p.astype(BF16), v_ref[cols, :])
        m_ref[rows, :] = m_next

    for diag in range(nq):
        for qi in range(diag, nq):
            step(qi, qi - diag)
    o_ref[...] = (acc_ref[...] / l_ref[...]).astype(o_ref.dtype)


def _fox_attn(main3, c4, *, tq):
    b, s, _ = main3.shape
    nh, dh = NUM_HEADS, HEAD_DIM
    return pl.pallas_call(
        functools.partial(_fox_attn_kernel, tq=tq),
        grid=(b, nh),
        in_specs=[
            pl.BlockSpec((None, s, dh), lambda i, h: (i, 0, h)),
            pl.BlockSpec((None, s, dh), lambda i, h: (i, 0, nh + h)),
            pl.BlockSpec((None, s, dh), lambda i, h: (i, 0, 2 * nh + h)),
            pl.BlockSpec((None, None, 1, s), lambda i, h: (i, h, 0, 0)),
        ],
        out_specs=pl.BlockSpec((None, s, dh), lambda i, h: (i, 0, h)),
        out_shape=jax.ShapeDtypeStruct((b, s, nh * dh), BF16),
        scratch_shapes=[pltpu.VMEM((s, dh), F32), pltpu.VMEM((s, dh), F32),
                        pltpu.VMEM((s, dh), F32)],
        compiler_params=_params("parallel", "parallel"),
        name="fox_attn",
    )(main3, main3, main3, c4)


def _hgrn_kernel(q_ref, f_ref, i_ref, g_ref, lbl_ref, ng_ref, o_ref,
                 st_ref, qe_ref, intra_ref, u_ref, dec_ref, a_ref, v_ref, *, lc):
    cs, sb, blk = HGRN_CHUNK, HGRN_SUB, HGRN_BLOCK
    half = sb // 2
    cpb = blk // cs

    @pl.when(pl.program_id(2) == 0)
    def _():
        st_ref[...] = jnp.zeros_like(st_ref)

    logits = lbl_ref[...]
    e = jnp.exp(logits - jnp.max(logits, axis=0, keepdims=True))
    lb = e[0:1, :] / jnp.sum(e, axis=0, keepdims=True)
    ng = ng_ref[...]

    src = lax.broadcasted_iota(jnp.int32, (blk, blk), 1)
    dst = lax.broadcasted_iota(jnp.int32, (blk, blk), 0)
    same_chunk = (src // cs) == (dst // cs)
    tri = jnp.where(same_chunk & (src <= dst), 1.0, 0.0).astype(BF16)
    half_row = lax.broadcasted_iota(jnp.int32, (half, HEAD_DIM), 0)
    blk_col = lax.broadcasted_iota(jnp.int32, (sb, cs), 1)
    chunk_row = lax.broadcasted_iota(jnp.int32, (cs, cs), 0)
    chunk_col = lax.broadcasted_iota(jnp.int32, (cs, cs), 1)

    def prep(rows):
        q = _silu(q_ref[rows, :].astype(F32))
        f = lb + (1.0 - lb) * _sigmoid(f_ref[rows, :])
        kk = 1.0 - f
        hi, mid, lo = _split3(jnp.log2(f))
        cum = _dot(tri, hi) + _dot(tri, mid) + _dot(tri, lo)
        lasts = [cum[c0 + cs - 1:c0 + cs, :] for c0 in range(0, blk, cs)]
        return q, kk, cum, lasts

    def state_increment(ci, kk, cum, lasts, v):
        c0 = ci * cs
        kdec = (kk[c0:c0 + cs, :] * jnp.exp2(lasts[ci] - cum[c0:c0 + cs, :])).astype(BF16)
        return _dot(v[c0:c0 + cs, :].T.astype(BF16), kdec)

    def finish(rows, outs):
        o = _rms(jnp.concatenate(outs, axis=0), ng) * _silu(g_ref[rows, :].astype(F32))
        o_ref[rows, :] = o.astype(o_ref.dtype)

    def precompute(gi, deepest):
        blocks = [gi * HGRN_GROUP + k for k in range(HGRN_GROUP)]
        rows = [pl.ds(pl.multiple_of(bi * blk, blk), blk) for bi in blocks]
        stage1 = [prep(r) for r in rows]
        stage2 = []
        for bi, r, (q, kk, cum, lasts) in zip(blocks, rows, stage1):
            v = i_ref[r, :]
            v32 = v.astype(F32)
            qe = (q * jnp.exp2(cum)).astype(BF16)
            kinv = (kk * jnp.exp2(-cum)).astype(BF16)
            qe_ref[r, :] = qe
            scores = [_dot_nt(qe[c0:c0 + cs, :], kinv[c0:c0 + cs, :]) for c0 in range(0, blk, cs)]
            for ci in range(cpb):
                u_ref[bi * cpb + ci] = state_increment(ci, kk, cum, lasts, v32)
                dec_ref[pl.ds(pl.multiple_of((bi * cpb + ci) * SUBLANES, SUBLANES), SUBLANES), :] = (
                    jnp.broadcast_to(jnp.exp2(lasts[ci]), (SUBLANES, HEAD_DIM)))
                deepest = jnp.maximum(deepest, -lasts[ci])
            stage2.append((v, scores))
        for r, (v, scores) in zip(rows, stage2):
            intra = [_dot(jnp.where(chunk_col <= chunk_row, sc, 0.0).astype(BF16), v[ci * cs:(ci + 1) * cs, :])
                     for ci, sc in enumerate(scores)]
            intra_ref[r, :] = jnp.concatenate(intra, axis=0)
        return deepest

    deepest = lax.fori_loop(0, lc // (blk * HGRN_GROUP), precompute, jnp.zeros((1, HEAD_DIM), F32))
    depth = jnp.max(deepest)

    @pl.when(depth <= HGRN_SAFE_LOG2)
    def _():
        def recur(bi, carry):
            base = pl.multiple_of(bi * blk, blk)
            st = st_ref[...]
            outs = []
            for ci in range(cpb):
                rows_c = pl.ds(base + ci * cs, cs)
                dec = dec_ref[pl.ds(pl.multiple_of((bi * cpb + ci) * SUBLANES, SUBLANES), SUBLANES), :]
                outs.append(_dot_nt(qe_ref[rows_c, :], st.astype(BF16)) + intra_ref[rows_c, :])
                st = st * dec[0:1, :] + u_ref[bi * cpb + ci]
            st_ref[...] = st
            finish(pl.ds(base, blk), outs)
            return carry

        lax.fori_loop(0, lc // blk, recur, 0, unroll=4)

    @pl.when(jnp.logical_not(depth <= HGRN_SAFE_LOG2))
    def _():
        def general(bi, carry):
            rows = pl.ds(pl.multiple_of(bi * blk, blk), blk)
            q, kk, cum, lasts = prep(rows)
            v = i_ref[rows, :]
            v32 = v.astype(F32)
            a_ref[...] = cum - jnp.log2(kk)
            v_ref[...] = v32
            st = st_ref[...]
            outs = []
            for ci in range(cpb):
                c0 = ci * cs
                cum_c = cum[c0:c0 + cs, :]
                q_c = q[c0:c0 + cs, :]
                kk_c = kk[c0:c0 + cs, :]
                inter = _dot_nt((q_c * jnp.exp2(cum_c)).astype(BF16), st.astype(BF16))
                st = st * jnp.exp2(lasts[ci]) + state_increment(ci, kk, cum, lasts, v32)

                score_rows = [jnp.zeros((sb, cs), F32)]
                for si in range(1, cs // sb):
                    r0 = si * sb
                    ref = cum_c[r0 - 1:r0, :]
                    qt = q_c[r0:r0 + sb, :] * jnp.exp2(cum_c[r0:r0 + sb, :] - ref)
                    kt = kk_c * jnp.exp2(jnp.minimum(ref - cum_c, 0.0))
                    sc = _dot_nt(qt.astype(BF16), kt.astype(BF16))
                    score_rows.append(jnp.where(blk_col < r0, sc, 0.0))
                off = _dot(jnp.concatenate(score_rows, axis=0).astype(BF16), v[c0:c0 + cs, :])

                diag_rows = []
                for si in range(cs // sb):
                    r0 = c0 + si * sb
                    q_lo, q_hi = q[r0:r0 + half, :], q[r0 + half:r0 + sb, :]
                    c_lo, c_hi = cum[r0:r0 + half, :], cum[r0 + half:r0 + sb, :]
                    acc_lo = jnp.zeros((half, HEAD_DIM), F32)
                    acc_hi = jnp.zeros((half, HEAD_DIM), F32)
                    for ti in range(sb):
                        a_s = a_ref[r0 + ti:r0 + ti + 1, :]
                        v_s = v_ref[r0 + ti:r0 + ti + 1, :]
                        if ti < half:
                            d = c_lo - a_s
                            if ti > 0:
                                d = jnp.where(half_row >= ti, d, NEG_BIG)
                            acc_lo = acc_lo + jnp.sum(q_lo * jnp.exp2(d), axis=-1, keepdims=True) * v_s
                            d = c_hi - a_s
                        else:
                            d = c_hi - a_s
                            if ti > half:
                                d = jnp.where(half_row >= ti - half, d, NEG_BIG)
                        acc_hi = acc_hi + jnp.sum(q_hi * jnp.exp2(d), axis=-1, keepdims=True) * v_s
                    diag_rows += [acc_lo, acc_hi]
                outs.append(inter + off + jnp.concatenate(diag_rows, axis=0))
            st_ref[...] = st
            finish(rows, outs)
            return carry

        lax.fori_loop(0, lc // blk, general, 0)


def _hgrn(main3, fg3, col_blocks, lb_logits, norm_g, *, lc):
    b, s, _ = main3.shape
    nh, dh = NUM_HEADS, HEAD_DIM
    nl = lb_logits.shape[0]
    assert lc % (HGRN_BLOCK * HGRN_GROUP) == 0
    q_blk, i_blk, g_blk = col_blocks

    def col(first):
        return pl.BlockSpec((None, lc, dh), lambda i, h, j: (i, j, first + h))

    return pl.pallas_call(
        functools.partial(_hgrn_kernel, lc=lc),
        grid=(b, nh, s // lc),
        in_specs=[
            col(q_blk), col(0), col(i_blk), col(g_blk),
            pl.BlockSpec((nl, dh), lambda i, h, j: (0, h)),
            pl.BlockSpec((1, dh), lambda i, h, j: (0, 0)),
        ],
        out_specs=pl.BlockSpec((None, lc, dh), lambda i, h, j: (i, j, h)),
        out_shape=jax.ShapeDtypeStruct((b, s, nh * dh), BF16),
        scratch_shapes=[
            pltpu.VMEM((dh, dh), F32),
            pltpu.VMEM((lc, dh), BF16),
            pltpu.VMEM((lc, dh), F32),
            pltpu.VMEM((lc // HGRN_CHUNK, dh, dh), F32),
            pltpu.VMEM((lc // HGRN_CHUNK * SUBLANES, dh), F32),
            pltpu.VMEM((HGRN_BLOCK, dh), F32),
            pltpu.VMEM((HGRN_BLOCK, dh), F32),
        ],
        compiler_params=_params("parallel", "parallel", "arbitrary"),
        name="hgrn",
    )(main3, fg3, main3, main3, lb_logits, norm_g.reshape(1, dh))


def _merge_kernel(h_ref, gpost_ref, ya_ref, yb_ref, ga_ref, gb_ref, wpa_ref, wpb_ref, wo_ref, o_ref):
    j = pl.program_id(1)

    @pl.when(j == 0)
    def _():
        o_ref[...] = jnp.zeros_like(o_ref)

    tn = wo_ref.shape[0]
    groups = [slice(c0, c0 + tn // MERGE_COL_GROUPS) for c0 in range(0, tn, tn // MERGE_COL_GROUPS)]
    stage1 = [(_dot(ya_ref[...], wpa_ref[:, c]), _dot(yb_ref[...], wpb_ref[:, c])) for c in groups]
    merged = [(ga_ref[:, c].astype(F32) * y_a + gb_ref[:, c].astype(F32) * y_b).astype(BF16)
              for c, (y_a, y_b) in zip(groups, stage1)]
    out = _dot(merged[0], wo_ref[groups[0], :])
    for mg, c in zip(merged[1:], groups[1:]):
        out += _dot(mg, wo_ref[c, :])
    o_ref[...] += out

    @pl.when(j == pl.num_programs(1) - 1)
    def _():
        _postnorm_residual_to(o_ref, h_ref, o_ref, gpost_ref, 1.0)


def _merge(h, g_post, y_a, y_b, gates, w_pa, w_pb, w_o, *, tm, tn):
    t, d = h.shape
    wa = y_a.shape[1]
    wb = y_b.shape[1]
    return pl.pallas_call(
        _merge_kernel,
        grid=(t // tm, d // tn),
        in_specs=[
            pl.BlockSpec((tm, d), lambda i, j: (i, 0)),
            pl.BlockSpec((1, d), lambda i, j: (0, 0)),
            pl.BlockSpec((tm, wa), lambda i, j: (i, 0)),
            pl.BlockSpec((tm, wb), lambda i, j: (i, 0)),
            pl.BlockSpec((tm, tn), lambda i, j: (i, j)),
            pl.BlockSpec((tm, tn), lambda i, j: (i, d // tn + j)),
            pl.BlockSpec((wa, tn), lambda i, j: (0, j)),
            pl.BlockSpec((wb, tn), lambda i, j: (0, j)),
            pl.BlockSpec((tn, d), lambda i, j: (j, 0)),
        ],
        out_specs=pl.BlockSpec((tm, d), lambda i, j: (i, 0)),
        out_shape=jax.ShapeDtypeStruct((t, d), F32),
        compiler_params=_params("parallel", "arbitrary"),
        name="merge",
    )(h, g_post.reshape(1, d), y_a, y_b, gates, gates, w_pa, w_pb, w_o)


def _ple_kernel(h_ref, gpre_ref, gpost_ref, p_ref, wg_ref, wp_ref, o_ref):
    tm = h_ref.shape[0]
    groups = [slice(r0, r0 + tm // PLE_ROW_GROUPS) for r0 in range(0, tm, tm // PLE_ROW_GROUPS)]
    emb = [_dot(p_ref[r, :].astype(BF16), wp_ref[...]) for r in groups]
    gates = [_dot(_rms(h_ref[r, :], gpre_ref[...]).astype(BF16), wg_ref[...]) for r in groups]
    for r, e, g in zip(groups, emb, gates):
        o_ref[r, :] = h_ref[r, :] + _rms(jax.nn.sigmoid(g) * e, gpost_ref[...])


def _ple(h, g_pre, g_post, p2, w_g, w_p, *, tm):
    t, d = h.shape
    dp = p2.shape[1]
    return pl.pallas_call(
        _ple_kernel,
        grid=(t // tm,),
        in_specs=[
            pl.BlockSpec((tm, d), lambda i: (i, 0)),
            pl.BlockSpec((1, d), lambda i: (0, 0)),
            pl.BlockSpec((1, d), lambda i: (0, 0)),
            pl.BlockSpec((tm, dp), lambda i: (i, 0)),
            pl.BlockSpec((d, d), lambda i: (0, 0)),
            pl.BlockSpec((dp, d), lambda i: (0, 0)),
        ],
        out_specs=pl.BlockSpec((tm, d), lambda i: (i, 0)),
        out_shape=jax.ShapeDtypeStruct((t, d), F32),
        compiler_params=_params("parallel"),
        name="ple",
    )(h, g_pre.reshape(1, d), g_post.reshape(1, d), p2, w_g, w_p)


def _tile(n, want):
    t = min(n, want)
    while n % t:
        t //= 2
    return t


def kernel(x, p, ffn1_pre_g, ffn1_post_g, ffn1_w_gate, ffn1_w_up, ffn1_w_down, mix_pre_g, mix_post_g, mix_w_in, fox_f_bias, hgrn_lb_logits, hgrn_norm_g, mix_w_proj_fox, mix_w_proj_hgrn, mix_w_out, ffn2_pre_g, ffn2_post_g, ffn2_w_gate, ffn2_w_up, ffn2_w_down, ple_pre_g, ple_post_g, ple_w_gate, ple_w_proj):
    b, s, d = x.shape
    t = b * s
    depth = ffn1_pre_g.shape[0]
    assert depth == 1, "the HGRN2 lower bound is evaluated for a single layer"
    nh, dh = NUM_HEADS, HEAD_DIM
    width = nh * dh
    assert mix_w_in.shape[-1] == 3 * width + nh + 4 * width + 2 * d

    tm_big = _tile(t, 1024)
    h = x.reshape(t, d)
    for i in range(depth):
        h = _ffn(h, ffn1_pre_g[i], ffn1_post_g[i], ffn1_w_gate[i].astype(BF16),
                 ffn1_w_up[i].astype(BF16), ffn1_w_down[i].astype(BF16),
                 tm=tm_big, tf=_tile(ffn1_w_gate.shape[-1], 512))

        w_in = mix_w_in[i]
        o_f = 3 * width
        o_b = o_f + nh
        w_bf = w_in.astype(BF16)
        w_fa = jnp.pad(w_bf[:, o_f:o_b], ((0, 0), (0, dh - nh)))
        w_rest = w_bf[:, o_b:]

        main, gates, fg = _proj_in(h, mix_pre_g[i], w_bf, w_rest, w_fa, tm=tm_big, width=width,
                                   gate_width=2 * d, q_scale=dh ** -0.5 * LOG2E)
        main3 = main.reshape(b, s, 6 * width)
        fg3 = fg.reshape(b, s, width + dh)
        c = _fox_gate(fg3, nh, fox_f_bias[i], ts=_tile(s, 512))
        y_a = _fox_attn(main3, c.reshape(b, nh, 1, s), tq=_tile(s, 512))
        y_b = _hgrn(main3, fg3, (3 * nh, 4 * nh, 5 * nh), hgrn_lb_logits, hgrn_norm_g[i], lc=_tile(s, 2048))
        h = _merge(h, mix_post_g[i], y_a.reshape(t, width), y_b.reshape(t, width), gates,
                   mix_w_proj_fox[i].astype(BF16), mix_w_proj_hgrn[i].astype(BF16),
                   mix_w_out[i].astype(BF16), tm=tm_big, tn=_tile(d, 512))

        h = _ffn(h, ffn2_pre_g[i], ffn2_post_g[i], ffn2_w_gate[i].astype(BF16),
                 ffn2_w_up[i].astype(BF16), ffn2_w_down[i].astype(BF16),
                 tm=tm_big, tf=_tile(ffn2_w_gate.shape[-1], 512))
        h = _ple(h, ple_pre_g[i], ple_post_g[i], p[i].reshape(t, -1),
                 ple_w_gate[i].astype(BF16), ple_w_proj[i].astype(BF16), tm=_tile(t, 512))
    return h.reshape(b, s, d)
```

```python
import functools
import math

import jax
import jax.numpy as jnp
from jax import lax
from jax.experimental import pallas as pl
from jax.experimental.pallas import tpu as pltpu

NORM_EPS = 1e-6
MACARON_SCALE = 0.5
LOG2E = math.log2(math.e)
HEAD_DIM = 128
NUM_HEADS = 8
HGRN_CHUNK = 64
HGRN_SUB = 16
HGRN_BLOCK = 256
HGRN_GROUP = 4
HGRN_SAFE_LOG2 = 100.0
FFN_COL_GROUPS = 2
MERGE_COL_GROUPS = 2
PLE_ROW_GROUPS = 2
NORM_ROWS = 16
SUBLANES = 8
NEG_BIG = -1e30

VMEM_LIMIT_BYTES = 60 * 1024 * 1024

BF16 = jnp.bfloat16
F32 = jnp.float32


def _params(*semantics):
    return pltpu.CompilerParams(dimension_semantics=semantics,
                                vmem_limit_bytes=VMEM_LIMIT_BYTES)


def _rms(x, g):
    ms = jnp.mean(x * x, axis=-1, keepdims=True)
    return x * lax.rsqrt(ms + NORM_EPS) * g


def _postnorm_residual_to(o_ref, x_ref, y_ref, g_ref, scale):
    g = g_ref[...] if scale == 1.0 else scale * g_ref[...]
    for r0 in range(0, x_ref.shape[0], NORM_ROWS):
        rows = slice(r0, r0 + NORM_ROWS)
        o_ref[rows, :] = x_ref[rows, :] + _rms(y_ref[rows, :], g)


def _sigmoid(x):
    return 0.5 + 0.5 * jnp.tanh(0.5 * x)


def _silu(x):
    h = 0.5 * x
    return h + h * jnp.tanh(h)


def _dot(a, b):
    return jnp.dot(a, b, preferred_element_type=F32)


def _dot_nt(a, b):
    return lax.dot_general(a, b, (((1,), (1,)), ((), ())), preferred_element_type=F32)


def _split3(x):
    hi = x.astype(BF16)
    r = x - hi.astype(F32)
    mid = r.astype(BF16)
    lo = (r - mid.astype(F32)).astype(BF16)
    return hi, mid, lo


def _ffn_kernel(x_ref, gpre_ref, gpost_ref, wg_ref, wu_ref, wd_ref, o_ref, xn_ref):
    j = pl.program_id(1)

    @pl.when(j == 0)
    def _():
        xn_ref[...] = _rms(x_ref[...], gpre_ref[...]).astype(BF16)
        o_ref[...] = jnp.zeros_like(o_ref)

    xn = xn_ref[...]
    tf = wg_ref.shape[1]
    groups = [slice(c0, c0 + tf // FFN_COL_GROUPS) for c0 in range(0, tf, tf // FFN_COL_GROUPS)]
    gu = [(_dot(xn, wg_ref[:, c]), _dot(xn, wu_ref[:, c])) for c in groups]
    acts = [(g * jax.nn.sigmoid(g) * u).astype(BF16) for g, u in gu]
    down = _dot(acts[0], wd_ref[groups[0], :])
    for a, c in zip(acts[1:], groups[1:]):
        down += _dot(a, wd_ref[c, :])
    o_ref[...] += down

    @pl.when(j == pl.num_programs(1) - 1)
    def _():
        _postnorm_residual_to(o_ref, x_ref, o_ref, gpost_ref, MACARON_SCALE)


def _ffn(h, g_pre, g_post, w_gate, w_up, w_down, *, tm, tf):
    t, d = h.shape
    f = w_gate.shape[1]
    return pl.pallas_call(
        _ffn_kernel,
        grid=(t // tm, f // tf),
        in_specs=[
            pl.BlockSpec((tm, d), lambda i, j: (i, 0)),
            pl.BlockSpec((1, d), lambda i, j: (0, 0)),
            pl.BlockSpec((1, d), lambda i, j: (0, 0)),
            pl.BlockSpec((d, tf), lambda i, j: (0, j)),
            pl.BlockSpec((d, tf), lambda i, j: (0, j)),
            pl.BlockSpec((tf, d), lambda i, j: (j, 0)),
        ],
        out_specs=pl.BlockSpec((tm, d), lambda i, j: (i, 0)),
        out_shape=jax.ShapeDtypeStruct((t, d), F32),
        scratch_shapes=[pltpu.VMEM((tm, d), BF16)],
        compiler_params=_params("parallel", "arbitrary"),
        name="ffn",
    )(h, g_pre.reshape(1, d), g_post.reshape(1, d), w_gate, w_up, w_down)


def _prep_w_in_kernel(w_ref, fox_ref, rest_ref, *, rest_col):
    w = w_ref[...]
    fox_ref[...] = w[:, :fox_ref.shape[1]].astype(fox_ref.dtype)
    rest_ref[...] = w[:, rest_col:].astype(rest_ref.dtype)


def _prep_w_in(w_in, fox_cols, rest_col, *, rows):
    d, n = w_in.shape
    return pl.pallas_call(
        functools.partial(_prep_w_in_kernel, rest_col=rest_col),
        grid=(d // rows,),
        in_specs=[pl.BlockSpec((rows, n), lambda i: (i, 0))],
        out_specs=[pl.BlockSpec((rows, fox_cols), lambda i: (i, 0)),
                   pl.BlockSpec((rows, n - rest_col), lambda i: (i, 0))],
        out_shape=[jax.ShapeDtypeStruct((d, fox_cols), BF16),
                   jax.ShapeDtypeStruct((d, n - rest_col), BF16)],
        compiler_params=_params("parallel"),
        name="prep_w_in",
    )(w_in)


PROJ_FOX_STEPS = 3
PROJ_MAIN_STEPS = 6


def _proj_in_kernel(x_ref, g_ref, wa_ref, wr_ref, wfa_ref, o_ref, og_ref, of_ref, xn_ref, *, q_scale):
    j = pl.program_id(1)
    last = pl.num_programs(1) - 1

    @pl.when(j == 0)
    def _():
        xn_ref[...] = _rms(x_ref[...], g_ref[...]).astype(BF16)

    def store_heads(dst_ref, y):
        for hh in range(dst_ref.shape[0]):
            dst_ref[hh] = y[:, hh * HEAD_DIM:(hh + 1) * HEAD_DIM].astype(dst_ref.dtype)

    @pl.when(j < PROJ_FOX_STEPS)
    def _():
        y = _dot(xn_ref[...], wa_ref[...])
        store_heads(o_ref, y * jnp.where(j == 0, q_scale, 1.0))

    @pl.when((j >= PROJ_FOX_STEPS) & (j < PROJ_MAIN_STEPS))
    def _():
        store_heads(o_ref, _dot(xn_ref[...], wr_ref[...]))

    @pl.when((j >= PROJ_MAIN_STEPS) & (j < last))
    def _():
        og_ref[...] = _sigmoid(_dot(xn_ref[...], wr_ref[...])).astype(og_ref.dtype)

    @pl.when(j == last)
    def _():
        nh = wr_ref.shape[1] // HEAD_DIM
        store_heads(of_ref.at[:nh], _dot(xn_ref[...], wr_ref[...]))
        of_ref[nh] = _dot(xn_ref[...], wfa_ref[...])


def _proj_in(h, g, w_fox, w_rest, *, tm, width, gate_width, q_scale):
    t, d = h.shape
    nh = width // HEAD_DIM
    assert w_fox.shape[1] == PROJ_FOX_STEPS * width + HEAD_DIM
    assert w_rest.shape[1] == 4 * width + gate_width and gate_width % width == 0
    n_gate = gate_width // width
    last = PROJ_MAIN_STEPS + n_gate

    def rest_col(i, j):
        return (0, jnp.where(j <= 3, 0, jnp.where(j == last, 1, j - 2)))

    def gate_step(j):
        return jnp.clip(j - PROJ_MAIN_STEPS, 0, n_gate - 1)

    return pl.pallas_call(
        functools.partial(_proj_in_kernel, q_scale=q_scale),
        grid=(t // tm, last + 1),
        in_specs=[
            pl.BlockSpec((tm, d), lambda i, j: (i, 0)),
            pl.BlockSpec((1, d), lambda i, j: (0, 0)),
            pl.BlockSpec((d, width), lambda i, j: (0, jnp.minimum(j, PROJ_FOX_STEPS - 1))),
            pl.BlockSpec((d, width), rest_col),
            pl.BlockSpec((d, HEAD_DIM), lambda i, j: (0, PROJ_FOX_STEPS * nh)),
        ],
        out_specs=[
            pl.BlockSpec((nh, tm, HEAD_DIM), lambda i, j: (jnp.minimum(j, PROJ_MAIN_STEPS - 1), i, 0)),
            pl.BlockSpec((tm, width), lambda i, j: (i, gate_step(j))),
            pl.BlockSpec((nh + 1, tm, HEAD_DIM), lambda i, j: (0, i, 0)),
        ],
        out_shape=[jax.ShapeDtypeStruct((PROJ_MAIN_STEPS * nh, t, HEAD_DIM), BF16),
                   jax.ShapeDtypeStruct((t, gate_width), BF16),
                   jax.ShapeDtypeStruct((nh + 1, t, HEAD_DIM), F32)],
        scratch_shapes=[pltpu.VMEM((tm, d), BF16)],
        compiler_params=_params("parallel", "arbitrary"),
        name="proj_in",
    )(h, g.reshape(1, d), w_fox, w_rest, w_fox)


def _fox_gate_kernel(f_ref, b_ref, o_ref, carry_ref, *, ts):
    @pl.when(pl.program_id(1) == 0)
    def _():
        carry_ref[...] = jnp.zeros_like(carry_ref)

    z = f_ref[...].T[:NUM_HEADS, :] + b_ref[...]
    lf = (jnp.minimum(z, 0.0) - jnp.log1p(jnp.exp(-jnp.abs(z)))) * LOG2E
    src = lax.broadcasted_iota(jnp.int32, (ts, ts), 0)
    dst = lax.broadcasted_iota(jnp.int32, (ts, ts), 1)
    tri = jnp.where(src <= dst, 1.0, 0.0).astype(BF16)
    hi, mid, lo = _split3(lf)
    c = _dot(hi, tri) + _dot(mid, tri) + _dot(lo, tri) + carry_ref[:, :1]
    o_ref[...] = c
    carry_ref[...] = jnp.broadcast_to(c[:, ts - 1:ts], carry_ref.shape)


def _fox_gate(f4, slab, bias, *, ts):
    _, b, s, _ = f4.shape
    nh = NUM_HEADS
    return pl.pallas_call(
        functools.partial(_fox_gate_kernel, ts=ts),
        grid=(b, s // ts),
        in_specs=[
            pl.BlockSpec((None, None, ts, HEAD_DIM), lambda i, j: (slab, i, j, 0)),
            pl.BlockSpec((nh, 1), lambda i, j: (0, 0)),
        ],
        out_specs=pl.BlockSpec((None, nh, ts), lambda i, j: (i, 0, j)),
        out_shape=jax.ShapeDtypeStruct((b, nh, s), F32),
        scratch_shapes=[pltpu.VMEM((nh, HEAD_DIM), F32)],
        compiler_params=_params("parallel", "arbitrary"),
        name="fox_gate",
    )(f4, bias.reshape(nh, 1))


def _fox_attn_kernel(q_ref, k_ref, v_ref, c_ref, o_ref, m_ref, l_ref, acc_ref, *, tq):
    nq = q_ref.shape[0] // tq
    reps = tq // HEAD_DIM

    m_ref[...] = jnp.full_like(m_ref, NEG_BIG)
    l_ref[...] = jnp.zeros_like(l_ref)
    acc_ref[...] = jnp.zeros_like(acc_ref)

    def step(qi, ki):
        rows = slice(qi * tq, (qi + 1) * tq)
        cols = slice(ki * tq, (ki + 1) * tq)
        c0 = jnp.max(c_ref[:, rows], axis=-1, keepdims=True)
        s = _dot_nt(q_ref[rows, :], k_ref[cols, :]) + (c0 - c_ref[:, cols])
        if qi == ki:
            row = lax.broadcasted_iota(jnp.int32, (tq, tq), 0)
            col = lax.broadcasted_iota(jnp.int32, (tq, tq), 1)
            s = jnp.where(col <= row, s, NEG_BIG)
        m_prev = m_ref[rows, :]
        m_next = jnp.maximum(m_prev, jnp.max(s, axis=-1, keepdims=True))
        p = jnp.exp2(s - jnp.concatenate([m_next] * reps, axis=1))
        alpha = jnp.exp2(m_prev - m_next)
        l_ref[rows, :] = alpha * l_ref[rows, :] + jnp.sum(p, axis=-1, keepdims=True)
        acc_ref[rows, :] = alpha * acc_ref[rows, :] + _dot(p.astype(BF16), v_ref[cols, :])
        m_ref[rows, :] = m_next

    for diag in range(nq):
        for qi in range(diag, nq):
            step(qi, qi - diag)
    o_ref[...] = (acc_ref[...] / l_ref[...]).astype(o_ref.dtype)


def _fox_attn(main4, c4, *, tq):
    _, b, s, _ = main4.shape
    nh, dh = NUM_HEADS, HEAD_DIM
    return pl.pallas_call(
        functools.partial(_fox_attn_kernel, tq=tq),
        grid=(b, nh),
        in_specs=[
            pl.BlockSpec((None, None, s, dh), lambda i, h: (h, i, 0, 0)),
            pl.BlockSpec((None, None, s, dh), lambda i, h: (nh + h, i, 0, 0)),
            pl.BlockSpec((None, None, s, dh), lambda i, h: (2 * nh + h, i, 0, 0)),
            pl.BlockSpec((None, None, 1, s), lambda i, h: (i, h, 0, 0)),
        ],
        out_specs=pl.BlockSpec((None, s, dh), lambda i, h: (i, 0, h)),
        out_shape=jax.ShapeDtypeStruct((b, s, nh * dh), BF16),
        scratch_shapes=[pltpu.VMEM((s, dh), F32), pltpu.VMEM((s, dh), F32),
                        pltpu.VMEM((s, dh), F32)],
        compiler_params=_params("parallel", "parallel"),
        name="fox_attn",
    )(main4, main4, main4, c4)


def _hgrn_kernel(q_ref, f_ref, i_ref, g_ref, lbl_ref, ng_ref, o_ref,
                 st_ref, qe_ref, intra_ref, u_ref, dec_ref, a_ref, v_ref, *, lc):
    cs, sb, blk = HGRN_CHUNK, HGRN_SUB, HGRN_BLOCK
    half = sb // 2
    cpb = blk // cs

    @pl.when(pl.program_id(2) == 0)
    def _():
        st_ref[...] = jnp.zeros_like(st_ref)

    logits = lbl_ref[...]
    e = jnp.exp(logits - jnp.max(logits, axis=0, keepdims=True))
    lb = e[0:1, :] / jnp.sum(e, axis=0, keepdims=True)
    ng = ng_ref[...]

    src = lax.broadcasted_iota(jnp.int32, (blk, blk), 1)
    dst = lax.broadcasted_iota(jnp.int32, (blk, blk), 0)
    same_chunk = (src // cs) == (dst // cs)
    tri = jnp.where(same_chunk & (src <= dst), 1.0, 0.0).astype(BF16)
    half_row = lax.broadcasted_iota(jnp.int32, (half, HEAD_DIM), 0)
    blk_col = lax.broadcasted_iota(jnp.int32, (sb, cs), 1)
    chunk_row = lax.broadcasted_iota(jnp.int32, (cs, cs), 0)
    chunk_col = lax.broadcasted_iota(jnp.int32, (cs, cs), 1)

    def prep(rows):
        q = _silu(q_ref[rows, :].astype(F32))
        f = lb + (1.0 - lb) * _sigmoid(f_ref[rows, :])
        kk = 1.0 - f
        hi, mid, lo = _split3(jnp.log2(f))
        cum = _dot(tri, hi) + _dot(tri, mid) + _dot(tri, lo)
        lasts = [cum[c0 + cs - 1:c0 + cs, :] for c0 in range(0, blk, cs)]
        return q, kk, cum, lasts

    def state_increment(ci, kk, cum, lasts, v):
        c0 = ci * cs
        kdec = (kk[c0:c0 + cs, :] * jnp.exp2(lasts[ci] - cum[c0:c0 + cs, :])).astype(BF16)
        return _dot(v[c0:c0 + cs, :].T.astype(BF16), kdec)

    def finish(rows, outs):
        o = _rms(jnp.concatenate(outs, axis=0), ng) * _silu(g_ref[rows, :].astype(F32))
        o_ref[rows, :] = o.astype(o_ref.dtype)

    def precompute(gi, deepest):
        blocks = [gi * HGRN_GROUP + k for k in range(HGRN_GROUP)]
        rows = [pl.ds(pl.multiple_of(bi * blk, blk), blk) for bi in blocks]
        stage1 = [prep(r) for r in rows]
        stage2 = []
        for bi, r, (q, kk, cum, lasts) in zip(blocks, rows, stage1):
            v = i_ref[r, :]
            v32 = v.astype(F32)
            qe = (q * jnp.exp2(cum)).astype(BF16)
            kinv = (kk * jnp.exp2(-cum)).astype(BF16)
            qe_ref[r, :] = qe
            scores = [_dot_nt(qe[c0:c0 + cs, :], kinv[c0:c0 + cs, :]) for c0 in range(0, blk, cs)]
            for ci in range(cpb):
                u_ref[bi * cpb + ci] = state_increment(ci, kk, cum, lasts, v32)
                dec_ref[pl.ds(pl.multiple_of((bi * cpb + ci) * SUBLANES, SUBLANES), SUBLANES), :] = (
                    jnp.broadcast_to(jnp.exp2(lasts[ci]), (SUBLANES, HEAD_DIM)))
                deepest = jnp.maximum(deepest, -lasts[ci])
            stage2.append((v, scores))
        for r, (v, scores) in zip(rows, stage2):
            intra = [_dot(jnp.where(chunk_col <= chunk_row, sc, 0.0).astype(BF16), v[ci * cs:(ci + 1) * cs, :])
                     for ci, sc in enumerate(scores)]
            intra_ref[r, :] = jnp.concatenate(intra, axis=0)
        return deepest

    deepest = lax.fori_loop(0, lc // (blk * HGRN_GROUP), precompute, jnp.zeros((1, HEAD_DIM), F32))
    depth = jnp.max(deepest)

    @pl.when(depth <= HGRN_SAFE_LOG2)
    def _():
        def recur(bi, carry):
            base = pl.multiple_of(bi * blk, blk)
            st = st_ref[...]
            outs = []
            for ci in range(cpb):
                rows_c = pl.ds(base + ci * cs, cs)
                dec = dec_ref[pl.ds(pl.multiple_of((bi * cpb + ci) * SUBLANES, SUBLANES), SUBLANES), :]
                outs.append(_dot_nt(qe_ref[rows_c, :], st.astype(BF16)) + intra_ref[rows_c, :])
                st = st * dec[0:1, :] + u_ref[bi * cpb + ci]
            st_ref[...] = st
            finish(pl.ds(base, blk), outs)
            return carry

        lax.fori_loop(0, lc // blk, recur, 0, unroll=4)

    @pl.when(jnp.logical_not(depth <= HGRN_SAFE_LOG2))
    def _():
        def general(bi, carry):
            rows = pl.ds(pl.multiple_of(bi * blk, blk), blk)
            q, kk, cum, lasts = prep(rows)
            v = i_ref[rows, :]
            v32 = v.astype(F32)
            a_ref[...] = cum - jnp.log2(kk)
            v_ref[...] = v32
            st = st_ref[...]
            outs = []
            for ci in range(cpb):
                c0 = ci * cs
                cum_c = cum[c0:c0 + cs, :]
                q_c = q[c0:c0 + cs, :]
                kk_c = kk[c0:c0 + cs, :]
                inter = _dot_nt((q_c * jnp.exp2(cum_c)).astype(BF16), st.astype(BF16))
                st = st * jnp.exp2(lasts[ci]) + state_increment(ci, kk, cum, lasts, v32)

                score_rows = [jnp.zeros((sb, cs), F32)]
                for si in range(1, cs // sb):
                    r0 = si * sb
                    ref = cum_c[r0 - 1:r0, :]
                    qt = q_c[r0:r0 + sb, :] * jnp.exp2(cum_c[r0:r0 + sb, :] - ref)
                    kt = kk_c * jnp.exp2(jnp.minimum(ref - cum_c, 0.0))
                    sc = _dot_nt(qt.astype(BF16), kt.astype(BF16))
                    score_rows.append(jnp.where(blk_col < r0, sc, 0.0))
                off = _dot(jnp.concatenate(score_rows, axis=0).astype(BF16), v[c0:c0 + cs, :])

                diag_rows = []
                for si in range(cs // sb):
                    r0 = c0 + si * sb
                    q_lo, q_hi = q[r0:r0 + half, :], q[r0 + half:r0 + sb, :]
                    c_lo, c_hi = cum[r0:r0 + half, :], cum[r0 + half:r0 + sb, :]
                    acc_lo = jnp.zeros((half, HEAD_DIM), F32)
                    acc_hi = jnp.zeros((half, HEAD_DIM), F32)
                    for ti in range(sb):
                        a_s = a_ref[r0 + ti:r0 + ti + 1, :]
                        v_s = v_ref[r0 + ti:r0 + ti + 1, :]
                        if ti < half:
                            d = c_lo - a_s
                            if ti > 0:
                                d = jnp.where(half_row >= ti, d, NEG_BIG)
                            acc_lo = acc_lo + jnp.sum(q_lo * jnp.exp2(d), axis=-1, keepdims=True) * v_s
                            d = c_hi - a_s
                        else:
                            d = c_hi - a_s
                            if ti > half:
                                d = jnp.where(half_row >= ti - half, d, NEG_BIG)
                        acc_hi = acc_hi + jnp.sum(q_hi * jnp.exp2(d), axis=-1, keepdims=True) * v_s
                    diag_rows += [acc_lo, acc_hi]
                outs.append(inter + off + jnp.concatenate(diag_rows, axis=0))
            st_ref[...] = st
            finish(rows, outs)
            return carry

        lax.fori_loop(0, lc // blk, general, 0)


def _hgrn(main4, f4, slabs, lb_logits, norm_g, *, lc):
    _, b, s, _ = main4.shape
    nh, dh = NUM_HEADS, HEAD_DIM
    nl = lb_logits.shape[0]
    assert lc % (HGRN_BLOCK * HGRN_GROUP) == 0
    q_blk, i_blk, g_blk = slabs

    def col(first):
        return pl.BlockSpec((None, None, lc, dh), lambda i, h, j: (first + h, i, j, 0))

    return pl.pallas_call(
        functools.partial(_hgrn_kernel, lc=lc),
        grid=(b, nh, s // lc),
        in_specs=[
            col(q_blk), col(0), col(i_blk), col(g_blk),
            pl.BlockSpec((nl, dh), lambda i, h, j: (0, h)),
            pl.BlockSpec((1, dh), lambda i, h, j: (0, 0)),
        ],
        out_specs=pl.BlockSpec((None, lc, dh), lambda i, h, j: (i, j, h)),
        out_shape=jax.ShapeDtypeStruct((b, s, nh * dh), BF16),
        scratch_shapes=[
            pltpu.VMEM((dh, dh), F32),
            pltpu.VMEM((lc, dh), BF16),
            pltpu.VMEM((lc, dh), F32),
            pltpu.VMEM((lc // HGRN_CHUNK, dh, dh), F32),
            pltpu.VMEM((lc // HGRN_CHUNK * SUBLANES, dh), F32),
            pltpu.VMEM((HGRN_BLOCK, dh), F32),
            pltpu.VMEM((HGRN_BLOCK, dh), F32),
        ],
        compiler_params=_params("parallel", "parallel", "arbitrary"),
        name="hgrn",
    )(main4, f4, main4, main4, lb_logits, norm_g.reshape(1, dh))


def _merge_kernel(h_ref, gpost_ref, ya_ref, yb_ref, ga_ref, gb_ref, wpa_ref, wpb_ref, wo_ref, o_ref):
    j = pl.program_id(1)

    @pl.when(j == 0)
    def _():
        o_ref[...] = jnp.zeros_like(o_ref)

    tn = wo_ref.shape[0]
    groups = [slice(c0, c0 + tn // MERGE_COL_GROUPS) for c0 in range(0, tn, tn // MERGE_COL_GROUPS)]
    stage1 = [(_dot(ya_ref[...], wpa_ref[:, c]), _dot(yb_ref[...], wpb_ref[:, c])) for c in groups]
    merged = [(ga_ref[:, c].astype(F32) * y_a + gb_ref[:, c].astype(F32) * y_b).astype(BF16)
              for c, (y_a, y_b) in zip(groups, stage1)]
    out = _dot(merged[0], wo_ref[groups[0], :])
    for mg, c in zip(merged[1:], groups[1:]):
        out += _dot(mg, wo_ref[c, :])
    o_ref[...] += out

    @pl.when(j == pl.num_programs(1) - 1)
    def _():
        _postnorm_residual_to(o_ref, h_ref, o_ref, gpost_ref, 1.0)


def _merge(h, g_post, y_a, y_b, gates, w_pa, w_pb, w_o, *, tm, tn):
    t, d = h.shape
    wa = y_a.shape[1]
    wb = y_b.shape[1]
    return pl.pallas_call(
        _merge_kernel,
        grid=(t // tm, d // tn),
        in_specs=[
            pl.BlockSpec((tm, d), lambda i, j: (i, 0)),
            pl.BlockSpec((1, d), lambda i, j: (0, 0)),
            pl.BlockSpec((tm, wa), lambda i, j: (i, 0)),
            pl.BlockSpec((tm, wb), lambda i, j: (i, 0)),
            pl.BlockSpec((tm, tn), lambda i, j: (i, j)),
            pl.BlockSpec((tm, tn), lambda i, j: (i, d // tn + j)),
            pl.BlockSpec((wa, tn), lambda i, j: (0, j)),
            pl.BlockSpec((wb, tn), lambda i, j: (0, j)),
            pl.BlockSpec((tn, d), lambda i, j: (j, 0)),
        ],
        out_specs=pl.BlockSpec((tm, d), lambda i, j: (i, 0)),
        out_shape=jax.ShapeDtypeStruct((t, d), F32),
        compiler_params=_params("parallel", "arbitrary"),
        name="merge",
    )(h, g_post.reshape(1, d), y_a, y_b, gates, gates, w_pa, w_pb, w_o)


def _ple_kernel(h_ref, gpre_ref, gpost_ref, p_ref, wg_ref, wp_ref, o_ref):
    tm = h_ref.shape[0]
    groups = [slice(r0, r0 + tm // PLE_ROW_GROUPS) for r0 in range(0, tm, tm // PLE_ROW_GROUPS)]
    emb = [_dot(p_ref[r, :].astype(BF16), wp_ref[...]) for r in groups]
    gates = [_dot(_rms(h_ref[r, :], gpre_ref[...]).astype(BF16), wg_ref[...]) for r in groups]
    for r, e, g in zip(groups, emb, gates):
        o_ref[r, :] = h_ref[r, :] + _rms(jax.nn.sigmoid(g) * e, gpost_ref[...])


def _ple(h, g_pre, g_post, p2, w_g, w_p, *, tm):
    t, d = h.shape
    dp = p2.shape[1]
    return pl.pallas_call(
        _ple_kernel,
        grid=(t // tm,),
        in_specs=[
            pl.BlockSpec((tm, d), lambda i: (i, 0)),
            pl.BlockSpec((1, d), lambda i: (0, 0)),
            pl.BlockSpec((1, d), lambda i: (0, 0)),
            pl.BlockSpec((tm, dp), lambda i: (i, 0)),
            pl.BlockSpec((d, d), lambda i: (0, 0)),
            pl.BlockSpec((dp, d), lambda i: (0, 0)),
        ],
        out_specs=pl.BlockSpec((tm, d), lambda i: (i, 0)),
        out_shape=jax.ShapeDtypeStruct((t, d), F32),
        compiler_params=_params("parallel"),
        name="ple",
    )(h, g_pre.reshape(1, d), g_post.reshape(1, d), p2, w_g, w_p)


def _tile(n, want):
    t = min(n, want)
    while n % t:
        t //= 2
    return t


def kernel(x, p, ffn1_pre_g, ffn1_post_g, ffn1_w_gate, ffn1_w_up, ffn1_w_down, mix_pre_g, mix_post_g, mix_w_in, fox_f_bias, hgrn_lb_logits, hgrn_norm_g, mix_w_proj_fox, mix_w_proj_hgrn, mix_w_out, ffn2_pre_g, ffn2_post_g, ffn2_w_gate, ffn2_w_up, ffn2_w_down, ple_pre_g, ple_post_g, ple_w_gate, ple_w_proj):
    b, s, d = x.shape
    t = b * s
    depth = ffn1_pre_g.shape[0]
    assert depth == 1, "the HGRN2 lower bound is evaluated for a single layer"
    nh, dh = NUM_HEADS, HEAD_DIM
    width = nh * dh
    assert mix_w_in.shape[-1] == 3 * width + nh + 4 * width + 2 * d

    tm_big = _tile(t, 1024)
    h = x.reshape(t, d)
    for i in range(depth):
        h = _ffn(h, ffn1_pre_g[i], ffn1_post_g[i], ffn1_w_gate[i].astype(BF16),
                 ffn1_w_up[i].astype(BF16), ffn1_w_down[i].astype(BF16),
                 tm=tm_big, tf=_tile(ffn1_w_gate.shape[-1], 512))

        w_in = mix_w_in[i]
        o_f = 3 * width
        o_b = o_f + nh
        w_fox, w_rest = _prep_w_in(w_in, o_f + dh, o_b, rows=_tile(d, 256))

        main, gates, f = _proj_in(h, mix_pre_g[i], w_fox, w_rest, tm=tm_big, width=width,
                                  gate_width=2 * d, q_scale=dh ** -0.5 * LOG2E)
        main4 = main.reshape(6 * nh, b, s, dh)
        f4 = f.reshape(nh + 1, b, s, dh)
        c = _fox_gate(f4, nh, fox_f_bias[i], ts=_tile(s, 512))
        y_a = _fox_attn(main4, c.reshape(b, nh, 1, s), tq=_tile(s, 512))
        y_b = _hgrn(main4, f4, (3 * nh, 4 * nh, 5 * nh), hgrn_lb_logits, hgrn_norm_g[i], lc=_tile(s, 2048))
        h = _merge(h, mix_post_g[i], y_a.reshape(t, width), y_b.reshape(t, width), gates,
                   mix_w_proj_fox[i].astype(BF16), mix_w_proj_hgrn[i].astype(BF16),
                   mix_w_out[i].astype(BF16), tm=tm_big, tn=_tile(d, 512))

        h = _ffn(h, ffn2_pre_g[i], ffn2_post_g[i], ffn2_w_gate[i].astype(BF16),
                 ffn2_w_up[i].astype(BF16), ffn2_w_down[i].astype(BF16),
                 tm=tm_big, tf=_tile(ffn2_w_gate.shape[-1], 512))
        h = _ple(h, ple_pre_g[i], ple_post_g[i], p[i].reshape(t, -1),
                 ple_w_gate[i].astype(BF16), ple_w_proj[i].astype(BF16), tm=_tile(t, 512))
    return h.reshape(b, s, d)
```

```python
import functools
import math

import jax
import jax.numpy as jnp
from jax import lax
from jax.experimental import pallas as pl
from jax.experimental.pallas import tpu as pltpu

NORM_EPS = 1e-6
MACARON_SCALE = 0.5
LOG2E = math.log2(math.e)
HEAD_DIM = 128
NUM_HEADS = 8
HGRN_CHUNK = 64
HGRN_SUB = 16
HGRN_BLOCK = 256
HGRN_GROUP = 4
HGRN_SAFE_LOG2 = 100.0
FFN_COL_GROUPS = 2
MERGE_COL_GROUPS = 2
PLE_ROW_GROUPS = 2
NORM_ROWS = 16
SUBLANES = 8
BF16_ROWS = 16
NEG_BIG = -1e30

VMEM_LIMIT_BYTES = 60 * 1024 * 1024

BF16 = jnp.bfloat16
F32 = jnp.float32


def _params(*semantics):
    return pltpu.CompilerParams(dimension_semantics=semantics,
                                vmem_limit_bytes=VMEM_LIMIT_BYTES)


def _rms(x, g):
    ms = jnp.mean(x * x, axis=-1, keepdims=True)
    return x * lax.rsqrt(ms + NORM_EPS) * g


def _postnorm_residual_to(o_ref, x_ref, y_ref, g_ref, scale):
    g = g_ref[...] if scale == 1.0 else scale * g_ref[...]
    for r0 in range(0, x_ref.shape[0], NORM_ROWS):
        rows = slice(r0, r0 + NORM_ROWS)
        o_ref[rows, :] = x_ref[rows, :] + _rms(y_ref[rows, :], g)


def _sigmoid(x):
    return 0.5 + 0.5 * jnp.tanh(0.5 * x)


def _silu(x):
    h = 0.5 * x
    return h + h * jnp.tanh(h)


def _dot(a, b):
    return jnp.dot(a, b, preferred_element_type=F32)


def _dot_nt(a, b):
    return lax.dot_general(a, b, (((1,), (1,)), ((), ())), preferred_element_type=F32)


def _split3(x):
    hi = x.astype(BF16)
    r = x - hi.astype(F32)
    mid = r.astype(BF16)
    lo = (r - mid.astype(F32)).astype(BF16)
    return hi, mid, lo


def _with_passengers(body, n_in, n_out, n_pass):
    def kernel(*refs):
        ins, refs = refs[:n_in], refs[n_in:]
        pass_in, refs = refs[:n_pass], refs[n_pass:]
        outs, refs = refs[:n_out], refs[n_out:]
        pass_out, scratch = refs[:n_pass], refs[n_pass:]
        for src, dst in zip(pass_in, pass_out):
            dst[...] = src[...].astype(dst.dtype)
        body(*ins, *outs, *scratch)

    return kernel


def _passenger_specs(weights, n_steps, step_of):
    in_specs, out_specs, out_shapes = [], [], []
    for w in weights:
        r, c = w.shape
        share = 1
        while (r * share) % n_steps or (r * share // n_steps) % BF16_ROWS:
            share *= 2
        spec = pl.BlockSpec((r * share // n_steps, c), lambda *g, share=share: (step_of(*g) // share, 0))
        in_specs.append(spec)
        out_specs.append(spec)
        out_shapes.append(jax.ShapeDtypeStruct((r, c), BF16))
    return in_specs, out_specs, out_shapes


def _ffn_kernel(x_ref, gpre_ref, gpost_ref, wg_ref, wu_ref, wd_ref, o_ref, xn_ref):
    j = pl.program_id(1)

    @pl.when(j == 0)
    def _():
        xn_ref[...] = _rms(x_ref[...], gpre_ref[...]).astype(BF16)
        o_ref[...] = jnp.zeros_like(o_ref)

    xn = xn_ref[...]
    tf = wg_ref.shape[1]
    groups = [slice(c0, c0 + tf // FFN_COL_GROUPS) for c0 in range(0, tf, tf // FFN_COL_GROUPS)]
    gu = [(_dot(xn, wg_ref[:, c]), _dot(xn, wu_ref[:, c])) for c in groups]
    acts = [(g * jax.nn.sigmoid(g) * u).astype(BF16) for g, u in gu]
    down = _dot(acts[0], wd_ref[groups[0], :])
    for a, c in zip(acts[1:], groups[1:]):
        down += _dot(a, wd_ref[c, :])
    o_ref[...] += down

    @pl.when(j == pl.num_programs(1) - 1)
    def _():
        _postnorm_residual_to(o_ref, x_ref, o_ref, gpost_ref, MACARON_SCALE)


def _ffn(h, g_pre, g_post, w_gate, w_up, w_down, *, tm, tf):
    t, d = h.shape
    f = w_gate.shape[1]
    return pl.pallas_call(
        _ffn_kernel,
        grid=(t // tm, f // tf),
        in_specs=[
            pl.BlockSpec((tm, d), lambda i, j: (i, 0)),
            pl.BlockSpec((1, d), lambda i, j: (0, 0)),
            pl.BlockSpec((1, d), lambda i, j: (0, 0)),
            pl.BlockSpec((d, tf), lambda i, j: (0, j)),
            pl.BlockSpec((d, tf), lambda i, j: (0, j)),
            pl.BlockSpec((tf, d), lambda i, j: (j, 0)),
        ],
        out_specs=pl.BlockSpec((tm, d), lambda i, j: (i, 0)),
        out_shape=jax.ShapeDtypeStruct((t, d), F32),
        scratch_shapes=[pltpu.VMEM((tm, d), BF16)],
        compiler_params=_params("parallel", "arbitrary"),
        name="ffn",
    )(h, g_pre.reshape(1, d), g_post.reshape(1, d), w_gate, w_up, w_down)


PROJ_FOX_STEPS = 3
PROJ_MAIN_STEPS = 6


def _proj_in_kernel(x_ref, g_ref, wa_ref, wr_ref, wfa_ref, o_ref, og_ref, of_ref, xn_ref, *, q_scale):
    j = pl.program_id(1)
    last = pl.num_programs(1) - 1

    @pl.when(j == 0)
    def _():
        xn_ref[...] = _rms(x_ref[...], g_ref[...]).astype(BF16)

    def store_heads(dst_ref, y):
        for hh in range(dst_ref.shape[0]):
            dst_ref[hh] = y[:, hh * HEAD_DIM:(hh + 1) * HEAD_DIM].astype(dst_ref.dtype)

    @pl.when(j < PROJ_FOX_STEPS)
    def _():
        y = _dot(xn_ref[...], wa_ref[...])
        store_heads(o_ref, y * jnp.where(j == 0, q_scale, 1.0))

    @pl.when((j >= PROJ_FOX_STEPS) & (j < PROJ_MAIN_STEPS))
    def _():
        store_heads(o_ref, _dot(xn_ref[...], wr_ref[...]))

    @pl.when((j >= PROJ_MAIN_STEPS) & (j < last))
    def _():
        og_ref[...] = _sigmoid(_dot(xn_ref[...], wr_ref[...])).astype(og_ref.dtype)

    @pl.when(j == last)
    def _():
        nh = wr_ref.shape[1] // HEAD_DIM
        store_heads(of_ref.at[:nh], _dot(xn_ref[...], wr_ref[...]))
        of_ref[nh] = _dot(xn_ref[...], wfa_ref[...])


def _proj_in(h, g, w_fox, w_rest, *, tm, width, gate_width, q_scale):
    t, d = h.shape
    nh = width // HEAD_DIM
    assert w_fox.shape[1] >= PROJ_FOX_STEPS * width + HEAD_DIM
    assert w_rest.shape[1] == 4 * width + gate_width and gate_width % width == 0
    n_gate = gate_width // width
    last = PROJ_MAIN_STEPS + n_gate

    def rest_col(i, j):
        return (0, jnp.where(j <= 3, 0, jnp.where(j == last, 1, j - 2)))

    def gate_step(j):
        return jnp.clip(j - PROJ_MAIN_STEPS, 0, n_gate - 1)

    return pl.pallas_call(
        functools.partial(_proj_in_kernel, q_scale=q_scale),
        grid=(t // tm, last + 1),
        in_specs=[
            pl.BlockSpec((tm, d), lambda i, j: (i, 0)),
            pl.BlockSpec((1, d), lambda i, j: (0, 0)),
            pl.BlockSpec((d, width), lambda i, j: (0, jnp.minimum(j, PROJ_FOX_STEPS - 1))),
            pl.BlockSpec((d, width), rest_col),
            pl.BlockSpec((d, HEAD_DIM), lambda i, j: (0, PROJ_FOX_STEPS * nh)),
        ],
        out_specs=[
            pl.BlockSpec((nh, tm, HEAD_DIM), lambda i, j: (jnp.minimum(j, PROJ_MAIN_STEPS - 1), i, 0)),
            pl.BlockSpec((tm, width), lambda i, j: (i, gate_step(j))),
            pl.BlockSpec((nh + 1, tm, HEAD_DIM), lambda i, j: (0, i, 0)),
        ],
        out_shape=[jax.ShapeDtypeStruct((PROJ_MAIN_STEPS * nh, t, HEAD_DIM), BF16),
                   jax.ShapeDtypeStruct((t, gate_width), BF16),
                   jax.ShapeDtypeStruct((nh + 1, t, HEAD_DIM), F32)],
        scratch_shapes=[pltpu.VMEM((tm, d), BF16)],
        compiler_params=_params("parallel", "arbitrary"),
        name="proj_in",
    )(h, g.reshape(1, d), w_fox, w_rest, w_fox)


def _fox_gate_kernel(f_ref, b_ref, o_ref, carry_ref, *, ts):
    @pl.when(pl.program_id(1) == 0)
    def _():
        carry_ref[...] = jnp.zeros_like(carry_ref)

    z = f_ref[...].T[:NUM_HEADS, :] + b_ref[...]
    lf = (jnp.minimum(z, 0.0) - jnp.log1p(jnp.exp(-jnp.abs(z)))) * LOG2E
    src = lax.broadcasted_iota(jnp.int32, (ts, ts), 0)
    dst = lax.broadcasted_iota(jnp.int32, (ts, ts), 1)
    tri = jnp.where(src <= dst, 1.0, 0.0).astype(BF16)
    hi, mid, lo = _split3(lf)
    c = _dot(hi, tri) + _dot(mid, tri) + _dot(lo, tri) + carry_ref[:, :1]
    o_ref[...] = c
    carry_ref[...] = jnp.broadcast_to(c[:, ts - 1:ts], carry_ref.shape)


def _fox_gate(f4, slab, bias, *, ts):
    _, b, s, _ = f4.shape
    nh = NUM_HEADS
    return pl.pallas_call(
        functools.partial(_fox_gate_kernel, ts=ts),
        grid=(b, s // ts),
        in_specs=[
            pl.BlockSpec((None, None, ts, HEAD_DIM), lambda i, j: (slab, i, j, 0)),
            pl.BlockSpec((nh, 1), lambda i, j: (0, 0)),
        ],
        out_specs=pl.BlockSpec((None, nh, ts), lambda i, j: (i, 0, j)),
        out_shape=jax.ShapeDtypeStruct((b, nh, s), F32),
        scratch_shapes=[pltpu.VMEM((nh, HEAD_DIM), F32)],
        compiler_params=_params("parallel", "arbitrary"),
        name="fox_gate",
    )(f4, bias.reshape(nh, 1))


def _fox_attn_kernel(q_ref, k_ref, v_ref, c_ref, o_ref, m_ref, l_ref, acc_ref, *, tq):
    nq = q_ref.shape[0] // tq
    reps = tq // HEAD_DIM

    m_ref[...] = jnp.full_like(m_ref, NEG_BIG)
    l_ref[...] = jnp.zeros_like(l_ref)
    acc_ref[...] = jnp.zeros_like(acc_ref)

    def step(qi, ki):
        rows = slice(qi * tq, (qi + 1) * tq)
        cols = slice(ki * tq, (ki + 1) * tq)
        c0 = jnp.max(c_ref[:, rows], axis=-1, keepdims=True)
        s = _dot_nt(q_ref[rows, :], k_ref[cols, :]) + (c0 - c_ref[:, cols])
        if qi == ki:
            row = lax.broadcasted_iota(jnp.int32, (tq, tq), 0)
            col = lax.broadcasted_iota(jnp.int32, (tq, tq), 1)
            s = jnp.where(col <= row, s, NEG_BIG)
        m_prev = m_ref[rows, :]
        m_next = jnp.maximum(m_prev, jnp.max(s, axis=-1, keepdims=True))
        p = jnp.exp2(s - jnp.concatenate([m_next] * reps, axis=1))
        alpha = jnp.exp2(m_prev - m_next)
        l_ref[rows, :] = alpha * l_ref[rows, :] + jnp.sum(p, axis=-1, keepdims=True)
        acc_ref[rows, :] = alpha * acc_ref[rows, :] + _dot(p.astype(BF16), v_ref[cols, :])
        m_ref[rows, :] = m_next

    for diag in range(nq):
        for qi in range(diag, nq):
            step(qi, qi - diag)
    o_ref[...] = (acc_ref[...] / l_ref[...]).astype(o_ref.dtype)


def _fox_attn(main4, c4, passengers, *, tq):
    _, b, s, _ = main4.shape
    nh, dh = NUM_HEADS, HEAD_DIM
    p_in, p_out, p_shapes = _passenger_specs(passengers, b * nh, lambda i, h: i * nh + h)
    return pl.pallas_call(
        _with_passengers(functools.partial(_fox_attn_kernel, tq=tq), 4, 1, len(passengers)),
        grid=(b, nh),
        in_specs=[
            pl.BlockSpec((None, None, s, dh), lambda i, h: (h, i, 0, 0)),
            pl.BlockSpec((None, None, s, dh), lambda i, h: (nh + h, i, 0, 0)),
            pl.BlockSpec((None, None, s, dh), lambda i, h: (2 * nh + h, i, 0, 0)),
            pl.BlockSpec((None, None, 1, s), lambda i, h: (i, h, 0, 0)),
        ] + p_in,
        out_specs=[pl.BlockSpec((None, s, dh), lambda i, h: (i, 0, h))] + p_out,
        out_shape=[jax.ShapeDtypeStruct((b, s, nh * dh), BF16)] + p_shapes,
        scratch_shapes=[pltpu.VMEM((s, dh), F32), pltpu.VMEM((s, dh), F32),
                        pltpu.VMEM((s, dh), F32)],
        compiler_params=_params("parallel", "parallel"),
        name="fox_attn",
    )(main4, main4, main4, c4, *passengers)


def _hgrn_kernel(q_ref, f_ref, i_ref, g_ref, lbl_ref, ng_ref, o_ref,
                 st_ref, qe_ref, intra_ref, u_ref, dec_ref, a_ref, v_ref, *, lc):
    cs, sb, blk = HGRN_CHUNK, HGRN_SUB, HGRN_BLOCK
    half = sb // 2
    cpb = blk // cs

    @pl.when(pl.program_id(2) == 0)
    def _():
        st_ref[...] = jnp.zeros_like(st_ref)

    logits = lbl_ref[...]
    e = jnp.exp(logits - jnp.max(logits, axis=0, keepdims=True))
    lb = e[0:1, :] / jnp.sum(e, axis=0, keepdims=True)
    ng = ng_ref[...]

    src = lax.broadcasted_iota(jnp.int32, (blk, blk), 1)
    dst = lax.broadcasted_iota(jnp.int32, (blk, blk), 0)
    same_chunk = (src // cs) == (dst // cs)
    tri = jnp.where(same_chunk & (src <= dst), 1.0, 0.0).astype(BF16)
    half_row = lax.broadcasted_iota(jnp.int32, (half, HEAD_DIM), 0)
    blk_col = lax.broadcasted_iota(jnp.int32, (sb, cs), 1)
    chunk_row = lax.broadcasted_iota(jnp.int32, (cs, cs), 0)
    chunk_col = lax.broadcasted_iota(jnp.int32, (cs, cs), 1)

    def prep(rows):
        q = _silu(q_ref[rows, :].astype(F32))
        f = lb + (1.0 - lb) * _sigmoid(f_ref[rows, :])
        kk = 1.0 - f
        hi, mid, lo = _split3(jnp.log2(f))
        cum = _dot(tri, hi) + _dot(tri, mid) + _dot(tri, lo)
        lasts = [cum[c0 + cs - 1:c0 + cs, :] for c0 in range(0, blk, cs)]
        return q, kk, cum, lasts

    def state_increment(ci, kk, cum, lasts, v):
        c0 = ci * cs
        kdec = (kk[c0:c0 + cs, :] * jnp.exp2(lasts[ci] - cum[c0:c0 + cs, :])).astype(BF16)
        return _dot(v[c0:c0 + cs, :].T.astype(BF16), kdec)

    def finish(rows, outs):
        o = _rms(jnp.concatenate(outs, axis=0), ng) * _silu(g_ref[rows, :].astype(F32))
        o_ref[rows, :] = o.astype(o_ref.dtype)

    def precompute(gi, deepest):
        blocks = [gi * HGRN_GROUP + k for k in range(HGRN_GROUP)]
        rows = [pl.ds(pl.multiple_of(bi * blk, blk), blk) for bi in blocks]
        stage1 = [prep(r) for r in rows]
        stage2 = []
        for bi, r, (q, kk, cum, lasts) in zip(blocks, rows, stage1):
            v = i_ref[r, :]
            v32 = v.astype(F32)
            qe = (q * jnp.exp2(cum)).astype(BF16)
            kinv = (kk * jnp.exp2(-cum)).astype(BF16)
            qe_ref[r, :] = qe
            scores = [_dot_nt(qe[c0:c0 + cs, :], kinv[c0:c0 + cs, :]) for c0 in range(0, blk, cs)]
            for ci in range(cpb):
                u_ref[bi * cpb + ci] = state_increment(ci, kk, cum, lasts, v32)
                dec_ref[pl.ds(pl.multiple_of((bi * cpb + ci) * SUBLANES, SUBLANES), SUBLANES), :] = (
                    jnp.broadcast_to(jnp.exp2(lasts[ci]), (SUBLANES, HEAD_DIM)))
                deepest = jnp.maximum(deepest, -lasts[ci])
            stage2.append((v, scores))
        for r, (v, scores) in zip(rows, stage2):
            intra = [_dot(jnp.where(chunk_col <= chunk_row, sc, 0.0).astype(BF16), v[ci * cs:(ci + 1) * cs, :])
                     for ci, sc in enumerate(scores)]
            intra_ref[r, :] = jnp.concatenate(intra, axis=0)
        return deepest

    deepest = lax.fori_loop(0, lc // (blk * HGRN_GROUP), precompute, jnp.zeros((1, HEAD_DIM), F32))
    depth = jnp.max(deepest)

    @pl.when(depth <= HGRN_SAFE_LOG2)
    def _():
        def recur(bi, carry):
            base = pl.multiple_of(bi * blk, blk)
            st = st_ref[...]
            outs = []
            for ci in range(cpb):
                rows_c = pl.ds(base + ci * cs, cs)
                dec = dec_ref[pl.ds(pl.multiple_of((bi * cpb + ci) * SUBLANES, SUBLANES), SUBLANES), :]
                outs.append(_dot_nt(qe_ref[rows_c, :], st.astype(BF16)) + intra_ref[rows_c, :])
                st = st * dec[0:1, :] + u_ref[bi * cpb + ci]
            st_ref[...] = st
            finish(pl.ds(base, blk), outs)
            return carry

        lax.fori_loop(0, lc // blk, recur, 0, unroll=4)

    @pl.when(jnp.logical_not(depth <= HGRN_SAFE_LOG2))
    def _():
        def general(bi, carry):
            rows = pl.ds(pl.multiple_of(bi * blk, blk), blk)
            q, kk, cum, lasts = prep(rows)
            v = i_ref[rows, :]
            v32 = v.astype(F32)
            a_ref[...] = cum - jnp.log2(kk)
            v_ref[...] = v32
            st = st_ref[...]
            outs = []
            for ci in range(cpb):
                c0 = ci * cs
                cum_c = cum[c0:c0 + cs, :]
                q_c = q[c0:c0 + cs, :]
                kk_c = kk[c0:c0 + cs, :]
                inter = _dot_nt((q_c * jnp.exp2(cum_c)).astype(BF16), st.astype(BF16))
                st = st * jnp.exp2(lasts[ci]) + state_increment(ci, kk, cum, lasts, v32)

                score_rows = [jnp.zeros((sb, cs), F32)]
                for si in range(1, cs // sb):
                    r0 = si * sb
                    ref = cum_c[r0 - 1:r0, :]
                    qt = q_c[r0:r0 + sb, :] * jnp.exp2(cum_c[r0:r0 + sb, :] - ref)
                    kt = kk_c * jnp.exp2(jnp.minimum(ref - cum_c, 0.0))
                    sc = _dot_nt(qt.astype(BF16), kt.astype(BF16))
                    score_rows.append(jnp.where(blk_col < r0, sc, 0.0))
                off = _dot(jnp.concatenate(score_rows, axis=0).astype(BF16), v[c0:c0 + cs, :])

                diag_rows = []
                for si in range(cs // sb):
                    r0 = c0 + si * sb
                    q_lo, q_hi = q[r0:r0 + half, :], q[r0 + half:r0 + sb, :]
                    c_lo, c_hi = cum[r0:r0 + half, :], cum[r0 + half:r0 + sb, :]
                    acc_lo = jnp.zeros((half, HEAD_DIM), F32)
                    acc_hi = jnp.zeros((half, HEAD_DIM), F32)
                    for ti in range(sb):
                        a_s = a_ref[r0 + ti:r0 + ti + 1, :]
                        v_s = v_ref[r0 + ti:r0 + ti + 1, :]
                        if ti < half:
                            d = c_lo - a_s
                            if ti > 0:
                                d = jnp.where(half_row >= ti, d, NEG_BIG)
                            acc_lo = acc_lo + jnp.sum(q_lo * jnp.exp2(d), axis=-1, keepdims=True) * v_s
                            d = c_hi - a_s
                        else:
                            d = c_hi - a_s
                            if ti > half:
                                d = jnp.where(half_row >= ti - half, d, NEG_BIG)
                        acc_hi = acc_hi + jnp.sum(q_hi * jnp.exp2(d), axis=-1, keepdims=True) * v_s
                    diag_rows += [acc_lo, acc_hi]
                outs.append(inter + off + jnp.concatenate(diag_rows, axis=0))
            st_ref[...] = st
            finish(rows, outs)
            return carry

        lax.fori_loop(0, lc // blk, general, 0)


def _hgrn(main4, f4, slabs, lb_logits, norm_g, passengers, *, lc):
    _, b, s, _ = main4.shape
    nh, dh = NUM_HEADS, HEAD_DIM
    nl = lb_logits.shape[0]
    assert lc % (HGRN_BLOCK * HGRN_GROUP) == 0
    q_blk, i_blk, g_blk = slabs

    def col(first):
        return pl.BlockSpec((None, None, lc, dh), lambda i, h, j: (first + h, i, j, 0))

    n_lc = s // lc
    p_in, p_out, p_shapes = _passenger_specs(passengers, b * nh * n_lc,
                                             lambda i, h, j: (i * nh + h) * n_lc + j)
    return pl.pallas_call(
        _with_passengers(functools.partial(_hgrn_kernel, lc=lc), 6, 1, len(passengers)),
        grid=(b, nh, n_lc),
        in_specs=[
            col(q_blk), col(0), col(i_blk), col(g_blk),
            pl.BlockSpec((nl, dh), lambda i, h, j: (0, h)),
            pl.BlockSpec((1, dh), lambda i, h, j: (0, 0)),
        ] + p_in,
        out_specs=[pl.BlockSpec((None, lc, dh), lambda i, h, j: (i, j, h))] + p_out,
        out_shape=[jax.ShapeDtypeStruct((b, s, nh * dh), BF16)] + p_shapes,
        scratch_shapes=[
            pltpu.VMEM((dh, dh), F32),
            pltpu.VMEM((lc, dh), BF16),
            pltpu.VMEM((lc, dh), F32),
            pltpu.VMEM((lc // HGRN_CHUNK, dh, dh), F32),
            pltpu.VMEM((lc // HGRN_CHUNK * SUBLANES, dh), F32),
            pltpu.VMEM((HGRN_BLOCK, dh), F32),
            pltpu.VMEM((HGRN_BLOCK, dh), F32),
        ],
        compiler_params=_params("parallel", "parallel", "arbitrary"),
        name="hgrn",
    )(main4, f4, main4, main4, lb_logits, norm_g.reshape(1, dh), *passengers)


def _merge_kernel(h_ref, gpost_ref, ya_ref, yb_ref, ga_ref, gb_ref, wpa_ref, wpb_ref, wo_ref, o_ref):
    j = pl.program_id(1)

    @pl.when(j == 0)
    def _():
        o_ref[...] = jnp.zeros_like(o_ref)

    tn = wo_ref.shape[0]
    groups = [slice(c0, c0 + tn // MERGE_COL_GROUPS) for c0 in range(0, tn, tn // MERGE_COL_GROUPS)]
    stage1 = [(_dot(ya_ref[...], wpa_ref[:, c]), _dot(yb_ref[...], wpb_ref[:, c])) for c in groups]
    merged = [(ga_ref[:, c].astype(F32) * y_a + gb_ref[:, c].astype(F32) * y_b).astype(BF16)
              for c, (y_a, y_b) in zip(groups, stage1)]
    out = _dot(merged[0], wo_ref[groups[0], :])
    for mg, c in zip(merged[1:], groups[1:]):
        out += _dot(mg, wo_ref[c, :])
    o_ref[...] += out

    @pl.when(j == pl.num_programs(1) - 1)
    def _():
        _postnorm_residual_to(o_ref, h_ref, o_ref, gpost_ref, 1.0)


def _merge(h, g_post, y_a, y_b, gates, w_pa, w_pb, w_o, *, tm, tn):
    t, d = h.shape
    wa = y_a.shape[1]
    wb = y_b.shape[1]
    return pl.pallas_call(
        _merge_kernel,
        grid=(t // tm, d // tn),
        in_specs=[
            pl.BlockSpec((tm, d), lambda i, j: (i, 0)),
            pl.BlockSpec((1, d), lambda i, j: (0, 0)),
            pl.BlockSpec((tm, wa), lambda i, j: (i, 0)),
            pl.BlockSpec((tm, wb), lambda i, j: (i, 0)),
            pl.BlockSpec((tm, tn), lambda i, j: (i, j)),
            pl.BlockSpec((tm, tn), lambda i, j: (i, d // tn + j)),
            pl.BlockSpec((wa, tn), lambda i, j: (0, j)),
            pl.BlockSpec((wb, tn), lambda i, j: (0, j)),
            pl.BlockSpec((tn, d), lambda i, j: (j, 0)),
        ],
        out_specs=pl.BlockSpec((tm, d), lambda i, j: (i, 0)),
        out_shape=jax.ShapeDtypeStruct((t, d), F32),
        compiler_params=_params("parallel", "arbitrary"),
        name="merge",
    )(h, g_post.reshape(1, d), y_a, y_b, gates, gates, w_pa, w_pb, w_o)


def _ple_kernel(h_ref, gpre_ref, gpost_ref, p_ref, wg_ref, wp_ref, o_ref):
    tm = h_ref.shape[0]
    groups = [slice(r0, r0 + tm // PLE_ROW_GROUPS) for r0 in range(0, tm, tm // PLE_ROW_GROUPS)]
    emb = [_dot(p_ref[r, :].astype(BF16), wp_ref[...]) for r in groups]
    gates = [_dot(_rms(h_ref[r, :], gpre_ref[...]).astype(BF16), wg_ref[...]) for r in groups]
    for r, e, g in zip(groups, emb, gates):
        o_ref[r, :] = h_ref[r, :] + _rms(jax.nn.sigmoid(g) * e, gpost_ref[...])


def _ple(h, g_pre, g_post, p2, w_g, w_p, *, tm):
    t, d = h.shape
    dp = p2.shape[1]
    return pl.pallas_call(
        _ple_kernel,
        grid=(t // tm,),
        in_specs=[
            pl.BlockSpec((tm, d), lambda i: (i, 0)),
            pl.BlockSpec((1, d), lambda i: (0, 0)),
            pl.BlockSpec((1, d), lambda i: (0, 0)),
            pl.BlockSpec((tm, dp), lambda i: (i, 0)),
            pl.BlockSpec((d, d), lambda i: (0, 0)),
            pl.BlockSpec((dp, d), lambda i: (0, 0)),
        ],
        out_specs=pl.BlockSpec((tm, d), lambda i: (i, 0)),
        out_shape=jax.ShapeDtypeStruct((t, d), F32),
        compiler_params=_params("parallel"),
        name="ple",
    )(h, g_pre.reshape(1, d), g_post.reshape(1, d), p2, w_g, w_p)


def _tile(n, want):
    t = min(n, want)
    while n % t:
        t //= 2
    return t


def kernel(x, p, ffn1_pre_g, ffn1_post_g, ffn1_w_gate, ffn1_w_up, ffn1_w_down, mix_pre_g, mix_post_g, mix_w_in, fox_f_bias, hgrn_lb_logits, hgrn_norm_g, mix_w_proj_fox, mix_w_proj_hgrn, mix_w_out, ffn2_pre_g, ffn2_post_g, ffn2_w_gate, ffn2_w_up, ffn2_w_down, ple_pre_g, ple_post_g, ple_w_gate, ple_w_proj):
    b, s, d = x.shape
    t = b * s
    depth = ffn1_pre_g.shape[0]
    assert depth == 1, "the HGRN2 lower bound is evaluated for a single layer"
    nh, dh = NUM_HEADS, HEAD_DIM
    width = nh * dh
    assert mix_w_in.shape[-1] == 3 * width + nh + 4 * width + 2 * d

    tm_big = _tile(t, 1024)
    h = x.reshape(t, d)
    for i in range(depth):
        h = _ffn(h, ffn1_pre_g[i], ffn1_post_g[i], ffn1_w_gate[i].astype(BF16),
                 ffn1_w_up[i].astype(BF16), ffn1_w_down[i].astype(BF16),
                 tm=tm_big, tf=_tile(ffn1_w_gate.shape[-1], 512))

        w_in = mix_w_in[i]
        o_f = 3 * width
        o_b = o_f + nh
        w_bf = w_in.astype(BF16)
        w_rest = w_bf[:, o_b:]

        main, gates, f = _proj_in(h, mix_pre_g[i], w_bf, w_rest, tm=tm_big, width=width,
                                  gate_width=2 * d, q_scale=dh ** -0.5 * LOG2E)
        main4 = main.reshape(6 * nh, b, s, dh)
        f4 = f.reshape(nh + 1, b, s, dh)
        c = _fox_gate(f4, nh, fox_f_bias[i], ts=_tile(s, 512))
        y_a, w2_gate, w2_up = _fox_attn(main4, c.reshape(b, nh, 1, s),
                                        [ffn2_w_gate[i], ffn2_w_up[i]], tq=_tile(s, 512))
        y_b, w2_down, w_pa, w_pb, w_o, w_pg = _hgrn(
            main4, f4, (3 * nh, 4 * nh, 5 * nh), hgrn_lb_logits, hgrn_norm_g[i],
            [ffn2_w_down[i], mix_w_proj_fox[i], mix_w_proj_hgrn[i], mix_w_out[i], ple_w_gate[i]],
            lc=_tile(s, 2048))
        h = _merge(h, mix_post_g[i], y_a.reshape(t, width), y_b.reshape(t, width), gates,
                   w_pa, w_pb, w_o, tm=tm_big, tn=_tile(d, 512))

        h = _ffn(h, ffn2_pre_g[i], ffn2_post_g[i], w2_gate, w2_up, w2_down,
                 tm=tm_big, tf=_tile(ffn2_w_gate.shape[-1], 512))
        h = _ple(h, ple_pre_g[i], ple_post_g[i], p[i].reshape(t, -1),
                 w_pg, ple_w_proj[i].astype(BF16), tm=_tile(t, 512))
    return h.reshape(b, s, d)
```

```python
import functools
import math

import jax
import jax.numpy as jnp
from jax import lax
from jax.experimental import pallas as pl
from jax.experimental.pallas import tpu as pltpu

NORM_EPS = 1e-6
MACARON_SCALE = 0.5
LOG2E = math.log2(math.e)
HEAD_DIM = 128
NUM_HEADS = 8
HGRN_CHUNK = 64
HGRN_SUB = 16
HGRN_BLOCK = 256
HGRN_GROUP = 4
HGRN_SAFE_LOG2 = 100.0
FFN_COL_GROUPS = 2
MERGE_COL_GROUPS = 2
PLE_ROW_GROUPS = 2
NORM_ROWS = 16
SUBLANES = 8
BF16_ROWS = 16
NEG_BIG = -1e30

VMEM_LIMIT_BYTES = 60 * 1024 * 1024

BF16 = jnp.bfloat16
F32 = jnp.float32


def _params(*semantics):
    return pltpu.CompilerParams(dimension_semantics=semantics,
                                vmem_limit_bytes=VMEM_LIMIT_BYTES)


def _rms(x, g):
    ms = jnp.mean(x * x, axis=-1, keepdims=True)
    return x * lax.rsqrt(ms + NORM_EPS) * g


def _postnorm_residual_to(o_ref, x_ref, y_ref, g_ref, scale):
    g = g_ref[...] if scale == 1.0 else scale * g_ref[...]
    for r0 in range(0, x_ref.shape[0], NORM_ROWS):
        rows = slice(r0, r0 + NORM_ROWS)
        o_ref[rows, :] = x_ref[rows, :] + _rms(y_ref[rows, :], g)


def _sigmoid(x):
    return 0.5 + 0.5 * jnp.tanh(0.5 * x)


def _silu(x):
    h = 0.5 * x
    return h + h * jnp.tanh(h)


def _dot(a, b):
    return jnp.dot(a, b, preferred_element_type=F32)


def _dot_nt(a, b):
    return lax.dot_general(a, b, (((1,), (1,)), ((), ())), preferred_element_type=F32)


def _split3(x):
    hi = x.astype(BF16)
    r = x - hi.astype(F32)
    mid = r.astype(BF16)
    lo = (r - mid.astype(F32)).astype(BF16)
    return hi, mid, lo


def _with_passengers(body, n_in, n_out, n_pass):
    def kernel(*refs):
        ins, refs = refs[:n_in], refs[n_in:]
        pass_in, refs = refs[:n_pass], refs[n_pass:]
        outs, refs = refs[:n_out], refs[n_out:]
        pass_out, scratch = refs[:n_pass], refs[n_pass:]
        for src, dst in zip(pass_in, pass_out):
            dst[...] = src[...].astype(dst.dtype)
        body(*ins, *outs, *scratch)

    return kernel


def _passenger_specs(weights, n_steps, step_of):
    in_specs, out_specs, out_shapes = [], [], []
    for w in weights:
        r, c = w.shape
        share = 1
        while (r * share) % n_steps or (r * share // n_steps) % BF16_ROWS:
            share *= 2
        spec = pl.BlockSpec((r * share // n_steps, c), lambda *g, share=share: (step_of(*g) // share, 0))
        in_specs.append(spec)
        out_specs.append(spec)
        out_shapes.append(jax.ShapeDtypeStruct((r, c), BF16))
    return in_specs, out_specs, out_shapes


def _ffn_kernel(x_ref, gpre_ref, gpost_ref, wg_ref, wu_ref, wd_ref, o_ref, xn_ref):
    j = pl.program_id(1)

    def sweep_step(first, last):
        if first:
            xn_ref[...] = _rms(x_ref[...], gpre_ref[...]).astype(BF16)
        xn = xn_ref[...]
        tf = wg_ref.shape[1]
        groups = [slice(c0, c0 + tf // FFN_COL_GROUPS) for c0 in range(0, tf, tf // FFN_COL_GROUPS)]
        gu = [(_dot(xn, wg_ref[:, c]), _dot(xn, wu_ref[:, c])) for c in groups]
        acts = [(g * jax.nn.sigmoid(g) * u).astype(BF16) for g, u in gu]
        down = _dot(acts[0], wd_ref[groups[0], :])
        for a, c in zip(acts[1:], groups[1:]):
            down += _dot(a, wd_ref[c, :])
        if first:
            o_ref[...] = down
        else:
            o_ref[...] += down
        if last:
            _postnorm_residual_to(o_ref, x_ref, o_ref, gpost_ref, MACARON_SCALE)

    n_steps = pl.num_programs(1)
    pl.when(j == 0)(functools.partial(sweep_step, True, False))
    pl.when((j > 0) & (j < n_steps - 1))(functools.partial(sweep_step, False, False))
    pl.when(j == n_steps - 1)(functools.partial(sweep_step, False, True))


def _ffn(h, g_pre, g_post, w_gate, w_up, w_down, *, tm, tf):
    t, d = h.shape
    f = w_gate.shape[1]
    return pl.pallas_call(
        _ffn_kernel,
        grid=(t // tm, f // tf),
        in_specs=[
            pl.BlockSpec((tm, d), lambda i, j: (i, 0)),
            pl.BlockSpec((1, d), lambda i, j: (0, 0)),
            pl.BlockSpec((1, d), lambda i, j: (0, 0)),
            pl.BlockSpec((d, tf), lambda i, j: (0, j)),
            pl.BlockSpec((d, tf), lambda i, j: (0, j)),
            pl.BlockSpec((tf, d), lambda i, j: (j, 0)),
        ],
        out_specs=pl.BlockSpec((tm, d), lambda i, j: (i, 0)),
        out_shape=jax.ShapeDtypeStruct((t, d), F32),
        scratch_shapes=[pltpu.VMEM((tm, d), BF16)],
        compiler_params=_params("parallel", "arbitrary"),
        name="ffn",
    )(h, g_pre.reshape(1, d), g_post.reshape(1, d), w_gate, w_up, w_down)


PROJ_FOX_STEPS = 3
PROJ_MAIN_STEPS = 6


def _proj_in_kernel(x_ref, g_ref, wa_ref, wr_ref, wfa_ref, o_ref, og_ref, of_ref, xn_ref, *, q_scale):
    j = pl.program_id(1)
    last = pl.num_programs(1) - 1

    def store_heads(dst_ref, y):
        for hh in range(dst_ref.shape[0]):
            dst_ref[hh] = y[:, hh * HEAD_DIM:(hh + 1) * HEAD_DIM].astype(dst_ref.dtype)

    @pl.when(j == 0)
    def _():
        xn_ref[...] = _rms(x_ref[...], g_ref[...]).astype(BF16)
        store_heads(o_ref, _dot(xn_ref[...], wa_ref[...]) * q_scale)

    @pl.when((j > 0) & (j < PROJ_FOX_STEPS))
    def _():
        store_heads(o_ref, _dot(xn_ref[...], wa_ref[...]))

    @pl.when((j >= PROJ_FOX_STEPS) & (j < PROJ_MAIN_STEPS))
    def _():
        store_heads(o_ref, _dot(xn_ref[...], wr_ref[...]))

    @pl.when((j >= PROJ_MAIN_STEPS) & (j < last))
    def _():
        og_ref[...] = _sigmoid(_dot(xn_ref[...], wr_ref[...])).astype(og_ref.dtype)

    @pl.when(j == last)
    def _():
        nh = wr_ref.shape[1] // HEAD_DIM
        store_heads(of_ref.at[:nh], _dot(xn_ref[...], wr_ref[...]))
        of_ref[nh] = _dot(xn_ref[...], wfa_ref[...])


def _proj_in(h, g, w_fox, w_rest, *, tm, width, gate_width, q_scale):
    t, d = h.shape
    nh = width // HEAD_DIM
    assert w_fox.shape[1] >= PROJ_FOX_STEPS * width + HEAD_DIM
    assert w_rest.shape[1] == 4 * width + gate_width and gate_width % width == 0
    n_gate = gate_width // width
    last = PROJ_MAIN_STEPS + n_gate

    def rest_col(i, j):
        return (0, jnp.where(j <= 3, 0, jnp.where(j == last, 1, j - 2)))

    def gate_step(j):
        return jnp.clip(j - PROJ_MAIN_STEPS, 0, n_gate - 1)

    return pl.pallas_call(
        functools.partial(_proj_in_kernel, q_scale=q_scale),
        grid=(t // tm, last + 1),
        in_specs=[
            pl.BlockSpec((tm, d), lambda i, j: (i, 0)),
            pl.BlockSpec((1, d), lambda i, j: (0, 0)),
            pl.BlockSpec((d, width), lambda i, j: (0, jnp.minimum(j, PROJ_FOX_STEPS - 1))),
            pl.BlockSpec((d, width), rest_col),
            pl.BlockSpec((d, HEAD_DIM), lambda i, j: (0, PROJ_FOX_STEPS * nh)),
        ],
        out_specs=[
            pl.BlockSpec((nh, tm, HEAD_DIM), lambda i, j: (jnp.minimum(j, PROJ_MAIN_STEPS - 1), i, 0)),
            pl.BlockSpec((tm, width), lambda i, j: (i, gate_step(j))),
            pl.BlockSpec((nh + 1, tm, HEAD_DIM), lambda i, j: (0, i, 0)),
        ],
        out_shape=[jax.ShapeDtypeStruct((PROJ_MAIN_STEPS * nh, t, HEAD_DIM), BF16),
                   jax.ShapeDtypeStruct((t, gate_width), BF16),
                   jax.ShapeDtypeStruct((nh + 1, t, HEAD_DIM), F32)],
        scratch_shapes=[pltpu.VMEM((tm, d), BF16)],
        compiler_params=_params("parallel", "arbitrary"),
        name="proj_in",
    )(h, g.reshape(1, d), w_fox, w_rest, w_fox)


def _fox_gate_kernel(f_ref, b_ref, o_ref, carry_ref, *, ts):
    @pl.when(pl.program_id(1) == 0)
    def _():
        carry_ref[...] = jnp.zeros_like(carry_ref)

    z = f_ref[...].T[:NUM_HEADS, :] + b_ref[...]
    lf = (jnp.minimum(z, 0.0) - jnp.log1p(jnp.exp(-jnp.abs(z)))) * LOG2E
    src = lax.broadcasted_iota(jnp.int32, (ts, ts), 0)
    dst = lax.broadcasted_iota(jnp.int32, (ts, ts), 1)
    tri = jnp.where(src <= dst, 1.0, 0.0).astype(BF16)
    hi, mid, lo = _split3(lf)
    c = _dot(hi, tri) + _dot(mid, tri) + _dot(lo, tri) + carry_ref[:, :1]
    o_ref[...] = c
    carry_ref[...] = jnp.broadcast_to(c[:, ts - 1:ts], carry_ref.shape)


def _fox_gate(f4, slab, bias, *, ts):
    _, b, s, _ = f4.shape
    nh = NUM_HEADS
    return pl.pallas_call(
        functools.partial(_fox_gate_kernel, ts=ts),
        grid=(b, s // ts),
        in_specs=[
            pl.BlockSpec((None, None, ts, HEAD_DIM), lambda i, j: (slab, i, j, 0)),
            pl.BlockSpec((nh, 1), lambda i, j: (0, 0)),
        ],
        out_specs=pl.BlockSpec((None, nh, ts), lambda i, j: (i, 0, j)),
        out_shape=jax.ShapeDtypeStruct((b, nh, s), F32),
        scratch_shapes=[pltpu.VMEM((nh, HEAD_DIM), F32)],
        compiler_params=_params("parallel", "arbitrary"),
        name="fox_gate",
    )(f4, bias.reshape(nh, 1))


def _fox_attn_kernel(q_ref, k_ref, v_ref, c_ref, o_ref, m_ref, l_ref, acc_ref, *, tq):
    nq = q_ref.shape[0] // tq
    reps = tq // HEAD_DIM

    m_ref[...] = jnp.full_like(m_ref, NEG_BIG)
    l_ref[...] = jnp.zeros_like(l_ref)
    acc_ref[...] = jnp.zeros_like(acc_ref)

    def step(qi, ki):
        rows = slice(qi * tq, (qi + 1) * tq)
        cols = slice(ki * tq, (ki + 1) * tq)
        c0 = jnp.max(c_ref[:, rows], axis=-1, keepdims=True)
        s = _dot_nt(q_ref[rows, :], k_ref[cols, :]) + (c0 - c_ref[:, cols])
        if qi == ki:
            row = lax.broadcasted_iota(jnp.int32, (tq, tq), 0)
            col = lax.broadcasted_iota(jnp.int32, (tq, tq), 1)
            s = jnp.where(col <= row, s, NEG_BIG)
        m_prev = m_ref[rows, :]
        m_next = jnp.maximum(m_prev, jnp.max(s, axis=-1, keepdims=True))
        p = jnp.exp2(s - jnp.concatenate([m_next] * reps, axis=1))
        alpha = jnp.exp2(m_prev - m_next)
        l_ref[rows, :] = alpha * l_ref[rows, :] + jnp.sum(p, axis=-1, keepdims=True)
        acc_ref[rows, :] = alpha * acc_ref[rows, :] + _dot(p.astype(BF16), v_ref[cols, :])
        m_ref[rows, :] = m_next

    for diag in range(nq):
        for qi in range(diag, nq):
            step(qi, qi - diag)
    o_ref[...] = (acc_ref[...] / l_ref[...]).astype(o_ref.dtype)


def _fox_attn(main4, c4, passengers, *, tq):
    _, b, s, _ = main4.shape
    nh, dh = NUM_HEADS, HEAD_DIM
    p_in, p_out, p_shapes = _passenger_specs(passengers, b * nh, lambda i, h: i * nh + h)
    return pl.pallas_call(
        _with_passengers(functools.partial(_fox_attn_kernel, tq=tq), 4, 1, len(passengers)),
        grid=(b, nh),
        in_specs=[
            pl.BlockSpec((None, None, s, dh), lambda i, h: (h, i, 0, 0)),
            pl.BlockSpec((None, None, s, dh), lambda i, h: (nh + h, i, 0, 0)),
            pl.BlockSpec((None, None, s, dh), lambda i, h: (2 * nh + h, i, 0, 0)),
            pl.BlockSpec((None, None, 1, s), lambda i, h: (i, h, 0, 0)),
        ] + p_in,
        out_specs=[pl.BlockSpec((None, s, dh), lambda i, h: (i, 0, h))] + p_out,
        out_shape=[jax.ShapeDtypeStruct((b, s, nh * dh), BF16)] + p_shapes,
        scratch_shapes=[pltpu.VMEM((s, dh), F32), pltpu.VMEM((s, dh), F32),
                        pltpu.VMEM((s, dh), F32)],
        compiler_params=_params("parallel", "parallel"),
        name="fox_attn",
    )(main4, main4, main4, c4, *passengers)


def _hgrn_kernel(q_ref, f_ref, i_ref, g_ref, lbl_ref, ng_ref, o_ref,
                 st_ref, qe_ref, intra_ref, u_ref, dec_ref, a_ref, v_ref, *, lc):
    cs, sb, blk = HGRN_CHUNK, HGRN_SUB, HGRN_BLOCK
    half = sb // 2
    cpb = blk // cs

    @pl.when(pl.program_id(2) == 0)
    def _():
        st_ref[...] = jnp.zeros_like(st_ref)

    logits = lbl_ref[...]
    e = jnp.exp(logits - jnp.max(logits, axis=0, keepdims=True))
    lb = e[0:1, :] / jnp.sum(e, axis=0, keepdims=True)
    ng = ng_ref[...]

    src = lax.broadcasted_iota(jnp.int32, (blk, blk), 1)
    dst = lax.broadcasted_iota(jnp.int32, (blk, blk), 0)
    same_chunk = (src // cs) == (dst // cs)
    tri = jnp.where(same_chunk & (src <= dst), 1.0, 0.0).astype(BF16)
    half_row = lax.broadcasted_iota(jnp.int32, (half, HEAD_DIM), 0)
    blk_col = lax.broadcasted_iota(jnp.int32, (sb, cs), 1)
    chunk_row = lax.broadcasted_iota(jnp.int32, (cs, cs), 0)
    chunk_col = lax.broadcasted_iota(jnp.int32, (cs, cs), 1)

    def prep(rows):
        q = _silu(q_ref[rows, :].astype(F32))
        f = lb + (1.0 - lb) * _sigmoid(f_ref[rows, :])
        kk = 1.0 - f
        hi, mid, lo = _split3(jnp.log2(f))
        cum = _dot(tri, hi) + _dot(tri, mid) + _dot(tri, lo)
        lasts = [cum[c0 + cs - 1:c0 + cs, :] for c0 in range(0, blk, cs)]
        return q, kk, cum, lasts

    def state_increment(ci, kk, cum, lasts, v):
        c0 = ci * cs
        kdec = (kk[c0:c0 + cs, :] * jnp.exp2(lasts[ci] - cum[c0:c0 + cs, :])).astype(BF16)
        return _dot(v[c0:c0 + cs, :].T.astype(BF16), kdec)

    def finish(rows, outs):
        o = _rms(jnp.concatenate(outs, axis=0), ng) * _silu(g_ref[rows, :].astype(F32))
        o_ref[rows, :] = o.astype(o_ref.dtype)

    def precompute(gi, deepest):
        blocks = [gi * HGRN_GROUP + k for k in range(HGRN_GROUP)]
        rows = [pl.ds(pl.multiple_of(bi * blk, blk), blk) for bi in blocks]
        stage1 = [prep(r) for r in rows]
        stage2 = []
        for bi, r, (q, kk, cum, lasts) in zip(blocks, rows, stage1):
            v = i_ref[r, :]
            v32 = v.astype(F32)
            qe = (q * jnp.exp2(cum)).astype(BF16)
            kinv = (kk * jnp.exp2(-cum)).astype(BF16)
            qe_ref[r, :] = qe
            scores = [_dot_nt(qe[c0:c0 + cs, :], kinv[c0:c0 + cs, :]) for c0 in range(0, blk, cs)]
            for ci in range(cpb):
                u_ref[bi * cpb + ci] = state_increment(ci, kk, cum, lasts, v32)
                dec_ref[pl.ds(pl.multiple_of((bi * cpb + ci) * SUBLANES, SUBLANES), SUBLANES), :] = (
                    jnp.broadcast_to(jnp.exp2(lasts[ci]), (SUBLANES, HEAD_DIM)))
                deepest = jnp.maximum(deepest, -lasts[ci])
            stage2.append((v, scores))
        for r, (v, scores) in zip(rows, stage2):
            intra = [_dot(jnp.where(chunk_col <= chunk_row, sc, 0.0).astype(BF16), v[ci * cs:(ci + 1) * cs, :])
                     for ci, sc in enumerate(scores)]
            intra_ref[r, :] = jnp.concatenate(intra, axis=0)
        return deepest

    deepest = lax.fori_loop(0, lc // (blk * HGRN_GROUP), precompute, jnp.zeros((1, HEAD_DIM), F32))
    depth = jnp.max(deepest)

    @pl.when(depth <= HGRN_SAFE_LOG2)
    def _():
        def recur(bi, carry):
            base = pl.multiple_of(bi * blk, blk)
            st = st_ref[...]
            outs = []
            for ci in range(cpb):
                rows_c = pl.ds(base + ci * cs, cs)
                dec = dec_ref[pl.ds(pl.multiple_of((bi * cpb + ci) * SUBLANES, SUBLANES), SUBLANES), :]
                outs.append(_dot_nt(qe_ref[rows_c, :], st.astype(BF16)) + intra_ref[rows_c, :])
                st = st * dec[0:1, :] + u_ref[bi * cpb + ci]
            st_ref[...] = st
            finish(pl.ds(base, blk), outs)
            return carry

        lax.fori_loop(0, lc // blk, recur, 0, unroll=4)

    @pl.when(jnp.logical_not(depth <= HGRN_SAFE_LOG2))
    def _():
        def general(bi, carry):
            rows = pl.ds(pl.multiple_of(bi * blk, blk), blk)
            q, kk, cum, lasts = prep(rows)
            v = i_ref[rows, :]
            v32 = v.astype(F32)
            a_ref[...] = cum - jnp.log2(kk)
            v_ref[...] = v32
            st = st_ref[...]
            outs = []
            for ci in range(cpb):
                c0 = ci * cs
                cum_c = cum[c0:c0 + cs, :]
                q_c = q[c0:c0 + cs, :]
                kk_c = kk[c0:c0 + cs, :]
                inter = _dot_nt((q_c * jnp.exp2(cum_c)).astype(BF16), st.astype(BF16))
                st = st * jnp.exp2(lasts[ci]) + state_increment(ci, kk, cum, lasts, v32)

                score_rows = [jnp.zeros((sb, cs), F32)]
                for si in range(1, cs // sb):
                    r0 = si * sb
                    ref = cum_c[r0 - 1:r0, :]
                    qt = q_c[r0:r0 + sb, :] * jnp.exp2(cum_c[r0:r0 + sb, :] - ref)
                    kt = kk_c * jnp.exp2(jnp.minimum(ref - cum_c, 0.0))
                    sc = _dot_nt(qt.astype(BF16), kt.astype(BF16))
                    score_rows.append(jnp.where(blk_col < r0, sc, 0.0))
                off = _dot(jnp.concatenate(score_rows, axis=0).astype(BF16), v[c0:c0 + cs, :])

                diag_rows = []
                for si in range(cs // sb):
                    r0 = c0 + si * sb
                    q_lo, q_hi = q[r0:r0 + half, :], q[r0 + half:r0 + sb, :]
                    c_lo, c_hi = cum[r0:r0 + half, :], cum[r0 + half:r0 + sb, :]
                    acc_lo = jnp.zeros((half, HEAD_DIM), F32)
                    acc_hi = jnp.zeros((half, HEAD_DIM), F32)
                    for ti in range(sb):
                        a_s = a_ref[r0 + ti:r0 + ti + 1, :]
                        v_s = v_ref[r0 + ti:r0 + ti + 1, :]
                        if ti < half:
                            d = c_lo - a_s
                            if ti > 0:
                                d = jnp.where(half_row >= ti, d, NEG_BIG)
                            acc_lo = acc_lo + jnp.sum(q_lo * jnp.exp2(d), axis=-1, keepdims=True) * v_s
                            d = c_hi - a_s
                        else:
                            d = c_hi - a_s
                            if ti > half:
                                d = jnp.where(half_row >= ti - half, d, NEG_BIG)
                        acc_hi = acc_hi + jnp.sum(q_hi * jnp.exp2(d), axis=-1, keepdims=True) * v_s
                    diag_rows += [acc_lo, acc_hi]
                outs.append(inter + off + jnp.concatenate(diag_rows, axis=0))
            st_ref[...] = st
            finish(rows, outs)
            return carry

        lax.fori_loop(0, lc // blk, general, 0)


def _hgrn(main4, f4, slabs, lb_logits, norm_g, passengers, *, lc):
    _, b, s, _ = main4.shape
    nh, dh = NUM_HEADS, HEAD_DIM
    nl = lb_logits.shape[0]
    assert lc % (HGRN_BLOCK * HGRN_GROUP) == 0
    q_blk, i_blk, g_blk = slabs

    def col(first):
        return pl.BlockSpec((None, None, lc, dh), lambda i, h, j: (first + h, i, j, 0))

    n_lc = s // lc
    p_in, p_out, p_shapes = _passenger_specs(passengers, b * nh * n_lc,
                                             lambda i, h, j: (i * nh + h) * n_lc + j)
    return pl.pallas_call(
        _with_passengers(functools.partial(_hgrn_kernel, lc=lc), 6, 1, len(passengers)),
        grid=(b, nh, n_lc),
        in_specs=[
            col(q_blk), col(0), col(i_blk), col(g_blk),
            pl.BlockSpec((nl, dh), lambda i, h, j: (0, h)),
            pl.BlockSpec((1, dh), lambda i, h, j: (0, 0)),
        ] + p_in,
        out_specs=[pl.BlockSpec((None, lc, dh), lambda i, h, j: (i, j, h))] + p_out,
        out_shape=[jax.ShapeDtypeStruct((b, s, nh * dh), BF16)] + p_shapes,
        scratch_shapes=[
            pltpu.VMEM((dh, dh), F32),
            pltpu.VMEM((lc, dh), BF16),
            pltpu.VMEM((lc, dh), F32),
            pltpu.VMEM((lc // HGRN_CHUNK, dh, dh), F32),
            pltpu.VMEM((lc // HGRN_CHUNK * SUBLANES, dh), F32),
            pltpu.VMEM((HGRN_BLOCK, dh), F32),
            pltpu.VMEM((HGRN_BLOCK, dh), F32),
        ],
        compiler_params=_params("parallel", "parallel", "arbitrary"),
        name="hgrn",
    )(main4, f4, main4, main4, lb_logits, norm_g.reshape(1, dh), *passengers)


def _merge_kernel(h_ref, gpost_ref, ya_ref, yb_ref, ga_ref, gb_ref, wpa_ref, wpb_ref, wo_ref, o_ref):
    j = pl.program_id(1)

    def sweep_step(first, last):
        tn = wo_ref.shape[0]
        groups = [slice(c0, c0 + tn // MERGE_COL_GROUPS) for c0 in range(0, tn, tn // MERGE_COL_GROUPS)]
        stage1 = [(_dot(ya_ref[...], wpa_ref[:, c]), _dot(yb_ref[...], wpb_ref[:, c])) for c in groups]
        merged = [(ga_ref[:, c].astype(F32) * y_a + gb_ref[:, c].astype(F32) * y_b).astype(BF16)
                  for c, (y_a, y_b) in zip(groups, stage1)]
        out = _dot(merged[0], wo_ref[groups[0], :])
        for mg, c in zip(merged[1:], groups[1:]):
            out += _dot(mg, wo_ref[c, :])
        if first:
            o_ref[...] = out
        else:
            o_ref[...] += out
        if last:
            _postnorm_residual_to(o_ref, h_ref, o_ref, gpost_ref, 1.0)

    n_steps = pl.num_programs(1)
    pl.when(j == 0)(functools.partial(sweep_step, True, False))
    pl.when((j > 0) & (j < n_steps - 1))(functools.partial(sweep_step, False, False))
    pl.when(j == n_steps - 1)(functools.partial(sweep_step, False, True))


def _merge(h, g_post, y_a, y_b, gates, w_pa, w_pb, w_o, *, tm, tn):
    t, d = h.shape
    wa = y_a.shape[1]
    wb = y_b.shape[1]
    n_j = d // tn

    def residual_rows(i, j):
        return (jnp.where(j == n_j - 1, i, jnp.maximum(i - 1, 0)), 0)

    return pl.pallas_call(
        _merge_kernel,
        grid=(t // tm, n_j),
        in_specs=[
            pl.BlockSpec((tm, d), residual_rows),
            pl.BlockSpec((1, d), lambda i, j: (0, 0)),
            pl.BlockSpec((tm, wa), lambda i, j: (i, 0)),
            pl.BlockSpec((tm, wb), lambda i, j: (i, 0)),
            pl.BlockSpec((tm, tn), lambda i, j: (i, j)),
            pl.BlockSpec((tm, tn), lambda i, j: (i, d // tn + j)),
            pl.BlockSpec((wa, tn), lambda i, j: (0, j)),
            pl.BlockSpec((wb, tn), lambda i, j: (0, j)),
            pl.BlockSpec((tn, d), lambda i, j: (j, 0)),
        ],
        out_specs=pl.BlockSpec((tm, d), lambda i, j: (i, 0)),
        out_shape=jax.ShapeDtypeStruct((t, d), F32),
        compiler_params=_params("parallel", "arbitrary"),
        name="merge",
    )(h, g_post.reshape(1, d), y_a, y_b, gates, gates, w_pa, w_pb, w_o)


def _ple_kernel(h_ref, gpre_ref, gpost_ref, p_ref, wg_ref, wp_ref, o_ref):
    tm = h_ref.shape[0]
    groups = [slice(r0, r0 + tm // PLE_ROW_GROUPS) for r0 in range(0, tm, tm // PLE_ROW_GROUPS)]
    emb = [_dot(p_ref[r, :].astype(BF16), wp_ref[...]) for r in groups]
    gates = [_dot(_rms(h_ref[r, :], gpre_ref[...]).astype(BF16), wg_ref[...]) for r in groups]
    for r, e, g in zip(groups, emb, gates):
        o_ref[r, :] = h_ref[r, :] + _rms(jax.nn.sigmoid(g) * e, gpost_ref[...])


def _ple(h, g_pre, g_post, p2, w_g, w_p, *, tm):
    t, d = h.shape
    dp = p2.shape[1]
    return pl.pallas_call(
        _ple_kernel,
        grid=(t // tm,),
        in_specs=[
            pl.BlockSpec((tm, d), lambda i: (i, 0)),
            pl.BlockSpec((1, d), lambda i: (0, 0)),
            pl.BlockSpec((1, d), lambda i: (0, 0)),
            pl.BlockSpec((tm, dp), lambda i: (i, 0)),
            pl.BlockSpec((d, d), lambda i: (0, 0)),
            pl.BlockSpec((dp, d), lambda i: (0, 0)),
        ],
        out_specs=pl.BlockSpec((tm, d), lambda i: (i, 0)),
        out_shape=jax.ShapeDtypeStruct((t, d), F32),
        compiler_params=_params("parallel"),
        name="ple",
    )(h, g_pre.reshape(1, d), g_post.reshape(1, d), p2, w_g, w_p)


def _tile(n, want):
    t = min(n, want)
    while n % t:
        t //= 2
    return t


def kernel(x, p, ffn1_pre_g, ffn1_post_g, ffn1_w_gate, ffn1_w_up, ffn1_w_down, mix_pre_g, mix_post_g, mix_w_in, fox_f_bias, hgrn_lb_logits, hgrn_norm_g, mix_w_proj_fox, mix_w_proj_hgrn, mix_w_out, ffn2_pre_g, ffn2_post_g, ffn2_w_gate, ffn2_w_up, ffn2_w_down, ple_pre_g, ple_post_g, ple_w_gate, ple_w_proj):
    b, s, d = x.shape
    t = b * s
    depth = ffn1_pre_g.shape[0]
    assert depth == 1, "the HGRN2 lower bound is evaluated for a single layer"
    nh, dh = NUM_HEADS, HEAD_DIM
    width = nh * dh
    assert mix_w_in.shape[-1] == 3 * width + nh + 4 * width + 2 * d

    tm_big = _tile(t, 1024)
    h = x.reshape(t, d)
    for i in range(depth):
        h = _ffn(h, ffn1_pre_g[i], ffn1_post_g[i], ffn1_w_gate[i].astype(BF16),
                 ffn1_w_up[i].astype(BF16), ffn1_w_down[i].astype(BF16),
                 tm=tm_big, tf=_tile(ffn1_w_gate.shape[-1], 512))

        w_in = mix_w_in[i]
        o_f = 3 * width
        o_b = o_f + nh
        w_bf = w_in.astype(BF16)
        w_rest = w_bf[:, o_b:]

        main, gates, f = _proj_in(h, mix_pre_g[i], w_bf, w_rest, tm=tm_big, width=width,
                                  gate_width=2 * d, q_scale=dh ** -0.5 * LOG2E)
        main4 = main.reshape(6 * nh, b, s, dh)
        f4 = f.reshape(nh + 1, b, s, dh)
        c = _fox_gate(f4, nh, fox_f_bias[i], ts=_tile(s, 512))
        y_a, w2_gate, w2_up = _fox_attn(main4, c.reshape(b, nh, 1, s),
                                        [ffn2_w_gate[i], ffn2_w_up[i]], tq=_tile(s, 512))
        y_b, w2_down, w_pa, w_pb, w_o, w_pg = _hgrn(
            main4, f4, (3 * nh, 4 * nh, 5 * nh), hgrn_lb_logits, hgrn_norm_g[i],
            [ffn2_w_down[i], mix_w_proj_fox[i], mix_w_proj_hgrn[i], mix_w_out[i], ple_w_gate[i]],
            lc=_tile(s, 2048))
        h = _merge(h, mix_post_g[i], y_a.reshape(t, width), y_b.reshape(t, width), gates,
                   w_pa, w_pb, w_o, tm=tm_big, tn=_tile(d, 512))

        h = _ffn(h, ffn2_pre_g[i], ffn2_post_g[i], w2_gate, w2_up, w2_down,
                 tm=tm_big, tf=_tile(ffn2_w_gate.shape[-1], 512))
        h = _ple(h, ple_pre_g[i], ple_post_g[i], p[i].reshape(t, -1),
                 w_pg, ple_w_proj[i].astype(BF16), tm=_tile(t, 512))
    return h.reshape(b, s, d)
```

```python
import functools
import math

import jax
import jax.numpy as jnp
from jax import lax
from jax.experimental import pallas as pl
from jax.experimental.pallas import tpu as pltpu

NORM_EPS = 1e-6
MACARON_SCALE = 0.5
LOG2E = math.log2(math.e)
HEAD_DIM = 128
NUM_HEADS = 8
HGRN_CHUNK = 64
HGRN_SUB = 16
HGRN_BLOCK = 256
HGRN_GROUP = 4
HGRN_SAFE_LOG2 = 100.0
FFN_COL_GROUPS = 2
MERGE_COL_GROUPS = 2
PLE_ROW_GROUPS = 2
NORM_ROWS = 16
SUBLANES = 8
BF16_ROWS = 16
NEG_BIG = -1e30

VMEM_LIMIT_BYTES = 60 * 1024 * 1024

BF16 = jnp.bfloat16
F32 = jnp.float32


def _params(*semantics):
    return pltpu.CompilerParams(dimension_semantics=semantics,
                                vmem_limit_bytes=VMEM_LIMIT_BYTES)


def _rms(x, g):
    ms = jnp.mean(x * x, axis=-1, keepdims=True)
    return x * lax.rsqrt(ms + NORM_EPS) * g


def _postnorm_residual_to(o_ref, x_ref, y_ref, g_ref, scale):
    g = g_ref[...] if scale == 1.0 else scale * g_ref[...]
    for r0 in range(0, x_ref.shape[0], NORM_ROWS):
        rows = slice(r0, r0 + NORM_ROWS)
        o_ref[rows, :] = x_ref[rows, :] + _rms(y_ref[rows, :], g)


def _sigmoid(x):
    return 0.5 + 0.5 * jnp.tanh(0.5 * x)


def _silu(x):
    h = 0.5 * x
    return h + h * jnp.tanh(h)


def _dot(a, b):
    return jnp.dot(a, b, preferred_element_type=F32)


def _dot_nt(a, b):
    return lax.dot_general(a, b, (((1,), (1,)), ((), ())), preferred_element_type=F32)


def _split3(x):
    hi = x.astype(BF16)
    r = x - hi.astype(F32)
    mid = r.astype(BF16)
    lo = (r - mid.astype(F32)).astype(BF16)
    return hi, mid, lo


def _with_passengers(body, n_in, n_out, n_pass):
    def kernel(*refs):
        ins, refs = refs[:n_in], refs[n_in:]
        pass_in, refs = refs[:n_pass], refs[n_pass:]
        outs, refs = refs[:n_out], refs[n_out:]
        pass_out, scratch = refs[:n_pass], refs[n_pass:]
        for src, dst in zip(pass_in, pass_out):
            dst[...] = src[...].astype(dst.dtype)
        body(*ins, *outs, *scratch)

    return kernel


def _passenger_specs(weights, n_steps, step_of):
    in_specs, out_specs, out_shapes = [], [], []
    for w in weights:
        r, c = w.shape
        share = 1
        while (r * share) % n_steps or (r * share // n_steps) % BF16_ROWS:
            share *= 2
        spec = pl.BlockSpec((r * share // n_steps, c), lambda *g, share=share: (step_of(*g) // share, 0))
        in_specs.append(spec)
        out_specs.append(spec)
        out_shapes.append(jax.ShapeDtypeStruct((r, c), BF16))
    return in_specs, out_specs, out_shapes


def _ffn_kernel(x_ref, gpre_ref, gpost_ref, wg_ref, wu_ref, wd_ref, o_ref, xn_ref):
    j = pl.program_id(1)

    def sweep_step(first, last):
        if first:
            xn_ref[...] = _rms(x_ref[...], gpre_ref[...]).astype(BF16)
        xn = xn_ref[...]
        tf = wg_ref.shape[1]
        groups = [slice(c0, c0 + tf // FFN_COL_GROUPS) for c0 in range(0, tf, tf // FFN_COL_GROUPS)]
        gu = [(_dot(xn, wg_ref[:, c]), _dot(xn, wu_ref[:, c])) for c in groups]
        acts = [(g * jax.nn.sigmoid(g) * u).astype(BF16) for g, u in gu]
        down = _dot(acts[0], wd_ref[groups[0], :])
        for a, c in zip(acts[1:], groups[1:]):
            down += _dot(a, wd_ref[c, :])
        if first:
            o_ref[...] = down
        else:
            o_ref[...] += down
        if last:
            _postnorm_residual_to(o_ref, x_ref, o_ref, gpost_ref, MACARON_SCALE)

    n_steps = pl.num_programs(1)
    pl.when(j == 0)(functools.partial(sweep_step, True, False))
    pl.when((j > 0) & (j < n_steps - 1))(functools.partial(sweep_step, False, False))
    pl.when(j == n_steps - 1)(functools.partial(sweep_step, False, True))


def _ffn(h, g_pre, g_post, w_gate, w_up, w_down, *, tm, tf):
    t, d = h.shape
    f = w_gate.shape[1]
    return pl.pallas_call(
        _ffn_kernel,
        grid=(t // tm, f // tf),
        in_specs=[
            pl.BlockSpec((tm, d), lambda i, j: (i, 0)),
            pl.BlockSpec((1, d), lambda i, j: (0, 0)),
            pl.BlockSpec((1, d), lambda i, j: (0, 0)),
            pl.BlockSpec((d, tf), lambda i, j: (0, j)),
            pl.BlockSpec((d, tf), lambda i, j: (0, j)),
            pl.BlockSpec((tf, d), lambda i, j: (j, 0)),
        ],
        out_specs=pl.BlockSpec((tm, d), lambda i, j: (i, 0)),
        out_shape=jax.ShapeDtypeStruct((t, d), F32),
        scratch_shapes=[pltpu.VMEM((tm, d), BF16)],
        compiler_params=_params("parallel", "arbitrary"),
        name="ffn",
    )(h, g_pre.reshape(1, d), g_post.reshape(1, d), w_gate, w_up, w_down)


PROJ_FOX_STEPS = 3
PROJ_MAIN_STEPS = 6


def _proj_in_kernel(x_ref, g_ref, wa_ref, wr_ref, wfa_ref, o_ref, og_ref, of_ref, xn_ref, *, q_scale):
    j = pl.program_id(1)
    last = pl.num_programs(1) - 1

    def store_heads(dst_ref, y):
        for hh in range(dst_ref.shape[0]):
            dst_ref[hh] = y[:, hh * HEAD_DIM:(hh + 1) * HEAD_DIM].astype(dst_ref.dtype)

    @pl.when(j == 0)
    def _():
        xn_ref[...] = _rms(x_ref[...], g_ref[...]).astype(BF16)
        store_heads(o_ref, _dot(xn_ref[...], wa_ref[...]) * q_scale)

    @pl.when((j > 0) & (j < PROJ_FOX_STEPS))
    def _():
        store_heads(o_ref, _dot(xn_ref[...], wa_ref[...]))

    @pl.when((j >= PROJ_FOX_STEPS) & (j < PROJ_MAIN_STEPS))
    def _():
        store_heads(o_ref, _dot(xn_ref[...], wr_ref[...]))

    @pl.when((j >= PROJ_MAIN_STEPS) & (j < last))
    def _():
        og_ref[...] = _sigmoid(_dot(xn_ref[...], wr_ref[...])).astype(og_ref.dtype)

    @pl.when(j == last)
    def _():
        nh = wr_ref.shape[1] // HEAD_DIM
        store_heads(of_ref.at[:nh], _dot(xn_ref[...], wr_ref[...]))
        of_ref[nh] = _dot(xn_ref[...], wfa_ref[...])


def _proj_in(h, g, w_fox, w_rest, *, tm, width, gate_width, q_scale):
    t, d = h.shape
    nh = width // HEAD_DIM
    assert w_fox.shape[1] >= PROJ_FOX_STEPS * width + HEAD_DIM
    assert w_rest.shape[1] == 4 * width + gate_width and gate_width % width == 0
    n_gate = gate_width // width
    last = PROJ_MAIN_STEPS + n_gate

    def rest_col(i, j):
        return (0, jnp.where(j <= 3, 0, jnp.where(j == last, 1, j - 2)))

    def gate_step(j):
        return jnp.clip(j - PROJ_MAIN_STEPS, 0, n_gate - 1)

    return pl.pallas_call(
        functools.partial(_proj_in_kernel, q_scale=q_scale),
        grid=(t // tm, last + 1),
        in_specs=[
            pl.BlockSpec((tm, d), lambda i, j: (i, 0)),
            pl.BlockSpec((1, d), lambda i, j: (0, 0)),
            pl.BlockSpec((d, width), lambda i, j: (0, jnp.minimum(j, PROJ_FOX_STEPS - 1))),
            pl.BlockSpec((d, width), rest_col),
            pl.BlockSpec((d, HEAD_DIM), lambda i, j: (0, PROJ_FOX_STEPS * nh)),
        ],
        out_specs=[
            pl.BlockSpec((nh, tm, HEAD_DIM), lambda i, j: (jnp.minimum(j, PROJ_MAIN_STEPS - 1), i, 0)),
            pl.BlockSpec((tm, width), lambda i, j: (i, gate_step(j))),
            pl.BlockSpec((nh + 1, tm, HEAD_DIM), lambda i, j: (0, i, 0)),
        ],
        out_shape=[jax.ShapeDtypeStruct((PROJ_MAIN_STEPS * nh, t, HEAD_DIM), BF16),
                   jax.ShapeDtypeStruct((t, gate_width), BF16),
                   jax.ShapeDtypeStruct((nh + 1, t, HEAD_DIM), F32)],
        scratch_shapes=[pltpu.VMEM((tm, d), BF16)],
        compiler_params=_params("parallel", "arbitrary"),
        name="proj_in",
    )(h, g.reshape(1, d), w_fox, w_rest, w_fox)


def _fox_gate_kernel(f_ref, b_ref, o_ref, carry_ref, *, ts):
    @pl.when(pl.program_id(1) == 0)
    def _():
        carry_ref[...] = jnp.zeros_like(carry_ref)

    z = f_ref[...].T[:NUM_HEADS, :] + b_ref[...]
    lf = (jnp.minimum(z, 0.0) - jnp.log1p(jnp.exp(-jnp.abs(z)))) * LOG2E
    src = lax.broadcasted_iota(jnp.int32, (ts, ts), 0)
    dst = lax.broadcasted_iota(jnp.int32, (ts, ts), 1)
    tri = jnp.where(src <= dst, 1.0, 0.0).astype(BF16)
    hi, mid, lo = _split3(lf)
    c = _dot(hi, tri) + _dot(mid, tri) + _dot(lo, tri) + carry_ref[:, :1]
    o_ref[...] = c
    carry_ref[...] = jnp.broadcast_to(c[:, ts - 1:ts], carry_ref.shape)


def _fox_gate(f4, slab, bias, *, ts):
    _, b, s, _ = f4.shape
    nh = NUM_HEADS
    return pl.pallas_call(
        functools.partial(_fox_gate_kernel, ts=ts),
        grid=(b, s // ts),
        in_specs=[
            pl.BlockSpec((None, None, ts, HEAD_DIM), lambda i, j: (slab, i, j, 0)),
            pl.BlockSpec((nh, 1), lambda i, j: (0, 0)),
        ],
        out_specs=pl.BlockSpec((None, nh, ts), lambda i, j: (i, 0, j)),
        out_shape=jax.ShapeDtypeStruct((b, nh, s), F32),
        scratch_shapes=[pltpu.VMEM((nh, HEAD_DIM), F32)],
        compiler_params=_params("parallel", "arbitrary"),
        name="fox_gate",
    )(f4, bias.reshape(nh, 1))


def _fox_attn_kernel(q_ref, k_ref, v_ref, c_ref, o_ref, m_ref, l_ref, acc_ref, *, tq):
    nq = q_ref.shape[0] // tq
    reps = tq // HEAD_DIM

    m_ref[...] = jnp.full_like(m_ref, NEG_BIG)
    l_ref[...] = jnp.zeros_like(l_ref)
    acc_ref[...] = jnp.zeros_like(acc_ref)

    def step(qi, ki):
        rows = slice(qi * tq, (qi + 1) * tq)
        cols = slice(ki * tq, (ki + 1) * tq)
        c0 = jnp.max(c_ref[:, rows], axis=-1, keepdims=True)
        s = _dot_nt(q_ref[rows, :], k_ref[cols, :]) + (c0 - c_ref[:, cols])
        if qi == ki:
            row = lax.broadcasted_iota(jnp.int32, (tq, tq), 0)
            col = lax.broadcasted_iota(jnp.int32, (tq, tq), 1)
            s = jnp.where(col <= row, s, NEG_BIG)
        m_prev = m_ref[rows, :]
        m_next = jnp.maximum(m_prev, jnp.max(s, axis=-1, keepdims=True))
        p = jnp.exp2(s - jnp.concatenate([m_next] * reps, axis=1))
        alpha = jnp.exp2(m_prev - m_next)
        l_ref[rows, :] = alpha * l_ref[rows, :] + jnp.sum(p, axis=-1, keepdims=True)
        acc_ref[rows, :] = alpha * acc_ref[rows, :] + _dot(p.astype(BF16), v_ref[cols, :])
        m_ref[rows, :] = m_next

    for diag in range(nq):
        for qi in range(diag, nq):
            step(qi, qi - diag)
    o_ref[...] = (acc_ref[...] / l_ref[...]).astype(o_ref.dtype)


def _fox_attn(main4, c4, passengers, *, tq):
    _, b, s, _ = main4.shape
    nh, dh = NUM_HEADS, HEAD_DIM
    p_in, p_out, p_shapes = _passenger_specs(passengers, b * nh, lambda i, h: i * nh + h)
    return pl.pallas_call(
        _with_passengers(functools.partial(_fox_attn_kernel, tq=tq), 4, 1, len(passengers)),
        grid=(b, nh),
        in_specs=[
            pl.BlockSpec((None, None, s, dh), lambda i, h: (h, i, 0, 0)),
            pl.BlockSpec((None, None, s, dh), lambda i, h: (nh + h, i, 0, 0)),
            pl.BlockSpec((None, None, s, dh), lambda i, h: (2 * nh + h, i, 0, 0)),
            pl.BlockSpec((None, None, 1, s), lambda i, h: (i, h, 0, 0)),
        ] + p_in,
        out_specs=[pl.BlockSpec((None, None, s, dh), lambda i, h: (h, i, 0, 0))] + p_out,
        out_shape=[jax.ShapeDtypeStruct((nh, b, s, dh), BF16)] + p_shapes,
        scratch_shapes=[pltpu.VMEM((s, dh), F32), pltpu.VMEM((s, dh), F32),
                        pltpu.VMEM((s, dh), F32)],
        compiler_params=_params("parallel", "parallel"),
        name="fox_attn",
    )(main4, main4, main4, c4, *passengers)


def _hgrn_kernel(q_ref, f_ref, i_ref, g_ref, lbl_ref, ng_ref, o_ref,
                 st_ref, qe_ref, intra_ref, u_ref, dec_ref, a_ref, v_ref, *, lc):
    cs, sb, blk = HGRN_CHUNK, HGRN_SUB, HGRN_BLOCK
    half = sb // 2
    cpb = blk // cs

    @pl.when(pl.program_id(2) == 0)
    def _():
        st_ref[...] = jnp.zeros_like(st_ref)

    logits = lbl_ref[...]
    e = jnp.exp(logits - jnp.max(logits, axis=0, keepdims=True))
    lb = e[0:1, :] / jnp.sum(e, axis=0, keepdims=True)
    ng = ng_ref[...]

    src = lax.broadcasted_iota(jnp.int32, (blk, blk), 1)
    dst = lax.broadcasted_iota(jnp.int32, (blk, blk), 0)
    same_chunk = (src // cs) == (dst // cs)
    tri = jnp.where(same_chunk & (src <= dst), 1.0, 0.0).astype(BF16)
    half_row = lax.broadcasted_iota(jnp.int32, (half, HEAD_DIM), 0)
    blk_col = lax.broadcasted_iota(jnp.int32, (sb, cs), 1)
    chunk_row = lax.broadcasted_iota(jnp.int32, (cs, cs), 0)
    chunk_col = lax.broadcasted_iota(jnp.int32, (cs, cs), 1)

    def prep(rows):
        q = _silu(q_ref[rows, :].astype(F32))
        f = lb + (1.0 - lb) * _sigmoid(f_ref[rows, :])
        kk = 1.0 - f
        hi, mid, lo = _split3(jnp.log2(f))
        cum = _dot(tri, hi) + _dot(tri, mid) + _dot(tri, lo)
        lasts = [cum[c0 + cs - 1:c0 + cs, :] for c0 in range(0, blk, cs)]
        return q, kk, cum, lasts

    def state_increment(ci, kk, cum, lasts, v):
        c0 = ci * cs
        kdec = (kk[c0:c0 + cs, :] * jnp.exp2(lasts[ci] - cum[c0:c0 + cs, :])).astype(BF16)
        return _dot(v[c0:c0 + cs, :].T.astype(BF16), kdec)

    def finish(rows, outs):
        o = _rms(jnp.concatenate(outs, axis=0), ng) * _silu(g_ref[rows, :].astype(F32))
        o_ref[rows, :] = o.astype(o_ref.dtype)

    def precompute(gi, deepest):
        blocks = [gi * HGRN_GROUP + k for k in range(HGRN_GROUP)]
        rows = [pl.ds(pl.multiple_of(bi * blk, blk), blk) for bi in blocks]
        stage1 = [prep(r) for r in rows]
        stage2 = []
        for bi, r, (q, kk, cum, lasts) in zip(blocks, rows, stage1):
            v = i_ref[r, :]
            v32 = v.astype(F32)
            qe = (q * jnp.exp2(cum)).astype(BF16)
            kinv = (kk * jnp.exp2(-cum)).astype(BF16)
            qe_ref[r, :] = qe
            scores = [_dot_nt(qe[c0:c0 + cs, :], kinv[c0:c0 + cs, :]) for c0 in range(0, blk, cs)]
            for ci in range(cpb):
                u_ref[bi * cpb + ci] = state_increment(ci, kk, cum, lasts, v32)
                dec_ref[pl.ds(pl.multiple_of((bi * cpb + ci) * SUBLANES, SUBLANES), SUBLANES), :] = (
                    jnp.broadcast_to(jnp.exp2(lasts[ci]), (SUBLANES, HEAD_DIM)))
                deepest = jnp.maximum(deepest, -lasts[ci])
            stage2.append((v, scores))
        for r, (v, scores) in zip(rows, stage2):
            intra = [_dot(jnp.where(chunk_col <= chunk_row, sc, 0.0).astype(BF16), v[ci * cs:(ci + 1) * cs, :])
                     for ci, sc in enumerate(scores)]
            intra_ref[r, :] = jnp.concatenate(intra, axis=0)
        return deepest

    deepest = lax.fori_loop(0, lc // (blk * HGRN_GROUP), precompute, jnp.zeros((1, HEAD_DIM), F32))
    depth = jnp.max(deepest)

    @pl.when(depth <= HGRN_SAFE_LOG2)
    def _():
        def recur(bi, carry):
            base = pl.multiple_of(bi * blk, blk)
            st = st_ref[...]
            outs = []
            for ci in range(cpb):
                rows_c = pl.ds(base + ci * cs, cs)
                dec = dec_ref[pl.ds(pl.multiple_of((bi * cpb + ci) * SUBLANES, SUBLANES), SUBLANES), :]
                outs.append(_dot_nt(qe_ref[rows_c, :], st.astype(BF16)) + intra_ref[rows_c, :])
                st = st * dec[0:1, :] + u_ref[bi * cpb + ci]
            st_ref[...] = st
            finish(pl.ds(base, blk), outs)
            return carry

        lax.fori_loop(0, lc // blk, recur, 0, unroll=4)

    @pl.when(jnp.logical_not(depth <= HGRN_SAFE_LOG2))
    def _():
        def general(bi, carry):
            rows = pl.ds(pl.multiple_of(bi * blk, blk), blk)
            q, kk, cum, lasts = prep(rows)
            v = i_ref[rows, :]
            v32 = v.astype(F32)
            a_ref[...] = cum - jnp.log2(kk)
            v_ref[...] = v32
            st = st_ref[...]
            outs = []
            for ci in range(cpb):
                c0 = ci * cs
                cum_c = cum[c0:c0 + cs, :]
                q_c = q[c0:c0 + cs, :]
                kk_c = kk[c0:c0 + cs, :]
                inter = _dot_nt((q_c * jnp.exp2(cum_c)).astype(BF16), st.astype(BF16))
                st = st * jnp.exp2(lasts[ci]) + state_increment(ci, kk, cum, lasts, v32)

                score_rows = [jnp.zeros((sb, cs), F32)]
                for si in range(1, cs // sb):
                    r0 = si * sb
                    ref = cum_c[r0 - 1:r0, :]
                    qt = q_c[r0:r0 + sb, :] * jnp.exp2(cum_c[r0:r0 + sb, :] - ref)
                    kt = kk_c * jnp.exp2(jnp.minimum(ref - cum_c, 0.0))
                    sc = _dot_nt(qt.astype(BF16), kt.astype(BF16))
                    score_rows.append(jnp.where(blk_col < r0, sc, 0.0))
                off = _dot(jnp.concatenate(score_rows, axis=0).astype(BF16), v[c0:c0 + cs, :])

                diag_rows = []
                for si in range(cs // sb):
                    r0 = c0 + si * sb
                    q_lo, q_hi = q[r0:r0 + half, :], q[r0 + half:r0 + sb, :]
                    c_lo, c_hi = cum[r0:r0 + half, :], cum[r0 + half:r0 + sb, :]
                    acc_lo = jnp.zeros((half, HEAD_DIM), F32)
                    acc_hi = jnp.zeros((half, HEAD_DIM), F32)
                    for ti in range(sb):
                        a_s = a_ref[r0 + ti:r0 + ti + 1, :]
                        v_s = v_ref[r0 + ti:r0 + ti + 1, :]
                        if ti < half:
                            d = c_lo - a_s
                            if ti > 0:
                                d = jnp.where(half_row >= ti, d, NEG_BIG)
                            acc_lo = acc_lo + jnp.sum(q_lo * jnp.exp2(d), axis=-1, keepdims=True) * v_s
                            d = c_hi - a_s
                        else:
                            d = c_hi - a_s
                            if ti > half:
                                d = jnp.where(half_row >= ti - half, d, NEG_BIG)
                        acc_hi = acc_hi + jnp.sum(q_hi * jnp.exp2(d), axis=-1, keepdims=True) * v_s
                    diag_rows += [acc_lo, acc_hi]
                outs.append(inter + off + jnp.concatenate(diag_rows, axis=0))
            st_ref[...] = st
            finish(rows, outs)
            return carry

        lax.fori_loop(0, lc // blk, general, 0)


def _hgrn(main4, f4, slabs, lb_logits, norm_g, passengers, *, lc):
    _, b, s, _ = main4.shape
    nh, dh = NUM_HEADS, HEAD_DIM
    nl = lb_logits.shape[0]
    assert lc % (HGRN_BLOCK * HGRN_GROUP) == 0
    q_blk, i_blk, g_blk = slabs

    def col(first):
        return pl.BlockSpec((None, None, lc, dh), lambda i, h, j: (first + h, i, j, 0))

    n_lc = s // lc
    p_in, p_out, p_shapes = _passenger_specs(passengers, b * nh * n_lc,
                                             lambda i, h, j: (i * nh + h) * n_lc + j)
    return pl.pallas_call(
        _with_passengers(functools.partial(_hgrn_kernel, lc=lc), 6, 1, len(passengers)),
        grid=(b, nh, n_lc),
        in_specs=[
            col(q_blk), col(0), col(i_blk), col(g_blk),
            pl.BlockSpec((nl, dh), lambda i, h, j: (0, h)),
            pl.BlockSpec((1, dh), lambda i, h, j: (0, 0)),
        ] + p_in,
        out_specs=[pl.BlockSpec((None, None, lc, dh), lambda i, h, j: (h, i, j, 0))] + p_out,
        out_shape=[jax.ShapeDtypeStruct((nh, b, s, dh), BF16)] + p_shapes,
        scratch_shapes=[
            pltpu.VMEM((dh, dh), F32),
            pltpu.VMEM((lc, dh), BF16),
            pltpu.VMEM((lc, dh), F32),
            pltpu.VMEM((lc // HGRN_CHUNK, dh, dh), F32),
            pltpu.VMEM((lc // HGRN_CHUNK * SUBLANES, dh), F32),
            pltpu.VMEM((HGRN_BLOCK, dh), F32),
            pltpu.VMEM((HGRN_BLOCK, dh), F32),
        ],
        compiler_params=_params("parallel", "parallel", "arbitrary"),
        name="hgrn",
    )(main4, f4, main4, main4, lb_logits, norm_g.reshape(1, dh), *passengers)


def _merge_kernel(h_ref, gpost_ref, ya_ref, yb_ref, ga_ref, gb_ref, wpa_ref, wpb_ref, wo_ref, o_ref):
    j = pl.program_id(1)

    def sweep_step(first, last):
        tn = wo_ref.shape[0]
        groups = [slice(c0, c0 + tn // MERGE_COL_GROUPS) for c0 in range(0, tn, tn // MERGE_COL_GROUPS)]
        y_a = jnp.concatenate([ya_ref[hh] for hh in range(ya_ref.shape[0])], axis=1)
        y_b = jnp.concatenate([yb_ref[hh] for hh in range(yb_ref.shape[0])], axis=1)
        stage1 = [(_dot(y_a, wpa_ref[:, c]), _dot(y_b, wpb_ref[:, c])) for c in groups]
        merged = [(ga_ref[:, c].astype(F32) * p_a + gb_ref[:, c].astype(F32) * p_b).astype(BF16)
                  for c, (p_a, p_b) in zip(groups, stage1)]
        out = _dot(merged[0], wo_ref[groups[0], :])
        for mg, c in zip(merged[1:], groups[1:]):
            out += _dot(mg, wo_ref[c, :])
        if first:
            o_ref[...] = out
        else:
            o_ref[...] += out
        if last:
            _postnorm_residual_to(o_ref, h_ref, o_ref, gpost_ref, 1.0)

    n_steps = pl.num_programs(1)
    pl.when(j == 0)(functools.partial(sweep_step, True, False))
    pl.when((j > 0) & (j < n_steps - 1))(functools.partial(sweep_step, False, False))
    pl.when(j == n_steps - 1)(functools.partial(sweep_step, False, True))


def _merge(h, g_post, y_a, y_b, gates, w_pa, w_pb, w_o, *, tm, tn):
    t, d = h.shape
    nh_a, _, dh_a = y_a.shape
    nh_b, _, dh_b = y_b.shape
    wa, wb = nh_a * dh_a, nh_b * dh_b
    n_j = d // tn

    def residual_rows(i, j):
        return (jnp.where(j == n_j - 1, i, jnp.maximum(i - 1, 0)), 0)

    return pl.pallas_call(
        _merge_kernel,
        grid=(t // tm, n_j),
        in_specs=[
            pl.BlockSpec((tm, d), residual_rows),
            pl.BlockSpec((1, d), lambda i, j: (0, 0)),
            pl.BlockSpec((nh_a, tm, dh_a), lambda i, j: (0, i, 0)),
            pl.BlockSpec((nh_b, tm, dh_b), lambda i, j: (0, i, 0)),
            pl.BlockSpec((tm, tn), lambda i, j: (i, j)),
            pl.BlockSpec((tm, tn), lambda i, j: (i, d // tn + j)),
            pl.BlockSpec((wa, tn), lambda i, j: (0, j)),
            pl.BlockSpec((wb, tn), lambda i, j: (0, j)),
            pl.BlockSpec((tn, d), lambda i, j: (j, 0)),
        ],
        out_specs=pl.BlockSpec((tm, d), lambda i, j: (i, 0)),
        out_shape=jax.ShapeDtypeStruct((t, d), F32),
        compiler_params=_params("parallel", "arbitrary"),
        name="merge",
    )(h, g_post.reshape(1, d), y_a, y_b, gates, gates, w_pa, w_pb, w_o)


def _ple_kernel(h_ref, gpre_ref, gpost_ref, p_ref, wg_ref, wp_ref, o_ref):
    tm = h_ref.shape[0]
    groups = [slice(r0, r0 + tm // PLE_ROW_GROUPS) for r0 in range(0, tm, tm // PLE_ROW_GROUPS)]
    emb = [_dot(p_ref[r, :].astype(BF16), wp_ref[...]) for r in groups]
    gates = [_dot(_rms(h_ref[r, :], gpre_ref[...]).astype(BF16), wg_ref[...]) for r in groups]
    for r, e, g in zip(groups, emb, gates):
        o_ref[r, :] = h_ref[r, :] + _rms(jax.nn.sigmoid(g) * e, gpost_ref[...])


def _ple(h, g_pre, g_post, p2, w_g, w_p, *, tm):
    t, d = h.shape
    dp = p2.shape[1]
    return pl.pallas_call(
        _ple_kernel,
        grid=(t // tm,),
        in_specs=[
            pl.BlockSpec((tm, d), lambda i: (i, 0)),
            pl.BlockSpec((1, d), lambda i: (0, 0)),
            pl.BlockSpec((1, d), lambda i: (0, 0)),
            pl.BlockSpec((tm, dp), lambda i: (i, 0)),
            pl.BlockSpec((d, d), lambda i: (0, 0)),
            pl.BlockSpec((dp, d), lambda i: (0, 0)),
        ],
        out_specs=pl.BlockSpec((tm, d), lambda i: (i, 0)),
        out_shape=jax.ShapeDtypeStruct((t, d), F32),
        compiler_params=_params("parallel"),
        name="ple",
    )(h, g_pre.reshape(1, d), g_post.reshape(1, d), p2, w_g, w_p)


def _tile(n, want):
    t = min(n, want)
    while n % t:
        t //= 2
    return t


def kernel(x, p, ffn1_pre_g, ffn1_post_g, ffn1_w_gate, ffn1_w_up, ffn1_w_down, mix_pre_g, mix_post_g, mix_w_in, fox_f_bias, hgrn_lb_logits, hgrn_norm_g, mix_w_proj_fox, mix_w_proj_hgrn, mix_w_out, ffn2_pre_g, ffn2_post_g, ffn2_w_gate, ffn2_w_up, ffn2_w_down, ple_pre_g, ple_post_g, ple_w_gate, ple_w_proj):
    b, s, d = x.shape
    t = b * s
    depth = ffn1_pre_g.shape[0]
    assert depth == 1, "the HGRN2 lower bound is evaluated for a single layer"
    nh, dh = NUM_HEADS, HEAD_DIM
    width = nh * dh
    assert mix_w_in.shape[-1] == 3 * width + nh + 4 * width + 2 * d

    tm_big = _tile(t, 1024)
    h = x.reshape(t, d)
    for i in range(depth):
        h = _ffn(h, ffn1_pre_g[i], ffn1_post_g[i], ffn1_w_gate[i].astype(BF16),
                 ffn1_w_up[i].astype(BF16), ffn1_w_down[i].astype(BF16),
                 tm=tm_big, tf=_tile(ffn1_w_gate.shape[-1], 512))

        w_in = mix_w_in[i]
        o_f = 3 * width
        o_b = o_f + nh
        w_bf = w_in.astype(BF16)
        w_rest = w_bf[:, o_b:]

        main, gates, f = _proj_in(h, mix_pre_g[i], w_bf, w_rest, tm=tm_big, width=width,
                                  gate_width=2 * d, q_scale=dh ** -0.5 * LOG2E)
        main4 = main.reshape(6 * nh, b, s, dh)
        f4 = f.reshape(nh + 1, b, s, dh)
        c = _fox_gate(f4, nh, fox_f_bias[i], ts=_tile(s, 512))
        y_a, w2_gate, w2_up = _fox_attn(main4, c.reshape(b, nh, 1, s),
                                        [ffn2_w_gate[i], ffn2_w_up[i]], tq=_tile(s, 512))
        y_b, w2_down, w_pa, w_pb, w_o, w_pg = _hgrn(
            main4, f4, (3 * nh, 4 * nh, 5 * nh), hgrn_lb_logits, hgrn_norm_g[i],
            [ffn2_w_down[i], mix_w_proj_fox[i], mix_w_proj_hgrn[i], mix_w_out[i], ple_w_gate[i]],
            lc=_tile(s, 2048))
        h = _merge(h, mix_post_g[i], y_a.reshape(nh, t, dh), y_b.reshape(nh, t, dh), gates,
                   w_pa, w_pb, w_o, tm=tm_big, tn=_tile(d, 512))

        h = _ffn(h, ffn2_pre_g[i], ffn2_post_g[i], w2_gate, w2_up, w2_down,
                 tm=tm_big, tf=_tile(ffn2_w_gate.shape[-1], 512))
        h = _ple(h, ple_pre_g[i], ple_post_g[i], p[i].reshape(t, -1),
                 w_pg, ple_w_proj[i].astype(BF16), tm=_tile(t, 512))
    return h.reshape(b, s, d)
```

```python
import functools
import math

import jax
import jax.numpy as jnp
from jax import lax
from jax.experimental import pallas as pl
from jax.experimental.pallas import tpu as pltpu

NORM_EPS = 1e-6
MACARON_SCALE = 0.5
LOG2E = math.log2(math.e)
HEAD_DIM = 128
NUM_HEADS = 8
HGRN_CHUNK = 64
HGRN_SUB = 16
HGRN_BLOCK = 256
HGRN_GROUP = 4
HGRN_SAFE_LOG2 = 100.0
FFN_COL_GROUPS = 2
MERGE_COL_GROUPS = 2
PLE_ROW_GROUPS = 2
NORM_ROWS = 16
SUBLANES = 8
BF16_ROWS = 16
NEG_BIG = -1e30

VMEM_LIMIT_BYTES = 60 * 1024 * 1024

BF16 = jnp.bfloat16
F32 = jnp.float32


def _params(*semantics):
    return pltpu.CompilerParams(dimension_semantics=semantics,
                                vmem_limit_bytes=VMEM_LIMIT_BYTES)


def _rms(x, g):
    ms = jnp.mean(x * x, axis=-1, keepdims=True)
    return x * lax.rsqrt(ms + NORM_EPS) * g


def _postnorm_residual_to(o_ref, x_ref, y_ref, g_ref, scale):
    g = g_ref[...] if scale == 1.0 else scale * g_ref[...]
    for r0 in range(0, x_ref.shape[0], NORM_ROWS):
        rows = slice(r0, r0 + NORM_ROWS)
        o_ref[rows, :] = x_ref[rows, :] + _rms(y_ref[rows, :], g)


def _sigmoid(x):
    return 0.5 + 0.5 * jnp.tanh(0.5 * x)


def _silu(x):
    h = 0.5 * x
    return h + h * jnp.tanh(h)


def _dot(a, b):
    return jnp.dot(a, b, preferred_element_type=F32)


def _dot_nt(a, b):
    return lax.dot_general(a, b, (((1,), (1,)), ((), ())), preferred_element_type=F32)


def _split3(x):
    hi = x.astype(BF16)
    r = x - hi.astype(F32)
    mid = r.astype(BF16)
    lo = (r - mid.astype(F32)).astype(BF16)
    return hi, mid, lo


def _with_passengers(body, n_in, n_out, n_pass):
    def kernel(*refs):
        ins, refs = refs[:n_in], refs[n_in:]
        pass_in, refs = refs[:n_pass], refs[n_pass:]
        outs, refs = refs[:n_out], refs[n_out:]
        pass_out, scratch = refs[:n_pass], refs[n_pass:]
        for src, dst in zip(pass_in, pass_out):
            dst[...] = src[...].astype(dst.dtype)
        body(*ins, *outs, *scratch)

    return kernel


def _passenger_specs(weights, n_steps, step_of):
    in_specs, out_specs, out_shapes = [], [], []
    for w in weights:
        r, c = w.shape
        share = 1
        while (r * share) % n_steps or (r * share // n_steps) % BF16_ROWS:
            share *= 2
        spec = pl.BlockSpec((r * share // n_steps, c), lambda *g, share=share: (step_of(*g) // share, 0))
        in_specs.append(spec)
        out_specs.append(spec)
        out_shapes.append(jax.ShapeDtypeStruct((r, c), BF16))
    return in_specs, out_specs, out_shapes


def _ffn_kernel(x_ref, gpre_ref, gpost_ref, wg_ref, wu_ref, wd_ref, o_ref, xn_ref):
    j = pl.program_id(1)

    def sweep_step(first, last):
        if first:
            xn_ref[...] = _rms(x_ref[...], gpre_ref[...]).astype(BF16)
        xn = xn_ref[...]
        tf = wg_ref.shape[1]
        groups = [slice(c0, c0 + tf // FFN_COL_GROUPS) for c0 in range(0, tf, tf // FFN_COL_GROUPS)]
        gu = [(_dot(xn, wg_ref[:, c]), _dot(xn, wu_ref[:, c])) for c in groups]
        acts = [(g * jax.nn.sigmoid(g) * u).astype(BF16) for g, u in gu]
        down = _dot(acts[0], wd_ref[groups[0], :])
        for a, c in zip(acts[1:], groups[1:]):
            down += _dot(a, wd_ref[c, :])
        if first:
            o_ref[...] = down
        else:
            o_ref[...] += down
        if last:
            _postnorm_residual_to(o_ref, x_ref, o_ref, gpost_ref, MACARON_SCALE)

    n_steps = pl.num_programs(1)
    pl.when(j == 0)(functools.partial(sweep_step, True, False))
    pl.when((j > 0) & (j < n_steps - 1))(functools.partial(sweep_step, False, False))
    pl.when(j == n_steps - 1)(functools.partial(sweep_step, False, True))


def _ffn(h, g_pre, g_post, w_gate, w_up, w_down, *, tm, tf):
    t, d = h.shape
    f = w_gate.shape[1]
    return pl.pallas_call(
        _ffn_kernel,
        grid=(t // tm, f // tf),
        in_specs=[
            pl.BlockSpec((tm, d), lambda i, j: (i, 0)),
            pl.BlockSpec((1, d), lambda i, j: (0, 0)),
            pl.BlockSpec((1, d), lambda i, j: (0, 0)),
            pl.BlockSpec((d, tf), lambda i, j: (0, j)),
            pl.BlockSpec((d, tf), lambda i, j: (0, j)),
            pl.BlockSpec((tf, d), lambda i, j: (j, 0)),
        ],
        out_specs=pl.BlockSpec((tm, d), lambda i, j: (i, 0)),
        out_shape=jax.ShapeDtypeStruct((t, d), F32),
        scratch_shapes=[pltpu.VMEM((tm, d), BF16)],
        compiler_params=_params("parallel", "arbitrary"),
        name="ffn",
    )(h, g_pre.reshape(1, d), g_post.reshape(1, d), w_gate, w_up, w_down)


PROJ_FOX_STEPS = 3
PROJ_MAIN_STEPS = 6


def _proj_in_kernel(x_ref, g_ref, wa_ref, wr_ref, wfa_ref, o_ref, og_ref, of_ref, xn_ref, *, q_scale):
    j = pl.program_id(1)
    last = pl.num_programs(1) - 1

    def store_heads(dst_ref, y):
        for hh in range(dst_ref.shape[0]):
            dst_ref[hh] = y[:, hh * HEAD_DIM:(hh + 1) * HEAD_DIM].astype(dst_ref.dtype)

    @pl.when(j == 0)
    def _():
        xn_ref[...] = _rms(x_ref[...], g_ref[...]).astype(BF16)
        store_heads(o_ref, _dot(xn_ref[...], wa_ref[...]) * q_scale)

    @pl.when((j > 0) & (j < PROJ_FOX_STEPS))
    def _():
        store_heads(o_ref, _dot(xn_ref[...], wa_ref[...]))

    @pl.when((j >= PROJ_FOX_STEPS) & (j < PROJ_MAIN_STEPS))
    def _():
        store_heads(o_ref, _dot(xn_ref[...], wr_ref[...]))

    @pl.when((j >= PROJ_MAIN_STEPS) & (j < last))
    def _():
        og_ref[...] = _sigmoid(_dot(xn_ref[...], wr_ref[...])).astype(og_ref.dtype)

    @pl.when(j == last)
    def _():
        nh = wr_ref.shape[1] // HEAD_DIM
        store_heads(of_ref.at[:nh], _dot(xn_ref[...], wr_ref[...]))
        of_ref[nh] = _dot(xn_ref[...], wfa_ref[...])


def _proj_in(h, g, w_fox, w_rest, *, tm, width, gate_width, q_scale):
    t, d = h.shape
    nh = width // HEAD_DIM
    assert w_fox.shape[1] >= PROJ_FOX_STEPS * width + HEAD_DIM
    assert w_rest.shape[1] == 4 * width + gate_width and gate_width % width == 0
    n_gate = gate_width // width
    last = PROJ_MAIN_STEPS + n_gate

    def rest_col(i, j):
        return (0, jnp.where(j <= 3, 0, jnp.where(j == last, 1, j - 2)))

    def gate_step(j):
        return jnp.clip(j - PROJ_MAIN_STEPS, 0, n_gate - 1)

    return pl.pallas_call(
        functools.partial(_proj_in_kernel, q_scale=q_scale),
        grid=(t // tm, last + 1),
        in_specs=[
            pl.BlockSpec((tm, d), lambda i, j: (i, 0)),
            pl.BlockSpec((1, d), lambda i, j: (0, 0)),
            pl.BlockSpec((d, width), lambda i, j: (0, jnp.minimum(j, PROJ_FOX_STEPS - 1))),
            pl.BlockSpec((d, width), rest_col),
            pl.BlockSpec((d, HEAD_DIM), lambda i, j: (0, PROJ_FOX_STEPS * nh)),
        ],
        out_specs=[
            pl.BlockSpec((nh, tm, HEAD_DIM), lambda i, j: (jnp.minimum(j, PROJ_MAIN_STEPS - 1), i, 0)),
            pl.BlockSpec((tm, width), lambda i, j: (i, gate_step(j))),
            pl.BlockSpec((nh + 1, tm, HEAD_DIM), lambda i, j: (0, i, 0)),
        ],
        out_shape=[jax.ShapeDtypeStruct((PROJ_MAIN_STEPS * nh, t, HEAD_DIM), BF16),
                   jax.ShapeDtypeStruct((t, gate_width), BF16),
                   jax.ShapeDtypeStruct((nh + 1, t, HEAD_DIM), F32)],
        scratch_shapes=[pltpu.VMEM((tm, d), BF16)],
        compiler_params=_params("parallel", "arbitrary"),
        name="proj_in",
    )(h, g.reshape(1, d), w_fox, w_rest, w_fox)


def _fox_gate_kernel(f_ref, b_ref, o_ref, carry_ref, *, ts):
    @pl.when(pl.program_id(1) == 0)
    def _():
        carry_ref[...] = jnp.zeros_like(carry_ref)

    z = f_ref[...].T[:NUM_HEADS, :] + b_ref[...]
    lf = (jnp.minimum(z, 0.0) - jnp.log1p(jnp.exp(-jnp.abs(z)))) * LOG2E
    src = lax.broadcasted_iota(jnp.int32, (ts, ts), 0)
    dst = lax.broadcasted_iota(jnp.int32, (ts, ts), 1)
    tri = jnp.where(src <= dst, 1.0, 0.0).astype(BF16)
    hi, mid, lo = _split3(lf)
    c = _dot(hi, tri) + _dot(mid, tri) + _dot(lo, tri) + carry_ref[:, :1]
    o_ref[...] = c
    carry_ref[...] = jnp.broadcast_to(c[:, ts - 1:ts], carry_ref.shape)


def _fox_gate(f4, slab, bias, *, ts):
    _, b, s, _ = f4.shape
    nh = NUM_HEADS
    return pl.pallas_call(
        functools.partial(_fox_gate_kernel, ts=ts),
        grid=(b, s // ts),
        in_specs=[
            pl.BlockSpec((None, None, ts, HEAD_DIM), lambda i, j: (slab, i, j, 0)),
            pl.BlockSpec((nh, 1), lambda i, j: (0, 0)),
        ],
        out_specs=pl.BlockSpec((None, nh, ts), lambda i, j: (i, 0, j)),
        out_shape=jax.ShapeDtypeStruct((b, nh, s), F32),
        scratch_shapes=[pltpu.VMEM((nh, HEAD_DIM), F32)],
        compiler_params=_params("parallel", "arbitrary"),
        name="fox_gate",
    )(f4, bias.reshape(nh, 1))


def _fox_attn_kernel(q_ref, k_ref, v_ref, c_ref, o_ref, m_ref, acc_ref, vone_ref, *, tq):
    nq = q_ref.shape[0] // tq
    reps = tq // HEAD_DIM
    dh = v_ref.shape[1]

    vone_ref[:, :dh] = v_ref[...]
    vone_ref[:, dh:] = jnp.ones_like(v_ref)
    m_ref[...] = jnp.full_like(m_ref, NEG_BIG)
    acc_ref[...] = jnp.zeros_like(acc_ref)

    def step(qi, ki):
        rows = slice(qi * tq, (qi + 1) * tq)
        cols = slice(ki * tq, (ki + 1) * tq)
        c0 = jnp.max(c_ref[:, rows], axis=-1, keepdims=True)
        s = _dot_nt(q_ref[rows, :], k_ref[cols, :]) + (c0 - c_ref[:, cols])
        if qi == ki:
            row = lax.broadcasted_iota(jnp.int32, (tq, tq), 0)
            col = lax.broadcasted_iota(jnp.int32, (tq, tq), 1)
            s = jnp.where(col <= row, s, NEG_BIG)
        m_prev = m_ref[rows, :]
        m_next = jnp.maximum(m_prev, jnp.max(s, axis=-1, keepdims=True))
        p = jnp.exp2(s - jnp.concatenate([m_next] * reps, axis=1))
        alpha = jnp.exp2(m_prev - m_next)
        acc_ref[rows, :] = (jnp.concatenate([alpha, alpha], axis=1) * acc_ref[rows, :]
                            + _dot(p.astype(BF16), vone_ref[cols, :]))
        m_ref[rows, :] = m_next

    for diag in range(nq):
        for qi in range(diag, nq):
            step(qi, qi - diag)
    o_ref[...] = (acc_ref[:, :dh] / acc_ref[:, dh:]).astype(o_ref.dtype)


def _fox_attn(main4, c4, passengers, *, tq):
    _, b, s, _ = main4.shape
    nh, dh = NUM_HEADS, HEAD_DIM
    p_in, p_out, p_shapes = _passenger_specs(passengers, b * nh, lambda i, h: i * nh + h)
    return pl.pallas_call(
        _with_passengers(functools.partial(_fox_attn_kernel, tq=tq), 4, 1, len(passengers)),
        grid=(b, nh),
        in_specs=[
            pl.BlockSpec((None, None, s, dh), lambda i, h: (h, i, 0, 0)),
            pl.BlockSpec((None, None, s, dh), lambda i, h: (nh + h, i, 0, 0)),
            pl.BlockSpec((None, None, s, dh), lambda i, h: (2 * nh + h, i, 0, 0)),
            pl.BlockSpec((None, None, 1, s), lambda i, h: (i, h, 0, 0)),
        ] + p_in,
        out_specs=[pl.BlockSpec((None, s, dh), lambda i, h: (i, 0, h))] + p_out,
        out_shape=[jax.ShapeDtypeStruct((b, s, nh * dh), BF16)] + p_shapes,
        scratch_shapes=[pltpu.VMEM((s, dh), F32), pltpu.VMEM((s, 2 * dh), F32),
                        pltpu.VMEM((s, 2 * dh), BF16)],
        compiler_params=_params("parallel", "parallel"),
        name="fox_attn",
    )(main4, main4, main4, c4, *passengers)


def _hgrn_kernel(q_ref, f_ref, i_ref, g_ref, lbl_ref, ng_ref, o_ref,
                 st_ref, qe_ref, intra_ref, u_ref, dec_ref, a_ref, v_ref, *, lc):
    cs, sb, blk = HGRN_CHUNK, HGRN_SUB, HGRN_BLOCK
    half = sb // 2
    cpb = blk // cs

    @pl.when(pl.program_id(2) == 0)
    def _():
        st_ref[...] = jnp.zeros_like(st_ref)

    logits = lbl_ref[...]
    e = jnp.exp(logits - jnp.max(logits, axis=0, keepdims=True))
    lb = e[0:1, :] / jnp.sum(e, axis=0, keepdims=True)
    ng = ng_ref[...]

    src = lax.broadcasted_iota(jnp.int32, (blk, blk), 1)
    dst = lax.broadcasted_iota(jnp.int32, (blk, blk), 0)
    same_chunk = (src // cs) == (dst // cs)
    tri = jnp.where(same_chunk & (src <= dst), 1.0, 0.0).astype(BF16)
    half_row = lax.broadcasted_iota(jnp.int32, (half, HEAD_DIM), 0)
    blk_col = lax.broadcasted_iota(jnp.int32, (sb, cs), 1)
    chunk_row = lax.broadcasted_iota(jnp.int32, (cs, cs), 0)
    chunk_col = lax.broadcasted_iota(jnp.int32, (cs, cs), 1)

    def prep(rows):
        q = _silu(q_ref[rows, :].astype(F32))
        f = lb + (1.0 - lb) * _sigmoid(f_ref[rows, :])
        kk = 1.0 - f
        hi, mid, lo = _split3(jnp.log2(f))
        cum = _dot(tri, hi) + _dot(tri, mid) + _dot(tri, lo)
        lasts = [cum[c0 + cs - 1:c0 + cs, :] for c0 in range(0, blk, cs)]
        return q, kk, cum, lasts

    def state_increment(ci, kk, cum, lasts, v):
        c0 = ci * cs
        kdec = (kk[c0:c0 + cs, :] * jnp.exp2(lasts[ci] - cum[c0:c0 + cs, :])).astype(BF16)
        return _dot(v[c0:c0 + cs, :].T.astype(BF16), kdec)

    def finish(rows, outs):
        o = _rms(jnp.concatenate(outs, axis=0), ng) * _silu(g_ref[rows, :].astype(F32))
        o_ref[rows, :] = o.astype(o_ref.dtype)

    def precompute(gi, deepest):
        blocks = [gi * HGRN_GROUP + k for k in range(HGRN_GROUP)]
        rows = [pl.ds(pl.multiple_of(bi * blk, blk), blk) for bi in blocks]
        stage1 = [prep(r) for r in rows]
        stage2 = []
        for bi, r, (q, kk, cum, lasts) in zip(blocks, rows, stage1):
            v = i_ref[r, :]
            v32 = v.astype(F32)
            qe = (q * jnp.exp2(cum)).astype(BF16)
            kinv = (kk * jnp.exp2(-cum)).astype(BF16)
            qe_ref[r, :] = qe
            scores = [_dot_nt(qe[c0:c0 + cs, :], kinv[c0:c0 + cs, :]) for c0 in range(0, blk, cs)]
            for ci in range(cpb):
                u_ref[bi * cpb + ci] = state_increment(ci, kk, cum, lasts, v32)
                dec_ref[pl.ds(pl.multiple_of((bi * cpb + ci) * SUBLANES, SUBLANES), SUBLANES), :] = (
                    jnp.broadcast_to(jnp.exp2(lasts[ci]), (SUBLANES, HEAD_DIM)))
                deepest = jnp.maximum(deepest, -lasts[ci])
            stage2.append((v, scores))
        for r, (v, scores) in zip(rows, stage2):
            intra = [_dot(jnp.where(chunk_col <= chunk_row, sc, 0.0).astype(BF16), v[ci * cs:(ci + 1) * cs, :])
                     for ci, sc in enumerate(scores)]
            intra_ref[r, :] = jnp.concatenate(intra, axis=0)
        return deepest

    deepest = lax.fori_loop(0, lc // (blk * HGRN_GROUP), precompute, jnp.zeros((1, HEAD_DIM), F32))
    depth = jnp.max(deepest)

    @pl.when(depth <= HGRN_SAFE_LOG2)
    def _():
        def recur(bi, carry):
            base = pl.multiple_of(bi * blk, blk)
            st = st_ref[...]
            outs = []
            for ci in range(cpb):
                rows_c = pl.ds(base + ci * cs, cs)
                dec = dec_ref[pl.ds(pl.multiple_of((bi * cpb + ci) * SUBLANES, SUBLANES), SUBLANES), :]
                outs.append(_dot_nt(qe_ref[rows_c, :], st.astype(BF16)) + intra_ref[rows_c, :])
                st = st * dec[0:1, :] + u_ref[bi * cpb + ci]
            st_ref[...] = st
            finish(pl.ds(base, blk), outs)
            return carry

        lax.fori_loop(0, lc // blk, recur, 0, unroll=4)

    @pl.when(jnp.logical_not(depth <= HGRN_SAFE_LOG2))
    def _():
        def general(bi, carry):
            rows = pl.ds(pl.multiple_of(bi * blk, blk), blk)
            q, kk, cum, lasts = prep(rows)
            v = i_ref[rows, :]
            v32 = v.astype(F32)
            a_ref[...] = cum - jnp.log2(kk)
            v_ref[...] = v32
            st = st_ref[...]
            outs = []
            for ci in range(cpb):
                c0 = ci * cs
                cum_c = cum[c0:c0 + cs, :]
                q_c = q[c0:c0 + cs, :]
                kk_c = kk[c0:c0 + cs, :]
                inter = _dot_nt((q_c * jnp.exp2(cum_c)).astype(BF16), st.astype(BF16))
                st = st * jnp.exp2(lasts[ci]) + state_increment(ci, kk, cum, lasts, v32)

                score_rows = [jnp.zeros((sb, cs), F32)]
                for si in range(1, cs // sb):
                    r0 = si * sb
                    ref = cum_c[r0 - 1:r0, :]
                    qt = q_c[r0:r0 + sb, :] * jnp.exp2(cum_c[r0:r0 + sb, :] - ref)
                    kt = kk_c * jnp.exp2(jnp.minimum(ref - cum_c, 0.0))
                    sc = _dot_nt(qt.astype(BF16), kt.astype(BF16))
                    score_rows.append(jnp.where(blk_col < r0, sc, 0.0))
                off = _dot(jnp.concatenate(score_rows, axis=0).astype(BF16), v[c0:c0 + cs, :])

                diag_rows = []
                for si in range(cs // sb):
                    r0 = c0 + si * sb
                    q_lo, q_hi = q[r0:r0 + half, :], q[r0 + half:r0 + sb, :]
                    c_lo, c_hi = cum[r0:r0 + half, :], cum[r0 + half:r0 + sb, :]
                    acc_lo = jnp.zeros((half, HEAD_DIM), F32)
                    acc_hi = jnp.zeros((half, HEAD_DIM), F32)
                    for ti in range(sb):
                        a_s = a_ref[r0 + ti:r0 + ti + 1, :]
                        v_s = v_ref[r0 + ti:r0 + ti + 1, :]
                        if ti < half:
                            d = c_lo - a_s
                            if ti > 0:
                                d = jnp.where(half_row >= ti, d, NEG_BIG)
                            acc_lo = acc_lo + jnp.sum(q_lo * jnp.exp2(d), axis=-1, keepdims=True) * v_s
                            d = c_hi - a_s
                        else:
                            d = c_hi - a_s
                            if ti > half:
                                d = jnp.where(half_row >= ti - half, d, NEG_BIG)
                        acc_hi = acc_hi + jnp.sum(q_hi * jnp.exp2(d), axis=-1, keepdims=True) * v_s
                    diag_rows += [acc_lo, acc_hi]
                outs.append(inter + off + jnp.concatenate(diag_rows, axis=0))
            st_ref[...] = st
            finish(rows, outs)
            return carry

        lax.fori_loop(0, lc // blk, general, 0)


def _hgrn(main4, f4, slabs, lb_logits, norm_g, passengers, *, lc):
    _, b, s, _ = main4.shape
    nh, dh = NUM_HEADS, HEAD_DIM
    nl = lb_logits.shape[0]
    assert lc % (HGRN_BLOCK * HGRN_GROUP) == 0
    q_blk, i_blk, g_blk = slabs

    def col(first):
        return pl.BlockSpec((None, None, lc, dh), lambda i, h, j: (first + h, i, j, 0))

    n_lc = s // lc
    p_in, p_out, p_shapes = _passenger_specs(passengers, b * nh * n_lc,
                                             lambda i, h, j: (i * nh + h) * n_lc + j)
    return pl.pallas_call(
        _with_passengers(functools.partial(_hgrn_kernel, lc=lc), 6, 1, len(passengers)),
        grid=(b, nh, n_lc),
        in_specs=[
            col(q_blk), col(0), col(i_blk), col(g_blk),
            pl.BlockSpec((nl, dh), lambda i, h, j: (0, h)),
            pl.BlockSpec((1, dh), lambda i, h, j: (0, 0)),
        ] + p_in,
        out_specs=[pl.BlockSpec((None, lc, dh), lambda i, h, j: (i, j, h))] + p_out,
        out_shape=[jax.ShapeDtypeStruct((b, s, nh * dh), BF16)] + p_shapes,
        scratch_shapes=[
            pltpu.VMEM((dh, dh), F32),
            pltpu.VMEM((lc, dh), BF16),
            pltpu.VMEM((lc, dh), F32),
            pltpu.VMEM((lc // HGRN_CHUNK, dh, dh), F32),
            pltpu.VMEM((lc // HGRN_CHUNK * SUBLANES, dh), F32),
            pltpu.VMEM((HGRN_BLOCK, dh), F32),
            pltpu.VMEM((HGRN_BLOCK, dh), F32),
        ],
        compiler_params=_params("parallel", "parallel", "arbitrary"),
        name="hgrn",
    )(main4, f4, main4, main4, lb_logits, norm_g.reshape(1, dh), *passengers)


def _merge_kernel(h_ref, gpost_ref, ya_ref, yb_ref, ga_ref, gb_ref, wpa_ref, wpb_ref, wo_ref, o_ref):
    j = pl.program_id(1)

    def sweep_step(first, last):
        tn = wo_ref.shape[0]
        groups = [slice(c0, c0 + tn // MERGE_COL_GROUPS) for c0 in range(0, tn, tn // MERGE_COL_GROUPS)]
        stage1 = [(_dot(ya_ref[...], wpa_ref[:, c]), _dot(yb_ref[...], wpb_ref[:, c])) for c in groups]
        merged = [(ga_ref[:, c].astype(F32) * y_a + gb_ref[:, c].astype(F32) * y_b).astype(BF16)
                  for c, (y_a, y_b) in zip(groups, stage1)]
        out = _dot(merged[0], wo_ref[groups[0], :])
        for mg, c in zip(merged[1:], groups[1:]):
            out += _dot(mg, wo_ref[c, :])
        if first:
            o_ref[...] = out
        else:
            o_ref[...] += out
        if last:
            _postnorm_residual_to(o_ref, h_ref, o_ref, gpost_ref, 1.0)

    n_steps = pl.num_programs(1)
    pl.when(j == 0)(functools.partial(sweep_step, True, False))
    pl.when((j > 0) & (j < n_steps - 1))(functools.partial(sweep_step, False, False))
    pl.when(j == n_steps - 1)(functools.partial(sweep_step, False, True))


def _merge(h, g_post, y_a, y_b, gates, w_pa, w_pb, w_o, *, tm, tn):
    t, d = h.shape
    wa = y_a.shape[1]
    wb = y_b.shape[1]
    n_j = d // tn

    def residual_rows(i, j):
        return (jnp.where(j == n_j - 1, i, jnp.maximum(i - 1, 0)), 0)

    return pl.pallas_call(
        _merge_kernel,
        grid=(t // tm, n_j),
        in_specs=[
            pl.BlockSpec((tm, d), residual_rows),
            pl.BlockSpec((1, d), lambda i, j: (0, 0)),
            pl.BlockSpec((tm, wa), lambda i, j: (i, 0)),
            pl.BlockSpec((tm, wb), lambda i, j: (i, 0)),
            pl.BlockSpec((tm, tn), lambda i, j: (i, j)),
            pl.BlockSpec((tm, tn), lambda i, j: (i, d // tn + j)),
            pl.BlockSpec((wa, tn), lambda i, j: (0, j)),
            pl.BlockSpec((wb, tn), lambda i, j: (0, j)),
            pl.BlockSpec((tn, d), lambda i, j: (j, 0)),
        ],
        out_specs=pl.BlockSpec((tm, d), lambda i, j: (i, 0)),
        out_shape=jax.ShapeDtypeStruct((t, d), F32),
        compiler_params=_params("parallel", "arbitrary"),
        name="merge",
    )(h, g_post.reshape(1, d), y_a, y_b, gates, gates, w_pa, w_pb, w_o)


def _ple_kernel(h_ref, gpre_ref, gpost_ref, p_ref, wg_ref, wp_ref, o_ref):
    tm = h_ref.shape[0]
    groups = [slice(r0, r0 + tm // PLE_ROW_GROUPS) for r0 in range(0, tm, tm // PLE_ROW_GROUPS)]
    emb = [_dot(p_ref[r, :].astype(BF16), wp_ref[...]) for r in groups]
    gates = [_dot(_rms(h_ref[r, :], gpre_ref[...]).astype(BF16), wg_ref[...]) for r in groups]
    for r, e, g in zip(groups, emb, gates):
        o_ref[r, :] = h_ref[r, :] + _rms(jax.nn.sigmoid(g) * e, gpost_ref[...])


def _ple(h, g_pre, g_post, p2, w_g, w_p, *, tm):
    t, d = h.shape
    dp = p2.shape[1]
    return pl.pallas_call(
        _ple_kernel,
        grid=(t // tm,),
        in_specs=[
            pl.BlockSpec((tm, d), lambda i: (i, 0)),
            pl.BlockSpec((1, d), lambda i: (0, 0)),
            pl.BlockSpec((1, d), lambda i: (0, 0)),
            pl.BlockSpec((tm, dp), lambda i: (i, 0)),
            pl.BlockSpec((d, d), lambda i: (0, 0)),
            pl.BlockSpec((dp, d), lambda i: (0, 0)),
        ],
        out_specs=pl.BlockSpec((tm, d), lambda i: (i, 0)),
        out_shape=jax.ShapeDtypeStruct((t, d), F32),
        compiler_params=_params("parallel"),
        name="ple",
    )(h, g_pre.reshape(1, d), g_post.reshape(1, d), p2, w_g, w_p)


def _tile(n, want):
    t = min(n, want)
    while n % t:
        t //= 2
    return t


def kernel(x, p, ffn1_pre_g, ffn1_post_g, ffn1_w_gate, ffn1_w_up, ffn1_w_down, mix_pre_g, mix_post_g, mix_w_in, fox_f_bias, hgrn_lb_logits, hgrn_norm_g, mix_w_proj_fox, mix_w_proj_hgrn, mix_w_out, ffn2_pre_g, ffn2_post_g, ffn2_w_gate, ffn2_w_up, ffn2_w_down, ple_pre_g, ple_post_g, ple_w_gate, ple_w_proj):
    b, s, d = x.shape
    t = b * s
    depth = ffn1_pre_g.shape[0]
    assert depth == 1, "the HGRN2 lower bound is evaluated for a single layer"
    nh, dh = NUM_HEADS, HEAD_DIM
    width = nh * dh
    assert mix_w_in.shape[-1] == 3 * width + nh + 4 * width + 2 * d

    tm_big = _tile(t, 1024)
    h = x.reshape(t, d)
    for i in range(depth):
        h = _ffn(h, ffn1_pre_g[i], ffn1_post_g[i], ffn1_w_gate[i].astype(BF16),
                 ffn1_w_up[i].astype(BF16), ffn1_w_down[i].astype(BF16),
                 tm=tm_big, tf=_tile(ffn1_w_gate.shape[-1], 512))

        w_in = mix_w_in[i]
        o_f = 3 * width
        o_b = o_f + nh
        w_bf = w_in.astype(BF16)
        w_rest = w_bf[:, o_b:]

        main, gates, f = _proj_in(h, mix_pre_g[i], w_bf, w_rest, tm=tm_big, width=width,
                                  gate_width=2 * d, q_scale=dh ** -0.5 * LOG2E)
        main4 = main.reshape(6 * nh, b, s, dh)
        f4 = f.reshape(nh + 1, b, s, dh)
        c = _fox_gate(f4, nh, fox_f_bias[i], ts=_tile(s, 512))
        y_a, w2_gate, w2_up = _fox_attn(main4, c.reshape(b, nh, 1, s),
                                        [ffn2_w_gate[i], ffn2_w_up[i]], tq=_tile(s, 512))
        y_b, w2_down, w_pa, w_pb, w_o, w_pg = _hgrn(
            main4, f4, (3 * nh, 4 * nh, 5 * nh), hgrn_lb_logits, hgrn_norm_g[i],
            [ffn2_w_down[i], mix_w_proj_fox[i], mix_w_proj_hgrn[i], mix_w_out[i], ple_w_gate[i]],
            lc=_tile(s, 2048))
        h = _merge(h, mix_post_g[i], y_a.reshape(t, width), y_b.reshape(t, width), gates,
                   w_pa, w_pb, w_o, tm=tm_big, tn=_tile(d, 512))

        h = _ffn(h, ffn2_pre_g[i], ffn2_post_g[i], w2_gate, w2_up, w2_down,
                 tm=tm_big, tf=_tile(ffn2_w_gate.shape[-1], 512))
        h = _ple(h, ple_pre_g[i], ple_post_g[i], p[i].reshape(t, -1),
                 w_pg, ple_w_proj[i].astype(BF16), tm=_tile(t, 512))
    return h.reshape(b, s, d)
```

```python
import functools
import math

import jax
import jax.numpy as jnp
from jax import lax
from jax.experimental import pallas as pl
from jax.experimental.pallas import tpu as pltpu

NORM_EPS = 1e-6
MACARON_SCALE = 0.5
LOG2E = math.log2(math.e)
HEAD_DIM = 128
NUM_HEADS = 8
HGRN_CHUNK = 64
HGRN_SUB = 16
HGRN_BLOCK = 256
HGRN_GROUP = 8
HGRN_SAFE_LOG2 = 100.0
FFN_COL_GROUPS = 2
MERGE_COL_GROUPS = 2
PLE_ROW_GROUPS = 2
NORM_ROWS = 16
SUBLANES = 8
BF16_ROWS = 16
NEG_BIG = -1e30

VMEM_LIMIT_BYTES = 60 * 1024 * 1024

BF16 = jnp.bfloat16
F32 = jnp.float32


def _params(*semantics):
    return pltpu.CompilerParams(dimension_semantics=semantics,
                                vmem_limit_bytes=VMEM_LIMIT_BYTES)


def _rms(x, g):
    ms = jnp.mean(x * x, axis=-1, keepdims=True)
    return x * lax.rsqrt(ms + NORM_EPS) * g


def _postnorm_residual_to(o_ref, x_ref, y_ref, g_ref, scale):
    g = g_ref[...] if scale == 1.0 else scale * g_ref[...]
    for r0 in range(0, x_ref.shape[0], NORM_ROWS):
        rows = slice(r0, r0 + NORM_ROWS)
        o_ref[rows, :] = x_ref[rows, :] + _rms(y_ref[rows, :], g)


def _sigmoid(x):
    return 0.5 + 0.5 * jnp.tanh(0.5 * x)


def _silu(x):
    h = 0.5 * x
    return h + h * jnp.tanh(h)


def _dot(a, b):
    return jnp.dot(a, b, preferred_element_type=F32)


def _dot_nt(a, b):
    return lax.dot_general(a, b, (((1,), (1,)), ((), ())), preferred_element_type=F32)


def _split3(x):
    hi = x.astype(BF16)
    r = x - hi.astype(F32)
    mid = r.astype(BF16)
    lo = (r - mid.astype(F32)).astype(BF16)
    return hi, mid, lo


def _with_passengers(body, n_in, n_out, n_pass):
    def kernel(*refs):
        ins, refs = refs[:n_in], refs[n_in:]
        pass_in, refs = refs[:n_pass], refs[n_pass:]
        outs, refs = refs[:n_out], refs[n_out:]
        pass_out, scratch = refs[:n_pass], refs[n_pass:]
        for src, dst in zip(pass_in, pass_out):
            dst[...] = src[...].astype(dst.dtype)
        body(*ins, *outs, *scratch)

    return kernel


def _passenger_specs(weights, n_steps, step_of):
    in_specs, out_specs, out_shapes = [], [], []
    for w in weights:
        r, c = w.shape
        share = 1
        while (r * share) % n_steps or (r * share // n_steps) % BF16_ROWS:
            share *= 2
        spec = pl.BlockSpec((r * share // n_steps, c), lambda *g, share=share: (step_of(*g) // share, 0))
        in_specs.append(spec)
        out_specs.append(spec)
        out_shapes.append(jax.ShapeDtypeStruct((r, c), BF16))
    return in_specs, out_specs, out_shapes


def _ffn_kernel(x_ref, gpre_ref, gpost_ref, wg_ref, wu_ref, wd_ref, o_ref, xn_ref):
    j = pl.program_id(1)

    def sweep_step(first, last):
        if first:
            xn_ref[...] = _rms(x_ref[...], gpre_ref[...]).astype(BF16)
        xn = xn_ref[...]
        tf = wg_ref.shape[1]
        groups = [slice(c0, c0 + tf // FFN_COL_GROUPS) for c0 in range(0, tf, tf // FFN_COL_GROUPS)]
        gu = [(_dot(xn, wg_ref[:, c]), _dot(xn, wu_ref[:, c])) for c in groups]
        acts = [(g * jax.nn.sigmoid(g) * u).astype(BF16) for g, u in gu]
        down = _dot(acts[0], wd_ref[groups[0], :])
        for a, c in zip(acts[1:], groups[1:]):
            down += _dot(a, wd_ref[c, :])
        if first:
            o_ref[...] = down
        else:
            o_ref[...] += down
        if last:
            _postnorm_residual_to(o_ref, x_ref, o_ref, gpost_ref, MACARON_SCALE)

    n_steps = pl.num_programs(1)
    pl.when(j == 0)(functools.partial(sweep_step, True, False))
    pl.when((j > 0) & (j < n_steps - 1))(functools.partial(sweep_step, False, False))
    pl.when(j == n_steps - 1)(functools.partial(sweep_step, False, True))


def _ffn(h, g_pre, g_post, w_gate, w_up, w_down, *, tm, tf):
    t, d = h.shape
    f = w_gate.shape[1]
    return pl.pallas_call(
        _ffn_kernel,
        grid=(t // tm, f // tf),
        in_specs=[
            pl.BlockSpec((tm, d), lambda i, j: (i, 0)),
            pl.BlockSpec((1, d), lambda i, j: (0, 0)),
            pl.BlockSpec((1, d), lambda i, j: (0, 0)),
            pl.BlockSpec((d, tf), lambda i, j: (0, j)),
            pl.BlockSpec((d, tf), lambda i, j: (0, j)),
            pl.BlockSpec((tf, d), lambda i, j: (j, 0)),
        ],
        out_specs=pl.BlockSpec((tm, d), lambda i, j: (i, 0)),
        out_shape=jax.ShapeDtypeStruct((t, d), F32),
        scratch_shapes=[pltpu.VMEM((tm, d), BF16)],
        compiler_params=_params("parallel", "arbitrary"),
        name="ffn",
    )(h, g_pre.reshape(1, d), g_post.reshape(1, d), w_gate, w_up, w_down)


PROJ_FOX_STEPS = 3
PROJ_MAIN_STEPS = 6


def _proj_in_kernel(x_ref, g_ref, wa_ref, wr_ref, wfa_ref, o_ref, og_ref, of_ref, xn_ref, *, q_scale):
    j = pl.program_id(1)
    last = pl.num_programs(1) - 1

    def store_heads(dst_ref, y):
        for hh in range(dst_ref.shape[0]):
            dst_ref[hh] = y[:, hh * HEAD_DIM:(hh + 1) * HEAD_DIM].astype(dst_ref.dtype)

    @pl.when(j == 0)
    def _():
        xn_ref[...] = _rms(x_ref[...], g_ref[...]).astype(BF16)
        store_heads(o_ref, _dot(xn_ref[...], wa_ref[...]) * q_scale)

    @pl.when((j > 0) & (j < PROJ_FOX_STEPS))
    def _():
        store_heads(o_ref, _dot(xn_ref[...], wa_ref[...]))

    @pl.when((j >= PROJ_FOX_STEPS) & (j < PROJ_MAIN_STEPS))
    def _():
        store_heads(o_ref, _dot(xn_ref[...], wr_ref[...]))

    @pl.when((j >= PROJ_MAIN_STEPS) & (j < last))
    def _():
        og_ref[...] = _sigmoid(_dot(xn_ref[...], wr_ref[...])).astype(og_ref.dtype)

    @pl.when(j == last)
    def _():
        nh = wr_ref.shape[1] // HEAD_DIM
        store_heads(of_ref.at[:nh], _dot(xn_ref[...], wr_ref[...]))
        of_ref[nh] = _dot(xn_ref[...], wfa_ref[...])


def _proj_in(h, g, w_fox, w_rest, *, tm, width, gate_width, q_scale):
    t, d = h.shape
    nh = width // HEAD_DIM
    assert w_fox.shape[1] >= PROJ_FOX_STEPS * width + HEAD_DIM
    assert w_rest.shape[1] == 4 * width + gate_width and gate_width % width == 0
    n_gate = gate_width // width
    last = PROJ_MAIN_STEPS + n_gate

    def rest_col(i, j):
        return (0, jnp.where(j <= 3, 0, jnp.where(j == last, 1, j - 2)))

    def gate_step(j):
        return jnp.clip(j - PROJ_MAIN_STEPS, 0, n_gate - 1)

    return pl.pallas_call(
        functools.partial(_proj_in_kernel, q_scale=q_scale),
        grid=(t // tm, last + 1),
        in_specs=[
            pl.BlockSpec((tm, d), lambda i, j: (i, 0)),
            pl.BlockSpec((1, d), lambda i, j: (0, 0)),
            pl.BlockSpec((d, width), lambda i, j: (0, jnp.minimum(j, PROJ_FOX_STEPS - 1))),
            pl.BlockSpec((d, width), rest_col),
            pl.BlockSpec((d, HEAD_DIM), lambda i, j: (0, PROJ_FOX_STEPS * nh)),
        ],
        out_specs=[
            pl.BlockSpec((nh, tm, HEAD_DIM), lambda i, j: (jnp.minimum(j, PROJ_MAIN_STEPS - 1), i, 0)),
            pl.BlockSpec((tm, width), lambda i, j: (i, gate_step(j))),
            pl.BlockSpec((nh + 1, tm, HEAD_DIM), lambda i, j: (0, i, 0)),
        ],
        out_shape=[jax.ShapeDtypeStruct((PROJ_MAIN_STEPS * nh, t, HEAD_DIM), BF16),
                   jax.ShapeDtypeStruct((t, gate_width), BF16),
                   jax.ShapeDtypeStruct((nh + 1, t, HEAD_DIM), F32)],
        scratch_shapes=[pltpu.VMEM((tm, d), BF16)],
        compiler_params=_params("parallel", "arbitrary"),
        name="proj_in",
    )(h, g.reshape(1, d), w_fox, w_rest, w_fox)


def _fox_gate_kernel(f_ref, b_ref, o_ref, carry_ref, *, ts):
    @pl.when(pl.program_id(1) == 0)
    def _():
        carry_ref[...] = jnp.zeros_like(carry_ref)

    z = f_ref[...].T[:NUM_HEADS, :] + b_ref[...]
    lf = (jnp.minimum(z, 0.0) - jnp.log1p(jnp.exp(-jnp.abs(z)))) * LOG2E
    src = lax.broadcasted_iota(jnp.int32, (ts, ts), 0)
    dst = lax.broadcasted_iota(jnp.int32, (ts, ts), 1)
    tri = jnp.where(src <= dst, 1.0, 0.0).astype(BF16)
    hi, mid, lo = _split3(lf)
    c = _dot(hi, tri) + _dot(mid, tri) + _dot(lo, tri) + carry_ref[:, :1]
    o_ref[...] = c
    carry_ref[...] = jnp.broadcast_to(c[:, ts - 1:ts], carry_ref.shape)


def _fox_gate(f4, slab, bias, *, ts):
    _, b, s, _ = f4.shape
    nh = NUM_HEADS
    return pl.pallas_call(
        functools.partial(_fox_gate_kernel, ts=ts),
        grid=(b, s // ts),
        in_specs=[
            pl.BlockSpec((None, None, ts, HEAD_DIM), lambda i, j: (slab, i, j, 0)),
            pl.BlockSpec((nh, 1), lambda i, j: (0, 0)),
        ],
        out_specs=pl.BlockSpec((None, nh, ts), lambda i, j: (i, 0, j)),
        out_shape=jax.ShapeDtypeStruct((b, nh, s), F32),
        scratch_shapes=[pltpu.VMEM((nh, HEAD_DIM), F32)],
        compiler_params=_params("parallel", "arbitrary"),
        name="fox_gate",
    )(f4, bias.reshape(nh, 1))


def _fox_attn_kernel(q_ref, k_ref, v_ref, c_ref, o_ref, m_ref, acc_ref, vone_ref, *, tq):
    nq = q_ref.shape[0] // tq
    reps = tq // HEAD_DIM
    dh = v_ref.shape[1]

    vone_ref[:, :dh] = v_ref[...]
    vone_ref[:, dh:] = jnp.ones_like(v_ref)
    m_ref[...] = jnp.full_like(m_ref, NEG_BIG)
    acc_ref[...] = jnp.zeros_like(acc_ref)

    def step(qi, ki):
        rows = slice(qi * tq, (qi + 1) * tq)
        cols = slice(ki * tq, (ki + 1) * tq)
        c0 = jnp.max(c_ref[:, rows], axis=-1, keepdims=True)
        s = _dot_nt(q_ref[rows, :], k_ref[cols, :]) + (c0 - c_ref[:, cols])
        if qi == ki:
            row = lax.broadcasted_iota(jnp.int32, (tq, tq), 0)
            col = lax.broadcasted_iota(jnp.int32, (tq, tq), 1)
            s = jnp.where(col <= row, s, NEG_BIG)
        m_prev = m_ref[rows, :]
        m_next = jnp.maximum(m_prev, jnp.max(s, axis=-1, keepdims=True))
        p = jnp.exp2(s - jnp.concatenate([m_next] * reps, axis=1))
        alpha = jnp.exp2(m_prev - m_next)
        acc_ref[rows, :] = (jnp.concatenate([alpha, alpha], axis=1) * acc_ref[rows, :]
                            + _dot(p.astype(BF16), vone_ref[cols, :]))
        m_ref[rows, :] = m_next

    for diag in range(nq):
        for qi in range(diag, nq):
            step(qi, qi - diag)
    o_ref[...] = (acc_ref[:, :dh] / acc_ref[:, dh:]).astype(o_ref.dtype)


def _fox_attn(main4, c4, passengers, *, tq):
    _, b, s, _ = main4.shape
    nh, dh = NUM_HEADS, HEAD_DIM
    p_in, p_out, p_shapes = _passenger_specs(passengers, b * nh, lambda i, h: i * nh + h)
    return pl.pallas_call(
        _with_passengers(functools.partial(_fox_attn_kernel, tq=tq), 4, 1, len(passengers)),
        grid=(b, nh),
        in_specs=[
            pl.BlockSpec((None, None, s, dh), lambda i, h: (h, i, 0, 0)),
            pl.BlockSpec((None, None, s, dh), lambda i, h: (nh + h, i, 0, 0)),
            pl.BlockSpec((None, None, s, dh), lambda i, h: (2 * nh + h, i, 0, 0)),
            pl.BlockSpec((None, None, 1, s), lambda i, h: (i, h, 0, 0)),
        ] + p_in,
        out_specs=[pl.BlockSpec((None, s, dh), lambda i, h: (i, 0, h))] + p_out,
        out_shape=[jax.ShapeDtypeStruct((b, s, nh * dh), BF16)] + p_shapes,
        scratch_shapes=[pltpu.VMEM((s, dh), F32), pltpu.VMEM((s, 2 * dh), F32),
                        pltpu.VMEM((s, 2 * dh), BF16)],
        compiler_params=_params("parallel", "parallel"),
        name="fox_attn",
    )(main4, main4, main4, c4, *passengers)


def _hgrn_kernel(q_ref, f_ref, i_ref, g_ref, lbl_ref, ng_ref, o_ref,
                 st_ref, qe_ref, intra_ref, u_ref, dec_ref, a_ref, v_ref, *, lc):
    cs, sb, blk = HGRN_CHUNK, HGRN_SUB, HGRN_BLOCK
    half = sb // 2
    cpb = blk // cs

    @pl.when(pl.program_id(2) == 0)
    def _():
        st_ref[...] = jnp.zeros_like(st_ref)

    logits = lbl_ref[...]
    e = jnp.exp(logits - jnp.max(logits, axis=0, keepdims=True))
    lb = e[0:1, :] / jnp.sum(e, axis=0, keepdims=True)
    ng = ng_ref[...]

    src = lax.broadcasted_iota(jnp.int32, (blk, blk), 1)
    dst = lax.broadcasted_iota(jnp.int32, (blk, blk), 0)
    same_chunk = (src // cs) == (dst // cs)
    tri = jnp.where(same_chunk & (src <= dst), 1.0, 0.0).astype(BF16)
    half_row = lax.broadcasted_iota(jnp.int32, (half, HEAD_DIM), 0)
    blk_col = lax.broadcasted_iota(jnp.int32, (sb, cs), 1)
    chunk_row = lax.broadcasted_iota(jnp.int32, (cs, cs), 0)
    chunk_col = lax.broadcasted_iota(jnp.int32, (cs, cs), 1)

    def prep(rows):
        q = _silu(q_ref[rows, :].astype(F32))
        f = lb + (1.0 - lb) * _sigmoid(f_ref[rows, :])
        kk = 1.0 - f
        hi, mid, lo = _split3(jnp.log2(f))
        cum = _dot(tri, hi) + _dot(tri, mid) + _dot(tri, lo)
        lasts = [cum[c0 + cs - 1:c0 + cs, :] for c0 in range(0, blk, cs)]
        return q, kk, cum, lasts

    def state_increment(ci, kk, cum, lasts, v):
        c0 = ci * cs
        kdec = (kk[c0:c0 + cs, :] * jnp.exp2(lasts[ci] - cum[c0:c0 + cs, :])).astype(BF16)
        return _dot(v[c0:c0 + cs, :].T.astype(BF16), kdec)

    def finish(rows, outs):
        o = _rms(jnp.concatenate(outs, axis=0), ng) * _silu(g_ref[rows, :].astype(F32))
        o_ref[rows, :] = o.astype(o_ref.dtype)

    group = min(HGRN_GROUP, lc // blk)

    def precompute(gi, deepest):
        blocks = [gi * group + k for k in range(group)]
        rows = [pl.ds(pl.multiple_of(bi * blk, blk), blk) for bi in blocks]
        stage1 = [prep(r) for r in rows]
        stage2 = []
        for bi, r, (q, kk, cum, lasts) in zip(blocks, rows, stage1):
            v = i_ref[r, :]
            v32 = v.astype(F32)
            qe = (q * jnp.exp2(cum)).astype(BF16)
            kinv32 = kk * jnp.exp2(-cum)
            kinv = kinv32.astype(BF16)
            qe_ref[r, :] = qe
            scores = [_dot_nt(qe[c0:c0 + cs, :], kinv[c0:c0 + cs, :]) for c0 in range(0, blk, cs)]
            for ci in range(cpb):
                c0 = ci * cs
                dec = jnp.exp2(lasts[ci])
                kdec = (kinv32[c0:c0 + cs, :] * dec).astype(BF16)
                u_ref[bi * cpb + ci] = _dot(v32[c0:c0 + cs, :].T.astype(BF16), kdec)
                dec_ref[pl.ds(pl.multiple_of((bi * cpb + ci) * SUBLANES, SUBLANES), SUBLANES), :] = (
                    jnp.broadcast_to(dec, (SUBLANES, HEAD_DIM)))
                deepest = jnp.maximum(deepest, -lasts[ci])
            stage2.append((v, scores))
        for r, (v, scores) in zip(rows, stage2):
            intra = [_dot(jnp.where(chunk_col <= chunk_row, sc, 0.0).astype(BF16), v[ci * cs:(ci + 1) * cs, :])
                     for ci, sc in enumerate(scores)]
            intra_ref[r, :] = jnp.concatenate(intra, axis=0)
        return deepest

    deepest = lax.fori_loop(0, lc // (blk * group), precompute, jnp.zeros((1, HEAD_DIM), F32))
    depth = jnp.max(deepest)

    @pl.when(depth <= HGRN_SAFE_LOG2)
    def _():
        def recur(bi, carry):
            base = pl.multiple_of(bi * blk, blk)
            st = st_ref[...]
            outs = []
            for ci in range(cpb):
                rows_c = pl.ds(base + ci * cs, cs)
                dec = dec_ref[pl.ds(pl.multiple_of((bi * cpb + ci) * SUBLANES, SUBLANES), SUBLANES), :]
                outs.append(_dot_nt(qe_ref[rows_c, :], st.astype(BF16)) + intra_ref[rows_c, :])
                st = st * dec[0:1, :] + u_ref[bi * cpb + ci]
            st_ref[...] = st
            finish(pl.ds(base, blk), outs)
            return carry

        lax.fori_loop(0, lc // blk, recur, 0, unroll=True)

    @pl.when(jnp.logical_not(depth <= HGRN_SAFE_LOG2))
    def _():
        def general(bi, carry):
            rows = pl.ds(pl.multiple_of(bi * blk, blk), blk)
            q, kk, cum, lasts = prep(rows)
            v = i_ref[rows, :]
            v32 = v.astype(F32)
            a_ref[...] = cum - jnp.log2(kk)
            v_ref[...] = v32
            st = st_ref[...]
            outs = []
            for ci in range(cpb):
                c0 = ci * cs
                cum_c = cum[c0:c0 + cs, :]
                q_c = q[c0:c0 + cs, :]
                kk_c = kk[c0:c0 + cs, :]
                inter = _dot_nt((q_c * jnp.exp2(cum_c)).astype(BF16), st.astype(BF16))
                st = st * jnp.exp2(lasts[ci]) + state_increment(ci, kk, cum, lasts, v32)

                score_rows = [jnp.zeros((sb, cs), F32)]
                for si in range(1, cs // sb):
                    r0 = si * sb
                    ref = cum_c[r0 - 1:r0, :]
                    qt = q_c[r0:r0 + sb, :] * jnp.exp2(cum_c[r0:r0 + sb, :] - ref)
                    kt = kk_c * jnp.exp2(jnp.minimum(ref - cum_c, 0.0))
                    sc = _dot_nt(qt.astype(BF16), kt.astype(BF16))
                    score_rows.append(jnp.where(blk_col < r0, sc, 0.0))
                off = _dot(jnp.concatenate(score_rows, axis=0).astype(BF16), v[c0:c0 + cs, :])

                diag_rows = []
                for si in range(cs // sb):
                    r0 = c0 + si * sb
                    q_lo, q_hi = q[r0:r0 + half, :], q[r0 + half:r0 + sb, :]
                    c_lo, c_hi = cum[r0:r0 + half, :], cum[r0 + half:r0 + sb, :]
                    acc_lo = jnp.zeros((half, HEAD_DIM), F32)
                    acc_hi = jnp.zeros((half, HEAD_DIM), F32)
                    for ti in range(sb):
                        a_s = a_ref[r0 + ti:r0 + ti + 1, :]
                        v_s = v_ref[r0 + ti:r0 + ti + 1, :]
                        if ti < half:
                            d = c_lo - a_s
                            if ti > 0:
                                d = jnp.where(half_row >= ti, d, NEG_BIG)
                            acc_lo = acc_lo + jnp.sum(q_lo * jnp.exp2(d), axis=-1, keepdims=True) * v_s
                            d = c_hi - a_s
                        else:
                            d = c_hi - a_s
                            if ti > half:
                                d = jnp.where(half_row >= ti - half, d, NEG_BIG)
                        acc_hi = acc_hi + jnp.sum(q_hi * jnp.exp2(d), axis=-1, keepdims=True) * v_s
                    diag_rows += [acc_lo, acc_hi]
                outs.append(inter + off + jnp.concatenate(diag_rows, axis=0))
            st_ref[...] = st
            finish(rows, outs)
            return carry

        lax.fori_loop(0, lc // blk, general, 0)


def _hgrn(main4, f4, slabs, lb_logits, norm_g, passengers, *, lc):
    _, b, s, _ = main4.shape
    nh, dh = NUM_HEADS, HEAD_DIM
    nl = lb_logits.shape[0]
    assert lc % (HGRN_BLOCK * min(HGRN_GROUP, lc // HGRN_BLOCK)) == 0
    q_blk, i_blk, g_blk = slabs

    def col(first):
        return pl.BlockSpec((None, None, lc, dh), lambda i, h, j: (first + h, i, j, 0))

    n_lc = s // lc
    p_in, p_out, p_shapes = _passenger_specs(passengers, b * nh * n_lc,
                                             lambda i, h, j: (i * nh + h) * n_lc + j)
    return pl.pallas_call(
        _with_passengers(functools.partial(_hgrn_kernel, lc=lc), 6, 1, len(passengers)),
        grid=(b, nh, n_lc),
        in_specs=[
            col(q_blk), col(0), col(i_blk), col(g_blk),
            pl.BlockSpec((nl, dh), lambda i, h, j: (0, h)),
            pl.BlockSpec((1, dh), lambda i, h, j: (0, 0)),
        ] + p_in,
        out_specs=[pl.BlockSpec((None, lc, dh), lambda i, h, j: (i, j, h))] + p_out,
        out_shape=[jax.ShapeDtypeStruct((b, s, nh * dh), BF16)] + p_shapes,
        scratch_shapes=[
            pltpu.VMEM((dh, dh), F32),
            pltpu.VMEM((lc, dh), BF16),
            pltpu.VMEM((lc, dh), F32),
            pltpu.VMEM((lc // HGRN_CHUNK, dh, dh), F32),
            pltpu.VMEM((lc // HGRN_CHUNK * SUBLANES, dh), F32),
            pltpu.VMEM((HGRN_BLOCK, dh), F32),
            pltpu.VMEM((HGRN_BLOCK, dh), F32),
        ],
        compiler_params=_params("parallel", "parallel", "arbitrary"),
        name="hgrn",
    )(main4, f4, main4, main4, lb_logits, norm_g.reshape(1, dh), *passengers)


def _merge_kernel(h_ref, gpost_ref, ya_ref, yb_ref, ga_ref, gb_ref, wpa_ref, wpb_ref, wo_ref, o_ref):
    j = pl.program_id(1)

    def sweep_step(first, last):
        tn = wo_ref.shape[0]
        groups = [slice(c0, c0 + tn // MERGE_COL_GROUPS) for c0 in range(0, tn, tn // MERGE_COL_GROUPS)]
        stage1 = [(_dot(ya_ref[...], wpa_ref[:, c]), _dot(yb_ref[...], wpb_ref[:, c])) for c in groups]
        merged = [(ga_ref[:, c].astype(F32) * y_a + gb_ref[:, c].astype(F32) * y_b).astype(BF16)
                  for c, (y_a, y_b) in zip(groups, stage1)]
        out = _dot(merged[0], wo_ref[groups[0], :])
        for mg, c in zip(merged[1:], groups[1:]):
            out += _dot(mg, wo_ref[c, :])
        if first:
            o_ref[...] = out
        else:
            o_ref[...] += out
        if last:
            _postnorm_residual_to(o_ref, h_ref, o_ref, gpost_ref, 1.0)

    n_steps = pl.num_programs(1)
    pl.when(j == 0)(functools.partial(sweep_step, True, False))
    pl.when((j > 0) & (j < n_steps - 1))(functools.partial(sweep_step, False, False))
    pl.when(j == n_steps - 1)(functools.partial(sweep_step, False, True))


def _merge(h, g_post, y_a, y_b, gates, w_pa, w_pb, w_o, *, tm, tn):
    t, d = h.shape
    wa = y_a.shape[1]
    wb = y_b.shape[1]
    n_j = d // tn

    def residual_rows(i, j):
        return (jnp.where(j == n_j - 1, i, jnp.maximum(i - 1, 0)), 0)

    return pl.pallas_call(
        _merge_kernel,
        grid=(t // tm, n_j),
        in_specs=[
            pl.BlockSpec((tm, d), residual_rows),
            pl.BlockSpec((1, d), lambda i, j: (0, 0)),
            pl.BlockSpec((tm, wa), lambda i, j: (i, 0)),
            pl.BlockSpec((tm, wb), lambda i, j: (i, 0)),
            pl.BlockSpec((tm, tn), lambda i, j: (i, j)),
            pl.BlockSpec((tm, tn), lambda i, j: (i, d // tn + j)),
            pl.BlockSpec((wa, tn), lambda i, j: (0, j)),
            pl.BlockSpec((wb, tn), lambda i, j: (0, j)),
            pl.BlockSpec((tn, d), lambda i, j: (j, 0)),
        ],
        out_specs=pl.BlockSpec((tm, d), lambda i, j: (i, 0)),
        out_shape=jax.ShapeDtypeStruct((t, d), F32),
        compiler_params=_params("parallel", "arbitrary"),
        name="merge",
    )(h, g_post.reshape(1, d), y_a, y_b, gates, gates, w_pa, w_pb, w_o)


def _ple_kernel(h_ref, gpre_ref, gpost_ref, p_ref, wg_ref, wp_ref, o_ref):
    tm = h_ref.shape[0]
    groups = [slice(r0, r0 + tm // PLE_ROW_GROUPS) for r0 in range(0, tm, tm // PLE_ROW_GROUPS)]
    emb = [_dot(p_ref[r, :].astype(BF16), wp_ref[...]) for r in groups]
    gates = [_dot(_rms(h_ref[r, :], gpre_ref[...]).astype(BF16), wg_ref[...]) for r in groups]
    for r, e, g in zip(groups, emb, gates):
        o_ref[r, :] = h_ref[r, :] + _rms(jax.nn.sigmoid(g) * e, gpost_ref[...])


def _ple(h, g_pre, g_post, p2, w_g, w_p, *, tm):
    t, d = h.shape
    dp = p2.shape[1]
    return pl.pallas_call(
        _ple_kernel,
        grid=(t // tm,),
        in_specs=[
            pl.BlockSpec((tm, d), lambda i: (i, 0)),
            pl.BlockSpec((1, d), lambda i: (0, 0)),
            pl.BlockSpec((1, d), lambda i: (0, 0)),
            pl.BlockSpec((tm, dp), lambda i: (i, 0)),
            pl.BlockSpec((d, d), lambda i: (0, 0)),
            pl.BlockSpec((dp, d), lambda i: (0, 0)),
        ],
        out_specs=pl.BlockSpec((tm, d), lambda i: (i, 0)),
        out_shape=jax.ShapeDtypeStruct((t, d), F32),
        compiler_params=_params("parallel"),
        name="ple",
    )(h, g_pre.reshape(1, d), g_post.reshape(1, d), p2, w_g, w_p)


def _tile(n, want):
    t = min(n, want)
    while n % t:
        t //= 2
    return t


def kernel(x, p, ffn1_pre_g, ffn1_post_g, ffn1_w_gate, ffn1_w_up, ffn1_w_down, mix_pre_g, mix_post_g, mix_w_in, fox_f_bias, hgrn_lb_logits, hgrn_norm_g, mix_w_proj_fox, mix_w_proj_hgrn, mix_w_out, ffn2_pre_g, ffn2_post_g, ffn2_w_gate, ffn2_w_up, ffn2_w_down, ple_pre_g, ple_post_g, ple_w_gate, ple_w_proj):
    b, s, d = x.shape
    t = b * s
    depth = ffn1_pre_g.shape[0]
    assert depth == 1, "the HGRN2 lower bound is evaluated for a single layer"
    nh, dh = NUM_HEADS, HEAD_DIM
    width = nh * dh
    assert mix_w_in.shape[-1] == 3 * width + nh + 4 * width + 2 * d

    tm_big = _tile(t, 1024)
    h = x.reshape(t, d)
    for i in range(depth):
        h = _ffn(h, ffn1_pre_g[i], ffn1_post_g[i], ffn1_w_gate[i].astype(BF16),
                 ffn1_w_up[i].astype(BF16), ffn1_w_down[i].astype(BF16),
                 tm=tm_big, tf=_tile(ffn1_w_gate.shape[-1], 512))

        w_in = mix_w_in[i]
        o_f = 3 * width
        o_b = o_f + nh
        w_bf = w_in.astype(BF16)
        w_rest = w_bf[:, o_b:]

        main, gates, f = _proj_in(h, mix_pre_g[i], w_bf, w_rest, tm=tm_big, width=width,
                                  gate_width=2 * d, q_scale=dh ** -0.5 * LOG2E)
        main4 = main.reshape(6 * nh, b, s, dh)
        f4 = f.reshape(nh + 1, b, s, dh)
        c = _fox_gate(f4, nh, fox_f_bias[i], ts=_tile(s, 512))
        y_a, w2_gate, w2_up = _fox_attn(main4, c.reshape(b, nh, 1, s),
                                        [ffn2_w_gate[i], ffn2_w_up[i]], tq=_tile(s, 512))
        y_b, w2_down, w_pa, w_pb, w_o, w_pg = _hgrn(
            main4, f4, (3 * nh, 4 * nh, 5 * nh), hgrn_lb_logits, hgrn_norm_g[i],
            [ffn2_w_down[i], mix_w_proj_fox[i], mix_w_proj_hgrn[i], mix_w_out[i], ple_w_gate[i]],
            lc=_tile(s, 2048))
        h = _merge(h, mix_post_g[i], y_a.reshape(t, width), y_b.reshape(t, width), gates,
                   w_pa, w_pb, w_o, tm=tm_big, tn=_tile(d, 512))

        h = _ffn(h, ffn2_pre_g[i], ffn2_post_g[i], w2_gate, w2_up, w2_down,
                 tm=tm_big, tf=_tile(ffn2_w_gate.shape[-1], 512))
        h = _ple(h, ple_pre_g[i], ple_post_g[i], p[i].reshape(t, -1),
                 w_pg, ple_w_proj[i].astype(BF16), tm=_tile(t, 512))
    return h.reshape(b, s, d)
```

```python
import functools
import math

import jax
import jax.numpy as jnp
from jax import lax
from jax.experimental import pallas as pl
from jax.experimental.pallas import tpu as pltpu

NORM_EPS = 1e-6
MACARON_SCALE = 0.5
LOG2E = math.log2(math.e)
HEAD_DIM = 128
NUM_HEADS = 8
HGRN_CHUNK = 64
HGRN_SUB = 16
HGRN_BLOCK = 256
HGRN_GROUP = 8
HGRN_SAFE_LOG2 = 100.0
FFN_COL_GROUPS = 2
MERGE_COL_GROUPS = 2
PLE_ROW_GROUPS = 2
NORM_ROWS = 16
SUBLANES = 8
BF16_ROWS = 16
NEG_BIG = -1e30

VMEM_LIMIT_BYTES = 60 * 1024 * 1024

BF16 = jnp.bfloat16
F32 = jnp.float32


def _params(*semantics):
    return pltpu.CompilerParams(dimension_semantics=semantics,
                                vmem_limit_bytes=VMEM_LIMIT_BYTES)


def _rms(x, g):
    ms = jnp.mean(x * x, axis=-1, keepdims=True)
    return x * lax.rsqrt(ms + NORM_EPS) * g


def _postnorm_residual_to(o_ref, x_ref, y_ref, g_ref, scale):
    g = g_ref[...] if scale == 1.0 else scale * g_ref[...]
    for r0 in range(0, x_ref.shape[0], NORM_ROWS):
        rows = slice(r0, r0 + NORM_ROWS)
        o_ref[rows, :] = x_ref[rows, :] + _rms(y_ref[rows, :], g)


def _sigmoid(x):
    return 0.5 + 0.5 * jnp.tanh(0.5 * x)


def _silu(x):
    h = 0.5 * x
    return h + h * jnp.tanh(h)


def _dot(a, b):
    return jnp.dot(a, b, preferred_element_type=F32)


def _dot_nt(a, b):
    return lax.dot_general(a, b, (((1,), (1,)), ((), ())), preferred_element_type=F32)


def _split3(x):
    hi = x.astype(BF16)
    r = x - hi.astype(F32)
    mid = r.astype(BF16)
    lo = (r - mid.astype(F32)).astype(BF16)
    return hi, mid, lo


def _with_passengers(body, n_in, n_out, n_pass):
    def kernel(*refs):
        ins, refs = refs[:n_in], refs[n_in:]
        pass_in, refs = refs[:n_pass], refs[n_pass:]
        outs, refs = refs[:n_out], refs[n_out:]
        pass_out, scratch = refs[:n_pass], refs[n_pass:]
        for src, dst in zip(pass_in, pass_out):
            dst[...] = src[...].astype(dst.dtype)
        body(*ins, *outs, *scratch)

    return kernel


def _passenger_specs(weights, n_steps, step_of):
    in_specs, out_specs, out_shapes = [], [], []
    for w in weights:
        r, c = w.shape
        share = 1
        while (r * share) % n_steps or (r * share // n_steps) % BF16_ROWS:
            share *= 2
        spec = pl.BlockSpec((r * share // n_steps, c), lambda *g, share=share: (step_of(*g) // share, 0))
        in_specs.append(spec)
        out_specs.append(spec)
        out_shapes.append(jax.ShapeDtypeStruct((r, c), BF16))
    return in_specs, out_specs, out_shapes


def _ffn_kernel(x_ref, gpre_ref, gpost_ref, wg_ref, wu_ref, wd_ref, o_ref, xn_ref):
    j = pl.program_id(1)

    def sweep_step(first, last):
        if first:
            xn_ref[...] = _rms(x_ref[...], gpre_ref[...]).astype(BF16)
        xn = xn_ref[...]
        tf = wg_ref.shape[1]
        groups = [slice(c0, c0 + tf // FFN_COL_GROUPS) for c0 in range(0, tf, tf // FFN_COL_GROUPS)]
        gu = [(_dot(xn, wg_ref[:, c]), _dot(xn, wu_ref[:, c])) for c in groups]
        acts = [(g * jax.nn.sigmoid(g) * u).astype(BF16) for g, u in gu]
        down = _dot(acts[0], wd_ref[groups[0], :])
        for a, c in zip(acts[1:], groups[1:]):
            down += _dot(a, wd_ref[c, :])
        if first:
            o_ref[...] = down
        else:
            o_ref[...] += down
        if last:
            _postnorm_residual_to(o_ref, x_ref, o_ref, gpost_ref, MACARON_SCALE)

    n_steps = pl.num_programs(1)
    pl.when(j == 0)(functools.partial(sweep_step, True, False))
    pl.when((j > 0) & (j < n_steps - 1))(functools.partial(sweep_step, False, False))
    pl.when(j == n_steps - 1)(functools.partial(sweep_step, False, True))


def _ffn(h, g_pre, g_post, w_gate, w_up, w_down, *, tm, tf):
    t, d = h.shape
    f = w_gate.shape[1]
    return pl.pallas_call(
        _ffn_kernel,
        grid=(t // tm, f // tf),
        in_specs=[
            pl.BlockSpec((tm, d), lambda i, j: (i, 0)),
            pl.BlockSpec((1, d), lambda i, j: (0, 0)),
            pl.BlockSpec((1, d), lambda i, j: (0, 0)),
            pl.BlockSpec((d, tf), lambda i, j: (0, j)),
            pl.BlockSpec((d, tf), lambda i, j: (0, j)),
            pl.BlockSpec((tf, d), lambda i, j: (j, 0)),
        ],
        out_specs=pl.BlockSpec((tm, d), lambda i, j: (i, 0)),
        out_shape=jax.ShapeDtypeStruct((t, d), F32),
        scratch_shapes=[pltpu.VMEM((tm, d), BF16)],
        compiler_params=_params("parallel", "arbitrary"),
        name="ffn",
    )(h, g_pre.reshape(1, d), g_post.reshape(1, d), w_gate, w_up, w_down)


PROJ_FOX_STEPS = 3
PROJ_MAIN_STEPS = 6


def _proj_in_kernel(x_ref, g_ref, wa_ref, wr_ref, wfa_ref, o_ref, og_ref, of_ref, xn_ref, *, q_scale):
    j = pl.program_id(1)
    last = pl.num_programs(1) - 1

    def store_heads(dst_ref, y):
        for hh in range(dst_ref.shape[0]):
            dst_ref[hh] = y[:, hh * HEAD_DIM:(hh + 1) * HEAD_DIM].astype(dst_ref.dtype)

    @pl.when(j == 0)
    def _():
        xn_ref[...] = _rms(x_ref[...], g_ref[...]).astype(BF16)
        store_heads(o_ref, _dot(xn_ref[...], wa_ref[...]) * q_scale)

    @pl.when((j > 0) & (j < PROJ_FOX_STEPS))
    def _():
        store_heads(o_ref, _dot(xn_ref[...], wa_ref[...]))

    @pl.when((j >= PROJ_FOX_STEPS) & (j < PROJ_MAIN_STEPS))
    def _():
        store_heads(o_ref, _dot(xn_ref[...], wr_ref[...]))

    @pl.when((j >= PROJ_MAIN_STEPS) & (j < last))
    def _():
        og_ref[...] = _sigmoid(_dot(xn_ref[...], wr_ref[...])).astype(og_ref.dtype)

    @pl.when(j == last)
    def _():
        nh = wr_ref.shape[1] // HEAD_DIM
        store_heads(of_ref.at[:nh], _dot(xn_ref[...], wr_ref[...]))
        of_ref[nh] = _dot(xn_ref[...], wfa_ref[...])


def _proj_in(h, g, w_fox, w_rest, *, tm, width, gate_width, q_scale):
    t, d = h.shape
    nh = width // HEAD_DIM
    assert w_fox.shape[1] >= PROJ_FOX_STEPS * width + HEAD_DIM
    assert w_rest.shape[1] == 4 * width + gate_width and gate_width % width == 0
    n_gate = gate_width // width
    last = PROJ_MAIN_STEPS + n_gate

    def rest_col(i, j):
        return (0, jnp.where(j <= 3, 0, jnp.where(j == last, 1, j - 2)))

    def gate_step(j):
        return jnp.clip(j - PROJ_MAIN_STEPS, 0, n_gate - 1)

    return pl.pallas_call(
        functools.partial(_proj_in_kernel, q_scale=q_scale),
        grid=(t // tm, last + 1),
        in_specs=[
            pl.BlockSpec((tm, d), lambda i, j: (i, 0)),
            pl.BlockSpec((1, d), lambda i, j: (0, 0)),
            pl.BlockSpec((d, width), lambda i, j: (0, jnp.minimum(j, PROJ_FOX_STEPS - 1))),
            pl.BlockSpec((d, width), rest_col),
            pl.BlockSpec((d, HEAD_DIM), lambda i, j: (0, PROJ_FOX_STEPS * nh)),
        ],
        out_specs=[
            pl.BlockSpec((nh, tm, HEAD_DIM), lambda i, j: (jnp.minimum(j, PROJ_MAIN_STEPS - 1), i, 0)),
            pl.BlockSpec((tm, width), lambda i, j: (i, gate_step(j))),
            pl.BlockSpec((nh + 1, tm, HEAD_DIM), lambda i, j: (0, i, 0)),
        ],
        out_shape=[jax.ShapeDtypeStruct((PROJ_MAIN_STEPS * nh, t, HEAD_DIM), BF16),
                   jax.ShapeDtypeStruct((t, gate_width), BF16),
                   jax.ShapeDtypeStruct((nh + 1, t, HEAD_DIM), F32)],
        scratch_shapes=[pltpu.VMEM((tm, d), BF16)],
        compiler_params=_params("parallel", "arbitrary"),
        name="proj_in",
    )(h, g.reshape(1, d), w_fox, w_rest, w_fox)


def _fox_gate_kernel(f_ref, b_ref, o_ref, carry_ref, *, ts):
    @pl.when(pl.program_id(1) == 0)
    def _():
        carry_ref[...] = jnp.zeros_like(carry_ref)

    z = f_ref[...].T[:NUM_HEADS, :] + b_ref[...]
    lf = (jnp.minimum(z, 0.0) - jnp.log1p(jnp.exp(-jnp.abs(z)))) * LOG2E
    src = lax.broadcasted_iota(jnp.int32, (ts, ts), 0)
    dst = lax.broadcasted_iota(jnp.int32, (ts, ts), 1)
    tri = jnp.where(src <= dst, 1.0, 0.0).astype(BF16)
    hi, mid, lo = _split3(lf)
    c = _dot(hi, tri) + _dot(mid, tri) + _dot(lo, tri) + carry_ref[:, :1]
    o_ref[...] = c
    carry_ref[...] = jnp.broadcast_to(c[:, ts - 1:ts], carry_ref.shape)


def _fox_gate(f4, slab, bias, *, ts):
    _, b, s, _ = f4.shape
    nh = NUM_HEADS
    return pl.pallas_call(
        functools.partial(_fox_gate_kernel, ts=ts),
        grid=(b, s // ts),
        in_specs=[
            pl.BlockSpec((None, None, ts, HEAD_DIM), lambda i, j: (slab, i, j, 0)),
            pl.BlockSpec((nh, 1), lambda i, j: (0, 0)),
        ],
        out_specs=pl.BlockSpec((None, nh, ts), lambda i, j: (i, 0, j)),
        out_shape=jax.ShapeDtypeStruct((b, nh, s), F32),
        scratch_shapes=[pltpu.VMEM((nh, HEAD_DIM), F32)],
        compiler_params=_params("parallel", "arbitrary"),
        name="fox_gate",
    )(f4, bias.reshape(nh, 1))


def _fox_attn_kernel(q_ref, k_ref, v_ref, c_ref, o_ref, m_ref, acc_ref, vone_ref, *, tq):
    nq = q_ref.shape[0] // tq
    reps = tq // HEAD_DIM
    dh = v_ref.shape[1]

    vone_ref[:, :dh] = v_ref[...]
    vone_ref[:, dh:] = jnp.ones_like(v_ref)
    m_ref[...] = jnp.full_like(m_ref, NEG_BIG)
    acc_ref[...] = jnp.zeros_like(acc_ref)

    def step(qi, ki):
        rows = slice(qi * tq, (qi + 1) * tq)
        cols = slice(ki * tq, (ki + 1) * tq)
        c0 = jnp.max(c_ref[:, rows], axis=-1, keepdims=True)
        s = _dot_nt(q_ref[rows, :], k_ref[cols, :]) + (c0 - c_ref[:, cols])
        if qi == ki:
            row = lax.broadcasted_iota(jnp.int32, (tq, tq), 0)
            col = lax.broadcasted_iota(jnp.int32, (tq, tq), 1)
            s = jnp.where(col <= row, s, NEG_BIG)
        m_prev = m_ref[rows, :]
        m_next = jnp.maximum(m_prev, jnp.max(s.astype(BF16), axis=-1, keepdims=True).astype(F32))
        p = jnp.exp2(s - jnp.concatenate([m_next] * reps, axis=1))
        alpha = jnp.exp2(m_prev - m_next)
        acc_ref[rows, :] = (jnp.concatenate([alpha, alpha], axis=1) * acc_ref[rows, :]
                            + _dot(p.astype(BF16), vone_ref[cols, :]))
        m_ref[rows, :] = m_next

    for diag in range(nq):
        for qi in range(diag, nq):
            step(qi, qi - diag)
    o_ref[...] = (acc_ref[:, :dh] / acc_ref[:, dh:]).astype(o_ref.dtype)


def _fox_attn(main4, c4, passengers, *, tq):
    _, b, s, _ = main4.shape
    nh, dh = NUM_HEADS, HEAD_DIM
    p_in, p_out, p_shapes = _passenger_specs(passengers, b * nh, lambda i, h: i * nh + h)
    return pl.pallas_call(
        _with_passengers(functools.partial(_fox_attn_kernel, tq=tq), 4, 1, len(passengers)),
        grid=(b, nh),
        in_specs=[
            pl.BlockSpec((None, None, s, dh), lambda i, h: (h, i, 0, 0)),
            pl.BlockSpec((None, None, s, dh), lambda i, h: (nh + h, i, 0, 0)),
            pl.BlockSpec((None, None, s, dh), lambda i, h: (2 * nh + h, i, 0, 0)),
            pl.BlockSpec((None, None, 1, s), lambda i, h: (i, h, 0, 0)),
        ] + p_in,
        out_specs=[pl.BlockSpec((None, s, dh), lambda i, h: (i, 0, h))] + p_out,
        out_shape=[jax.ShapeDtypeStruct((b, s, nh * dh), BF16)] + p_shapes,
        scratch_shapes=[pltpu.VMEM((s, dh), F32), pltpu.VMEM((s, 2 * dh), F32),
                        pltpu.VMEM((s, 2 * dh), BF16)],
        compiler_params=_params("parallel", "parallel"),
        name="fox_attn",
    )(main4, main4, main4, c4, *passengers)


def _hgrn_kernel(q_ref, f_ref, i_ref, g_ref, lbl_ref, ng_ref, o_ref,
                 st_ref, qe_ref, intra_ref, u_ref, dec_ref, a_ref, v_ref, *, lc):
    cs, sb, blk = HGRN_CHUNK, HGRN_SUB, HGRN_BLOCK
    half = sb // 2
    cpb = blk // cs

    @pl.when(pl.program_id(2) == 0)
    def _():
        st_ref[...] = jnp.zeros_like(st_ref)

    logits = lbl_ref[...]
    e = jnp.exp(logits - jnp.max(logits, axis=0, keepdims=True))
    lb = e[0:1, :] / jnp.sum(e, axis=0, keepdims=True)
    ng = ng_ref[...]

    src = lax.broadcasted_iota(jnp.int32, (blk, blk), 1)
    dst = lax.broadcasted_iota(jnp.int32, (blk, blk), 0)
    same_chunk = (src // cs) == (dst // cs)
    tri = jnp.where(same_chunk & (src <= dst), 1.0, 0.0).astype(BF16)
    half_row = lax.broadcasted_iota(jnp.int32, (half, HEAD_DIM), 0)
    blk_col = lax.broadcasted_iota(jnp.int32, (sb, cs), 1)
    chunk_row = lax.broadcasted_iota(jnp.int32, (cs, cs), 0)
    chunk_col = lax.broadcasted_iota(jnp.int32, (cs, cs), 1)

    def prep(rows):
        q = _silu(q_ref[rows, :].astype(F32))
        f = lb + (1.0 - lb) * _sigmoid(f_ref[rows, :])
        kk = 1.0 - f
        hi, mid, lo = _split3(jnp.log2(f))
        cum = _dot(tri, hi) + _dot(tri, mid) + _dot(tri, lo)
        lasts = [cum[c0 + cs - 1:c0 + cs, :] for c0 in range(0, blk, cs)]
        return q, kk, cum, lasts

    def state_increment(ci, kk, cum, lasts, v):
        c0 = ci * cs
        kdec = (kk[c0:c0 + cs, :] * jnp.exp2(lasts[ci] - cum[c0:c0 + cs, :])).astype(BF16)
        return _dot(v[c0:c0 + cs, :].T.astype(BF16), kdec)

    def finish(rows, outs):
        o = _rms(jnp.concatenate(outs, axis=0), ng) * _silu(g_ref[rows, :].astype(F32))
        o_ref[rows, :] = o.astype(o_ref.dtype)

    group = min(HGRN_GROUP, lc // blk)

    def precompute(gi, deepest):
        blocks = [gi * group + k for k in range(group)]
        rows = [pl.ds(pl.multiple_of(bi * blk, blk), blk) for bi in blocks]
        stage1 = [prep(r) for r in rows]
        stage2 = []
        for bi, r, (q, kk, cum, lasts) in zip(blocks, rows, stage1):
            v = i_ref[r, :]
            v32 = v.astype(F32)
            qe = (q * jnp.exp2(cum)).astype(BF16)
            kinv32 = kk * jnp.exp2(-cum)
            kinv = kinv32.astype(BF16)
            qe_ref[r, :] = qe
            scores = [_dot_nt(qe[c0:c0 + cs, :], kinv[c0:c0 + cs, :]) for c0 in range(0, blk, cs)]
            for ci in range(cpb):
                c0 = ci * cs
                dec = jnp.exp2(lasts[ci])
                kdec = (kinv32[c0:c0 + cs, :] * dec).astype(BF16)
                u_ref[bi * cpb + ci] = _dot(v32[c0:c0 + cs, :].T.astype(BF16), kdec)
                dec_ref[pl.ds(pl.multiple_of((bi * cpb + ci) * SUBLANES, SUBLANES), SUBLANES), :] = (
                    jnp.broadcast_to(dec, (SUBLANES, HEAD_DIM)))
                deepest = jnp.maximum(deepest, -lasts[ci])
            stage2.append((v, scores))
        for r, (v, scores) in zip(rows, stage2):
            intra = [_dot(jnp.where(chunk_col <= chunk_row, sc, 0.0).astype(BF16), v[ci * cs:(ci + 1) * cs, :])
                     for ci, sc in enumerate(scores)]
            intra_ref[r, :] = jnp.concatenate(intra, axis=0)
        return deepest

    deepest = lax.fori_loop(0, lc // (blk * group), precompute, jnp.zeros((1, HEAD_DIM), F32))
    depth = jnp.max(deepest)

    @pl.when(depth <= HGRN_SAFE_LOG2)
    def _():
        def recur(bi, carry):
            base = pl.multiple_of(bi * blk, blk)
            st = st_ref[...]
            outs = []
            for ci in range(cpb):
                rows_c = pl.ds(base + ci * cs, cs)
                dec = dec_ref[pl.ds(pl.multiple_of((bi * cpb + ci) * SUBLANES, SUBLANES), SUBLANES), :]
                outs.append(_dot_nt(qe_ref[rows_c, :], st.astype(BF16)) + intra_ref[rows_c, :])
                st = st * dec[0:1, :] + u_ref[bi * cpb + ci]
            st_ref[...] = st
            finish(pl.ds(base, blk), outs)
            return carry

        lax.fori_loop(0, lc // blk, recur, 0, unroll=True)

    @pl.when(jnp.logical_not(depth <= HGRN_SAFE_LOG2))
    def _():
        def general(bi, carry):
            rows = pl.ds(pl.multiple_of(bi * blk, blk), blk)
            q, kk, cum, lasts = prep(rows)
            v = i_ref[rows, :]
            v32 = v.astype(F32)
            a_ref[...] = cum - jnp.log2(kk)
            v_ref[...] = v32
            st = st_ref[...]
            outs = []
            for ci in range(cpb):
                c0 = ci * cs
                cum_c = cum[c0:c0 + cs, :]
                q_c = q[c0:c0 + cs, :]
                kk_c = kk[c0:c0 + cs, :]
                inter = _dot_nt((q_c * jnp.exp2(cum_c)).astype(BF16), st.astype(BF16))
                st = st * jnp.exp2(lasts[ci]) + state_increment(ci, kk, cum, lasts, v32)

                score_rows = [jnp.zeros((sb, cs), F32)]
                for si in range(1, cs // sb):
                    r0 = si * sb
                    ref = cum_c[r0 - 1:r0, :]
                    qt = q_c[r0:r0 + sb, :] * jnp.exp2(cum_c[r0:r0 + sb, :] - ref)
                    kt = kk_c * jnp.exp2(jnp.minimum(ref - cum_c, 0.0))
                    sc = _dot_nt(qt.astype(BF16), kt.astype(BF16))
                    score_rows.append(jnp.where(blk_col < r0, sc, 0.0))
                off = _dot(jnp.concatenate(score_rows, axis=0).astype(BF16), v[c0:c0 + cs, :])

                diag_rows = []
                for si in range(cs // sb):
                    r0 = c0 + si * sb
                    q_lo, q_hi = q[r0:r0 + half, :], q[r0 + half:r0 + sb, :]
                    c_lo, c_hi = cum[r0:r0 + half, :], cum[r0 + half:r0 + sb, :]
                    acc_lo = jnp.zeros((half, HEAD_DIM), F32)
                    acc_hi = jnp.zeros((half, HEAD_DIM), F32)
                    for ti in range(sb):
                        a_s = a_ref[r0 + ti:r0 + ti + 1, :]
                        v_s = v_ref[r0 + ti:r0 + ti + 1, :]
                        if ti < half:
                            d = c_lo - a_s
                            if ti > 0:
                                d = jnp.where(half_row >= ti, d, NEG_BIG)
                            acc_lo = acc_lo + jnp.sum(q_lo * jnp.exp2(d), axis=-1, keepdims=True) * v_s
                            d = c_hi - a_s
                        else:
                            d = c_hi - a_s
                            if ti > half:
                                d = jnp.where(half_row >= ti - half, d, NEG_BIG)
                        acc_hi = acc_hi + jnp.sum(q_hi * jnp.exp2(d), axis=-1, keepdims=True) * v_s
                    diag_rows += [acc_lo, acc_hi]
                outs.append(inter + off + jnp.concatenate(diag_rows, axis=0))
            st_ref[...] = st
            finish(rows, outs)
            return carry

        lax.fori_loop(0, lc // blk, general, 0)


def _hgrn(main4, f4, slabs, lb_logits, norm_g, passengers, *, lc):
    _, b, s, _ = main4.shape
    nh, dh = NUM_HEADS, HEAD_DIM
    nl = lb_logits.shape[0]
    assert lc % (HGRN_BLOCK * min(HGRN_GROUP, lc // HGRN_BLOCK)) == 0
    q_blk, i_blk, g_blk = slabs

    def col(first):
        return pl.BlockSpec((None, None, lc, dh), lambda i, h, j: (first + h, i, j, 0))

    n_lc = s // lc
    p_in, p_out, p_shapes = _passenger_specs(passengers, b * nh * n_lc,
                                             lambda i, h, j: (i * nh + h) * n_lc + j)
    return pl.pallas_call(
        _with_passengers(functools.partial(_hgrn_kernel, lc=lc), 6, 1, len(passengers)),
        grid=(b, nh, n_lc),
        in_specs=[
            col(q_blk), col(0), col(i_blk), col(g_blk),
            pl.BlockSpec((nl, dh), lambda i, h, j: (0, h)),
            pl.BlockSpec((1, dh), lambda i, h, j: (0, 0)),
        ] + p_in,
        out_specs=[pl.BlockSpec((None, lc, dh), lambda i, h, j: (i, j, h))] + p_out,
        out_shape=[jax.ShapeDtypeStruct((b, s, nh * dh), BF16)] + p_shapes,
        scratch_shapes=[
            pltpu.VMEM((dh, dh), F32),
            pltpu.VMEM((lc, dh), BF16),
            pltpu.VMEM((lc, dh), F32),
            pltpu.VMEM((lc // HGRN_CHUNK, dh, dh), F32),
            pltpu.VMEM((lc // HGRN_CHUNK * SUBLANES, dh), F32),
            pltpu.VMEM((HGRN_BLOCK, dh), F32),
            pltpu.VMEM((HGRN_BLOCK, dh), F32),
        ],
        compiler_params=_params("parallel", "parallel", "arbitrary"),
        name="hgrn",
    )(main4, f4, main4, main4, lb_logits, norm_g.reshape(1, dh), *passengers)


def _merge_kernel(h_ref, gpost_ref, ya_ref, yb_ref, ga_ref, gb_ref, wpa_ref, wpb_ref, wo_ref, o_ref):
    j = pl.program_id(1)

    def sweep_step(first, last):
        tn = wo_ref.shape[0]
        groups = [slice(c0, c0 + tn // MERGE_COL_GROUPS) for c0 in range(0, tn, tn // MERGE_COL_GROUPS)]
        stage1 = [(_dot(ya_ref[...], wpa_ref[:, c]), _dot(yb_ref[...], wpb_ref[:, c])) for c in groups]
        merged = [(ga_ref[:, c].astype(F32) * y_a + gb_ref[:, c].astype(F32) * y_b).astype(BF16)
                  for c, (y_a, y_b) in zip(groups, stage1)]
        out = _dot(merged[0], wo_ref[groups[0], :])
        for mg, c in zip(merged[1:], groups[1:]):
            out += _dot(mg, wo_ref[c, :])
        if first:
            o_ref[...] = out
        else:
            o_ref[...] += out
        if last:
            _postnorm_residual_to(o_ref, h_ref, o_ref, gpost_ref, 1.0)

    n_steps = pl.num_programs(1)
    pl.when(j == 0)(functools.partial(sweep_step, True, False))
    pl.when((j > 0) & (j < n_steps - 1))(functools.partial(sweep_step, False, False))
    pl.when(j == n_steps - 1)(functools.partial(sweep_step, False, True))


def _merge(h, g_post, y_a, y_b, gates, w_pa, w_pb, w_o, *, tm, tn):
    t, d = h.shape
    wa = y_a.shape[1]
    wb = y_b.shape[1]
    n_j = d // tn

    def residual_rows(i, j):
        return (jnp.where(j == n_j - 1, i, jnp.maximum(i - 1, 0)), 0)

    return pl.pallas_call(
        _merge_kernel,
        grid=(t // tm, n_j),
        in_specs=[
            pl.BlockSpec((tm, d), residual_rows),
            pl.BlockSpec((1, d), lambda i, j: (0, 0)),
            pl.BlockSpec((tm, wa), lambda i, j: (i, 0)),
            pl.BlockSpec((tm, wb), lambda i, j: (i, 0)),
            pl.BlockSpec((tm, tn), lambda i, j: (i, j)),
            pl.BlockSpec((tm, tn), lambda i, j: (i, d // tn + j)),
            pl.BlockSpec((wa, tn), lambda i, j: (0, j)),
            pl.BlockSpec((wb, tn), lambda i, j: (0, j)),
            pl.BlockSpec((tn, d), lambda i, j: (j, 0)),
        ],
        out_specs=pl.BlockSpec((tm, d), lambda i, j: (i, 0)),
        out_shape=jax.ShapeDtypeStruct((t, d), F32),
        compiler_params=_params("parallel", "arbitrary"),
        name="merge",
    )(h, g_post.reshape(1, d), y_a, y_b, gates, gates, w_pa, w_pb, w_o)


def _ple_kernel(h_ref, gpre_ref, gpost_ref, p_ref, wg_ref, wp_ref, o_ref):
    tm = h_ref.shape[0]
    groups = [slice(r0, r0 + tm // PLE_ROW_GROUPS) for r0 in range(0, tm, tm // PLE_ROW_GROUPS)]
    emb = [_dot(p_ref[r, :].astype(BF16), wp_ref[...]) for r in groups]
    gates = [_dot(_rms(h_ref[r, :], gpre_ref[...]).astype(BF16), wg_ref[...]) for r in groups]
    for r, e, g in zip(groups, emb, gates):
        o_ref[r, :] = h_ref[r, :] + _rms(jax.nn.sigmoid(g) * e, gpost_ref[...])


def _ple(h, g_pre, g_post, p2, w_g, w_p, *, tm):
    t, d = h.shape
    dp = p2.shape[1]
    return pl.pallas_call(
        _ple_kernel,
        grid=(t // tm,),
        in_specs=[
            pl.BlockSpec((tm, d), lambda i: (i, 0)),
            pl.BlockSpec((1, d), lambda i: (0, 0)),
            pl.BlockSpec((1, d), lambda i: (0, 0)),
            pl.BlockSpec((tm, dp), lambda i: (i, 0)),
            pl.BlockSpec((d, d), lambda i: (0, 0)),
            pl.BlockSpec((dp, d), lambda i: (0, 0)),
        ],
        out_specs=pl.BlockSpec((tm, d), lambda i: (i, 0)),
        out_shape=jax.ShapeDtypeStruct((t, d), F32),
        compiler_params=_params("parallel"),
        name="ple",
    )(h, g_pre.reshape(1, d), g_post.reshape(1, d), p2, w_g, w_p)


def _tile(n, want):
    t = min(n, want)
    while n % t:
        t //= 2
    return t


def kernel(x, p, ffn1_pre_g, ffn1_post_g, ffn1_w_gate, ffn1_w_up, ffn1_w_down, mix_pre_g, mix_post_g, mix_w_in, fox_f_bias, hgrn_lb_logits, hgrn_norm_g, mix_w_proj_fox, mix_w_proj_hgrn, mix_w_out, ffn2_pre_g, ffn2_post_g, ffn2_w_gate, ffn2_w_up, ffn2_w_down, ple_pre_g, ple_post_g, ple_w_gate, ple_w_proj):
    b, s, d = x.shape
    t = b * s
    depth = ffn1_pre_g.shape[0]
    assert depth == 1, "the HGRN2 lower bound is evaluated for a single layer"
    nh, dh = NUM_HEADS, HEAD_DIM
    width = nh * dh
    assert mix_w_in.shape[-1] == 3 * width + nh + 4 * width + 2 * d

    tm_big = _tile(t, 1024)
    h = x.reshape(t, d)
    for i in range(depth):
        h = _ffn(h, ffn1_pre_g[i], ffn1_post_g[i], ffn1_w_gate[i].astype(BF16),
                 ffn1_w_up[i].astype(BF16), ffn1_w_down[i].astype(BF16),
                 tm=tm_big, tf=_tile(ffn1_w_gate.shape[-1], 512))

        w_in = mix_w_in[i]
        o_f = 3 * width
        o_b = o_f + nh
        w_bf = w_in.astype(BF16)
        w_rest = w_bf[:, o_b:]

        main, gates, f = _proj_in(h, mix_pre_g[i], w_bf, w_rest, tm=tm_big, width=width,
                                  gate_width=2 * d, q_scale=dh ** -0.5 * LOG2E)
        main4 = main.reshape(6 * nh, b, s, dh)
        f4 = f.reshape(nh + 1, b, s, dh)
        c = _fox_gate(f4, nh, fox_f_bias[i], ts=_tile(s, 512))
        y_a, w2_gate, w2_up = _fox_attn(main4, c.reshape(b, nh, 1, s),
                                        [ffn2_w_gate[i], ffn2_w_up[i]], tq=_tile(s, 512))
        y_b, w2_down, w_pa, w_pb, w_o, w_pg = _hgrn(
            main4, f4, (3 * nh, 4 * nh, 5 * nh), hgrn_lb_logits, hgrn_norm_g[i],
            [ffn2_w_down[i], mix_w_proj_fox[i], mix_w_proj_hgrn[i], mix_w_out[i], ple_w_gate[i]],
            lc=_tile(s, 2048))
        h = _merge(h, mix_post_g[i], y_a.reshape(t, width), y_b.reshape(t, width), gates,
                   w_pa, w_pb, w_o, tm=tm_big, tn=_tile(d, 512))

        h = _ffn(h, ffn2_pre_g[i], ffn2_post_g[i], w2_gate, w2_up, w2_down,
                 tm=tm_big, tf=_tile(ffn2_w_gate.shape[-1], 512))
        h = _ple(h, ple_pre_g[i], ple_post_g[i], p[i].reshape(t, -1),
                 w_pg, ple_w_proj[i].astype(BF16), tm=_tile(t, 512))
    return h.reshape(b, s, d)
```

```python
import functools
import math

import jax
import jax.numpy as jnp
from jax import lax
from jax.experimental import pallas as pl
from jax.experimental.pallas import tpu as pltpu

NORM_EPS = 1e-6
MACARON_SCALE = 0.5
LOG2E = math.log2(math.e)
HEAD_DIM = 128
NUM_HEADS = 8
HGRN_CHUNK = 64
HGRN_SUB = 16
HGRN_BLOCK = 256
HGRN_GROUP = 8
HGRN_SAFE_LOG2 = 100.0
FFN_COL_GROUPS = 2
MERGE_COL_GROUPS = 2
PLE_ROW_GROUPS = 2
NORM_ROWS = 16
ATTN_HEADS_PER_STEP = 4
SUBLANES = 8
BF16_ROWS = 16
NEG_BIG = -1e30

VMEM_LIMIT_BYTES = 60 * 1024 * 1024

BF16 = jnp.bfloat16
F32 = jnp.float32


def _params(*semantics):
    return pltpu.CompilerParams(dimension_semantics=semantics,
                                vmem_limit_bytes=VMEM_LIMIT_BYTES)


def _rms(x, g):
    ms = jnp.mean(x * x, axis=-1, keepdims=True)
    return x * lax.rsqrt(ms + NORM_EPS) * g


def _postnorm_residual_to(o_ref, x_ref, y_ref, g_ref, scale):
    g = g_ref[...] if scale == 1.0 else scale * g_ref[...]
    for r0 in range(0, x_ref.shape[0], NORM_ROWS):
        rows = slice(r0, r0 + NORM_ROWS)
        o_ref[rows, :] = x_ref[rows, :] + _rms(y_ref[rows, :], g)


def _sigmoid(x):
    return 0.5 + 0.5 * jnp.tanh(0.5 * x)


def _silu(x):
    h = 0.5 * x
    return h + h * jnp.tanh(h)


def _dot(a, b):
    return jnp.dot(a, b, preferred_element_type=F32)


def _dot_nt(a, b):
    return lax.dot_general(a, b, (((1,), (1,)), ((), ())), preferred_element_type=F32)


def _split3(x):
    hi = x.astype(BF16)
    r = x - hi.astype(F32)
    mid = r.astype(BF16)
    lo = (r - mid.astype(F32)).astype(BF16)
    return hi, mid, lo


def _with_passengers(body, n_in, n_out, n_pass):
    def kernel(*refs):
        ins, refs = refs[:n_in], refs[n_in:]
        pass_in, refs = refs[:n_pass], refs[n_pass:]
        outs, refs = refs[:n_out], refs[n_out:]
        pass_out, scratch = refs[:n_pass], refs[n_pass:]
        for src, dst in zip(pass_in, pass_out):
            dst[...] = src[...].astype(dst.dtype)
        body(*ins, *outs, *scratch)

    return kernel


def _passenger_specs(weights, n_steps, step_of):
    in_specs, out_specs, out_shapes = [], [], []
    for w in weights:
        r, c = w.shape
        share = 1
        while (r * share) % n_steps or (r * share // n_steps) % BF16_ROWS:
            share *= 2
        spec = pl.BlockSpec((r * share // n_steps, c), lambda *g, share=share: (step_of(*g) // share, 0))
        in_specs.append(spec)
        out_specs.append(spec)
        out_shapes.append(jax.ShapeDtypeStruct((r, c), BF16))
    return in_specs, out_specs, out_shapes


def _ffn_kernel(x_ref, gpre_ref, gpost_ref, wg_ref, wu_ref, wd_ref, o_ref, xn_ref):
    j = pl.program_id(1)

    def sweep_step(first, last):
        if first:
            xn_ref[...] = _rms(x_ref[...], gpre_ref[...]).astype(BF16)
        xn = xn_ref[...]
        tf = wg_ref.shape[1]
        groups = [slice(c0, c0 + tf // FFN_COL_GROUPS) for c0 in range(0, tf, tf // FFN_COL_GROUPS)]
        gu = [(_dot(xn, wg_ref[:, c]), _dot(xn, wu_ref[:, c])) for c in groups]
        acts = [(g * jax.nn.sigmoid(g) * u).astype(BF16) for g, u in gu]
        down = _dot(acts[0], wd_ref[groups[0], :])
        for a, c in zip(acts[1:], groups[1:]):
            down += _dot(a, wd_ref[c, :])
        if first:
            o_ref[...] = down
        else:
            o_ref[...] += down
        if last:
            _postnorm_residual_to(o_ref, x_ref, o_ref, gpost_ref, MACARON_SCALE)

    n_steps = pl.num_programs(1)
    pl.when(j == 0)(functools.partial(sweep_step, True, False))
    pl.when((j > 0) & (j < n_steps - 1))(functools.partial(sweep_step, False, False))
    pl.when(j == n_steps - 1)(functools.partial(sweep_step, False, True))


def _ffn(h, g_pre, g_post, w_gate, w_up, w_down, *, tm, tf):
    t, d = h.shape
    f = w_gate.shape[1]
    return pl.pallas_call(
        _ffn_kernel,
        grid=(t // tm, f // tf),
        in_specs=[
            pl.BlockSpec((tm, d), lambda i, j: (i, 0)),
            pl.BlockSpec((1, d), lambda i, j: (0, 0)),
            pl.BlockSpec((1, d), lambda i, j: (0, 0)),
            pl.BlockSpec((d, tf), lambda i, j: (0, j)),
            pl.BlockSpec((d, tf), lambda i, j: (0, j)),
            pl.BlockSpec((tf, d), lambda i, j: (j, 0)),
        ],
        out_specs=pl.BlockSpec((tm, d), lambda i, j: (i, 0)),
        out_shape=jax.ShapeDtypeStruct((t, d), F32),
        scratch_shapes=[pltpu.VMEM((tm, d), BF16)],
        compiler_params=_params("parallel", "arbitrary"),
        name="ffn",
    )(h, g_pre.reshape(1, d), g_post.reshape(1, d), w_gate, w_up, w_down)


PROJ_FOX_STEPS = 3
PROJ_MAIN_STEPS = 6


def _proj_in_kernel(x_ref, g_ref, wa_ref, wr_ref, wfa_ref, o_ref, og_ref, of_ref, xn_ref, *, q_scale):
    j = pl.program_id(1)
    last = pl.num_programs(1) - 1

    def store_heads(dst_ref, y):
        for hh in range(dst_ref.shape[0]):
            dst_ref[hh] = y[:, hh * HEAD_DIM:(hh + 1) * HEAD_DIM].astype(dst_ref.dtype)

    @pl.when(j == 0)
    def _():
        xn_ref[...] = _rms(x_ref[...], g_ref[...]).astype(BF16)
        store_heads(o_ref, _dot(xn_ref[...], wa_ref[...]) * q_scale)

    @pl.when((j > 0) & (j < PROJ_FOX_STEPS))
    def _():
        store_heads(o_ref, _dot(xn_ref[...], wa_ref[...]))

    @pl.when((j >= PROJ_FOX_STEPS) & (j < PROJ_MAIN_STEPS))
    def _():
        store_heads(o_ref, _dot(xn_ref[...], wr_ref[...]))

    @pl.when((j >= PROJ_MAIN_STEPS) & (j < last))
    def _():
        og_ref[...] = _sigmoid(_dot(xn_ref[...], wr_ref[...])).astype(og_ref.dtype)

    @pl.when(j == last)
    def _():
        nh = wr_ref.shape[1] // HEAD_DIM
        store_heads(of_ref.at[:nh], _dot(xn_ref[...], wr_ref[...]))
        of_ref[nh] = _dot(xn_ref[...], wfa_ref[...])


def _proj_in(h, g, w_fox, w_rest, *, tm, width, gate_width, q_scale):
    t, d = h.shape
    nh = width // HEAD_DIM
    assert w_fox.shape[1] >= PROJ_FOX_STEPS * width + HEAD_DIM
    assert w_rest.shape[1] == 4 * width + gate_width and gate_width % width == 0
    n_gate = gate_width // width
    last = PROJ_MAIN_STEPS + n_gate

    def rest_col(i, j):
        return (0, jnp.where(j <= 3, 0, jnp.where(j == last, 1, j - 2)))

    def gate_step(j):
        return jnp.clip(j - PROJ_MAIN_STEPS, 0, n_gate - 1)

    return pl.pallas_call(
        functools.partial(_proj_in_kernel, q_scale=q_scale),
        grid=(t // tm, last + 1),
        in_specs=[
            pl.BlockSpec((tm, d), lambda i, j: (i, 0)),
            pl.BlockSpec((1, d), lambda i, j: (0, 0)),
            pl.BlockSpec((d, width), lambda i, j: (0, jnp.minimum(j, PROJ_FOX_STEPS - 1))),
            pl.BlockSpec((d, width), rest_col),
            pl.BlockSpec((d, HEAD_DIM), lambda i, j: (0, PROJ_FOX_STEPS * nh)),
        ],
        out_specs=[
            pl.BlockSpec((nh, tm, HEAD_DIM), lambda i, j: (jnp.minimum(j, PROJ_MAIN_STEPS - 1), i, 0)),
            pl.BlockSpec((tm, width), lambda i, j: (i, gate_step(j))),
            pl.BlockSpec((nh + 1, tm, HEAD_DIM), lambda i, j: (0, i, 0)),
        ],
        out_shape=[jax.ShapeDtypeStruct((PROJ_MAIN_STEPS * nh, t, HEAD_DIM), BF16),
                   jax.ShapeDtypeStruct((t, gate_width), BF16),
                   jax.ShapeDtypeStruct((nh + 1, t, HEAD_DIM), F32)],
        scratch_shapes=[pltpu.VMEM((tm, d), BF16)],
        compiler_params=_params("parallel", "arbitrary"),
        name="proj_in",
    )(h, g.reshape(1, d), w_fox, w_rest, w_fox)


def _fox_gate_kernel(f_ref, b_ref, o_ref, carry_ref, *, ts):
    @pl.when(pl.program_id(1) == 0)
    def _():
        carry_ref[...] = jnp.zeros_like(carry_ref)

    z = f_ref[...].T[:NUM_HEADS, :] + b_ref[...]
    lf = (jnp.minimum(z, 0.0) - jnp.log1p(jnp.exp(-jnp.abs(z)))) * LOG2E
    src = lax.broadcasted_iota(jnp.int32, (ts, ts), 0)
    dst = lax.broadcasted_iota(jnp.int32, (ts, ts), 1)
    tri = jnp.where(src <= dst, 1.0, 0.0).astype(BF16)
    hi, mid, lo = _split3(lf)
    c = _dot(hi, tri) + _dot(mid, tri) + _dot(lo, tri) + carry_ref[:, :1]
    o_ref[...] = c
    carry_ref[...] = jnp.broadcast_to(c[:, ts - 1:ts], carry_ref.shape)


def _fox_gate(f4, slab, bias, *, ts):
    _, b, s, _ = f4.shape
    nh = NUM_HEADS
    return pl.pallas_call(
        functools.partial(_fox_gate_kernel, ts=ts),
        grid=(b, s // ts),
        in_specs=[
            pl.BlockSpec((None, None, ts, HEAD_DIM), lambda i, j: (slab, i, j, 0)),
            pl.BlockSpec((nh, 1), lambda i, j: (0, 0)),
        ],
        out_specs=pl.BlockSpec((None, nh, ts), lambda i, j: (i, 0, j)),
        out_shape=jax.ShapeDtypeStruct((b, nh, s), F32),
        scratch_shapes=[pltpu.VMEM((nh, HEAD_DIM), F32)],
        compiler_params=_params("parallel", "arbitrary"),
        name="fox_gate",
    )(f4, bias.reshape(nh, 1))


def _fox_attn_kernel(q_ref, k_ref, v_ref, c_ref, o_ref, m_ref, acc_ref, vone_ref, *, tq):
    heads = q_ref.shape[0]
    nq = q_ref.shape[1] // tq
    reps = tq // HEAD_DIM
    dh = v_ref.shape[2]

    vone_ref[:, :, :dh] = v_ref[...]
    vone_ref[:, :, dh:] = jnp.ones_like(v_ref)
    m_ref[...] = jnp.full_like(m_ref, NEG_BIG)
    acc_ref[...] = jnp.zeros_like(acc_ref)

    def step(g, qi, ki):
        rows = slice(qi * tq, (qi + 1) * tq)
        cols = slice(ki * tq, (ki + 1) * tq)
        c0 = jnp.max(c_ref[g, :, rows], axis=-1, keepdims=True)
        s = _dot_nt(q_ref[g, rows, :], k_ref[g, cols, :]) + (c0 - c_ref[g, :, cols])
        if qi == ki:
            row = lax.broadcasted_iota(jnp.int32, (tq, tq), 0)
            col = lax.broadcasted_iota(jnp.int32, (tq, tq), 1)
            s = jnp.where(col <= row, s, NEG_BIG)
        m_prev = m_ref[g, rows, :]
        m_next = jnp.maximum(m_prev, jnp.max(s.astype(BF16), axis=-1, keepdims=True).astype(F32))
        p = jnp.exp2(s - jnp.concatenate([m_next] * reps, axis=1))
        alpha = jnp.exp2(m_prev - m_next)
        acc_ref[g, rows, :] = (jnp.concatenate([alpha, alpha], axis=1) * acc_ref[g, rows, :]
                               + _dot(p.astype(BF16), vone_ref[g, cols, :]))
        m_ref[g, rows, :] = m_next

    for diag in range(nq):
        for qi in range(diag, nq):
            for g in range(heads):
                step(g, qi, qi - diag)
    for g in range(heads):
        o_ref[:, g * dh:(g + 1) * dh] = (acc_ref[g, :, :dh] / acc_ref[g, :, dh:]).astype(o_ref.dtype)


def _fox_attn(main4, c4, passengers, *, tq):
    _, b, s, _ = main4.shape
    nh, dh, g = NUM_HEADS, HEAD_DIM, ATTN_HEADS_PER_STEP
    ng = nh // g
    p_in, p_out, p_shapes = _passenger_specs(passengers, b * ng, lambda i, h: i * ng + h)
    return pl.pallas_call(
        _with_passengers(functools.partial(_fox_attn_kernel, tq=tq), 4, 1, len(passengers)),
        grid=(b, ng),
        in_specs=[
            pl.BlockSpec((g, None, s, dh), lambda i, h: (h, i, 0, 0)),
            pl.BlockSpec((g, None, s, dh), lambda i, h: (ng + h, i, 0, 0)),
            pl.BlockSpec((g, None, s, dh), lambda i, h: (2 * ng + h, i, 0, 0)),
            pl.BlockSpec((None, g, 1, s), lambda i, h: (i, h, 0, 0)),
        ] + p_in,
        out_specs=[pl.BlockSpec((None, s, g * dh), lambda i, h: (i, 0, h))] + p_out,
        out_shape=[jax.ShapeDtypeStruct((b, s, nh * dh), BF16)] + p_shapes,
        scratch_shapes=[pltpu.VMEM((g, s, dh), F32), pltpu.VMEM((g, s, 2 * dh), F32),
                        pltpu.VMEM((g, s, 2 * dh), BF16)],
        compiler_params=_params("parallel", "parallel"),
        name="fox_attn",
    )(main4, main4, main4, c4, *passengers)


def _hgrn_kernel(q_ref, f_ref, i_ref, g_ref, lbl_ref, ng_ref, o_ref,
                 st_ref, qe_ref, intra_ref, u_ref, dec_ref, a_ref, v_ref, *, lc):
    cs, sb, blk = HGRN_CHUNK, HGRN_SUB, HGRN_BLOCK
    half = sb // 2
    cpb = blk // cs

    @pl.when(pl.program_id(2) == 0)
    def _():
        st_ref[...] = jnp.zeros_like(st_ref)

    logits = lbl_ref[...]
    e = jnp.exp(logits - jnp.max(logits, axis=0, keepdims=True))
    lb = e[0:1, :] / jnp.sum(e, axis=0, keepdims=True)
    ng = ng_ref[...]

    src = lax.broadcasted_iota(jnp.int32, (blk, blk), 1)
    dst = lax.broadcasted_iota(jnp.int32, (blk, blk), 0)
    same_chunk = (src // cs) == (dst // cs)
    tri = jnp.where(same_chunk & (src <= dst), 1.0, 0.0).astype(BF16)
    half_row = lax.broadcasted_iota(jnp.int32, (half, HEAD_DIM), 0)
    blk_col = lax.broadcasted_iota(jnp.int32, (sb, cs), 1)
    chunk_row = lax.broadcasted_iota(jnp.int32, (cs, cs), 0)
    chunk_col = lax.broadcasted_iota(jnp.int32, (cs, cs), 1)

    def prep(rows):
        q = _silu(q_ref[rows, :].astype(F32))
        f = lb + (1.0 - lb) * _sigmoid(f_ref[rows, :])
        kk = 1.0 - f
        hi, mid, lo = _split3(jnp.log2(f))
        cum = _dot(tri, hi) + _dot(tri, mid) + _dot(tri, lo)
        lasts = [cum[c0 + cs - 1:c0 + cs, :] for c0 in range(0, blk, cs)]
        return q, kk, cum, lasts

    def state_increment(ci, kk, cum, lasts, v):
        c0 = ci * cs
        kdec = (kk[c0:c0 + cs, :] * jnp.exp2(lasts[ci] - cum[c0:c0 + cs, :])).astype(BF16)
        return _dot(v[c0:c0 + cs, :].T.astype(BF16), kdec)

    def finish(rows, outs):
        o = _rms(jnp.concatenate(outs, axis=0), ng) * _silu(g_ref[rows, :].astype(F32))
        o_ref[rows, :] = o.astype(o_ref.dtype)

    group = min(HGRN_GROUP, lc // blk)

    def precompute(gi, deepest):
        blocks = [gi * group + k for k in range(group)]
        rows = [pl.ds(pl.multiple_of(bi * blk, blk), blk) for bi in blocks]
        stage1 = [prep(r) for r in rows]
        stage2 = []
        for bi, r, (q, kk, cum, lasts) in zip(blocks, rows, stage1):
            v = i_ref[r, :]
            v32 = v.astype(F32)
            qe = (q * jnp.exp2(cum)).astype(BF16)
            kinv32 = kk * jnp.exp2(-cum)
            kinv = kinv32.astype(BF16)
            qe_ref[r, :] = qe
            scores = [_dot_nt(qe[c0:c0 + cs, :], kinv[c0:c0 + cs, :]) for c0 in range(0, blk, cs)]
            for ci in range(cpb):
                c0 = ci * cs
                dec = jnp.exp2(lasts[ci])
                kdec = (kinv32[c0:c0 + cs, :] * dec).astype(BF16)
                u_ref[bi * cpb + ci] = _dot(v32[c0:c0 + cs, :].T.astype(BF16), kdec)
                dec_ref[pl.ds(pl.multiple_of((bi * cpb + ci) * SUBLANES, SUBLANES), SUBLANES), :] = (
                    jnp.broadcast_to(dec, (SUBLANES, HEAD_DIM)))
                deepest = jnp.maximum(deepest, -lasts[ci])
            stage2.append((v, scores))
        for r, (v, scores) in zip(rows, stage2):
            intra = [_dot(jnp.where(chunk_col <= chunk_row, sc, 0.0).astype(BF16), v[ci * cs:(ci + 1) * cs, :])
                     for ci, sc in enumerate(scores)]
            intra_ref[r, :] = jnp.concatenate(intra, axis=0)
        return deepest

    deepest = lax.fori_loop(0, lc // (blk * group), precompute, jnp.zeros((1, HEAD_DIM), F32))
    depth = jnp.max(deepest)

    @pl.when(depth <= HGRN_SAFE_LOG2)
    def _():
        def recur(bi, carry):
            base = pl.multiple_of(bi * blk, blk)
            st = st_ref[...]
            outs = []
            for ci in range(cpb):
                rows_c = pl.ds(base + ci * cs, cs)
                dec = dec_ref[pl.ds(pl.multiple_of((bi * cpb + ci) * SUBLANES, SUBLANES), SUBLANES), :]
                outs.append(_dot_nt(qe_ref[rows_c, :], st.astype(BF16)) + intra_ref[rows_c, :])
                st = st * dec[0:1, :] + u_ref[bi * cpb + ci]
            st_ref[...] = st
            finish(pl.ds(base, blk), outs)
            return carry

        lax.fori_loop(0, lc // blk, recur, 0, unroll=True)

    @pl.when(jnp.logical_not(depth <= HGRN_SAFE_LOG2))
    def _():
        def general(bi, carry):
            rows = pl.ds(pl.multiple_of(bi * blk, blk), blk)
            q, kk, cum, lasts = prep(rows)
            v = i_ref[rows, :]
            v32 = v.astype(F32)
            a_ref[...] = cum - jnp.log2(kk)
            v_ref[...] = v32
            st = st_ref[...]
            outs = []
            for ci in range(cpb):
                c0 = ci * cs
                cum_c = cum[c0:c0 + cs, :]
                q_c = q[c0:c0 + cs, :]
                kk_c = kk[c0:c0 + cs, :]
                inter = _dot_nt((q_c * jnp.exp2(cum_c)).astype(BF16), st.astype(BF16))
                st = st * jnp.exp2(lasts[ci]) + state_increment(ci, kk, cum, lasts, v32)

                score_rows = [jnp.zeros((sb, cs), F32)]
                for si in range(1, cs // sb):
                    r0 = si * sb
                    ref = cum_c[r0 - 1:r0, :]
                    qt = q_c[r0:r0 + sb, :] * jnp.exp2(cum_c[r0:r0 + sb, :] - ref)
                    kt = kk_c * jnp.exp2(jnp.minimum(ref - cum_c, 0.0))
                    sc = _dot_nt(qt.astype(BF16), kt.astype(BF16))
                    score_rows.append(jnp.where(blk_col < r0, sc, 0.0))
                off = _dot(jnp.concatenate(score_rows, axis=0).astype(BF16), v[c0:c0 + cs, :])

                diag_rows = []
                for si in range(cs // sb):
                    r0 = c0 + si * sb
                    q_lo, q_hi = q[r0:r0 + half, :], q[r0 + half:r0 + sb, :]
                    c_lo, c_hi = cum[r0:r0 + half, :], cum[r0 + half:r0 + sb, :]
                    acc_lo = jnp.zeros((half, HEAD_DIM), F32)
                    acc_hi = jnp.zeros((half, HEAD_DIM), F32)
                    for ti in range(sb):
                        a_s = a_ref[r0 + ti:r0 + ti + 1, :]
                        v_s = v_ref[r0 + ti:r0 + ti + 1, :]
                        if ti < half:
                            d = c_lo - a_s
                            if ti > 0:
                                d = jnp.where(half_row >= ti, d, NEG_BIG)
                            acc_lo = acc_lo + jnp.sum(q_lo * jnp.exp2(d), axis=-1, keepdims=True) * v_s
                            d = c_hi - a_s
                        else:
                            d = c_hi - a_s
                            if ti > half:
                                d = jnp.where(half_row >= ti - half, d, NEG_BIG)
                        acc_hi = acc_hi + jnp.sum(q_hi * jnp.exp2(d), axis=-1, keepdims=True) * v_s
                    diag_rows += [acc_lo, acc_hi]
                outs.append(inter + off + jnp.concatenate(diag_rows, axis=0))
            st_ref[...] = st
            finish(rows, outs)
            return carry

        lax.fori_loop(0, lc // blk, general, 0)


def _hgrn(main4, f4, slabs, lb_logits, norm_g, passengers, *, lc):
    _, b, s, _ = main4.shape
    nh, dh = NUM_HEADS, HEAD_DIM
    nl = lb_logits.shape[0]
    assert lc % (HGRN_BLOCK * min(HGRN_GROUP, lc // HGRN_BLOCK)) == 0
    q_blk, i_blk, g_blk = slabs

    def col(first):
        return pl.BlockSpec((None, None, lc, dh), lambda i, h, j: (first + h, i, j, 0))

    n_lc = s // lc
    p_in, p_out, p_shapes = _passenger_specs(passengers, b * nh * n_lc,
                                             lambda i, h, j: (i * nh + h) * n_lc + j)
    return pl.pallas_call(
        _with_passengers(functools.partial(_hgrn_kernel, lc=lc), 6, 1, len(passengers)),
        grid=(b, nh, n_lc),
        in_specs=[
            col(q_blk), col(0), col(i_blk), col(g_blk),
            pl.BlockSpec((nl, dh), lambda i, h, j: (0, h)),
            pl.BlockSpec((1, dh), lambda i, h, j: (0, 0)),
        ] + p_in,
        out_specs=[pl.BlockSpec((None, lc, dh), lambda i, h, j: (i, j, h))] + p_out,
        out_shape=[jax.ShapeDtypeStruct((b, s, nh * dh), BF16)] + p_shapes,
        scratch_shapes=[
            pltpu.VMEM((dh, dh), F32),
            pltpu.VMEM((lc, dh), BF16),
            pltpu.VMEM((lc, dh), F32),
            pltpu.VMEM((lc // HGRN_CHUNK, dh, dh), F32),
            pltpu.VMEM((lc // HGRN_CHUNK * SUBLANES, dh), F32),
            pltpu.VMEM((HGRN_BLOCK, dh), F32),
            pltpu.VMEM((HGRN_BLOCK, dh), F32),
        ],
        compiler_params=_params("parallel", "parallel", "arbitrary"),
        name="hgrn",
    )(main4, f4, main4, main4, lb_logits, norm_g.reshape(1, dh), *passengers)


def _merge_kernel(h_ref, gpost_ref, ya_ref, yb_ref, ga_ref, gb_ref, wpa_ref, wpb_ref, wo_ref, o_ref):
    j = pl.program_id(1)

    def sweep_step(first, last):
        tn = wo_ref.shape[0]
        groups = [slice(c0, c0 + tn // MERGE_COL_GROUPS) for c0 in range(0, tn, tn // MERGE_COL_GROUPS)]
        stage1 = [(_dot(ya_ref[...], wpa_ref[:, c]), _dot(yb_ref[...], wpb_ref[:, c])) for c in groups]
        merged = [(ga_ref[:, c].astype(F32) * y_a + gb_ref[:, c].astype(F32) * y_b).astype(BF16)
                  for c, (y_a, y_b) in zip(groups, stage1)]
        out = _dot(merged[0], wo_ref[groups[0], :])
        for mg, c in zip(merged[1:], groups[1:]):
            out += _dot(mg, wo_ref[c, :])
        if first:
            o_ref[...] = out
        else:
            o_ref[...] += out
        if last:
            _postnorm_residual_to(o_ref, h_ref, o_ref, gpost_ref, 1.0)

    n_steps = pl.num_programs(1)
    pl.when(j == 0)(functools.partial(sweep_step, True, False))
    pl.when((j > 0) & (j < n_steps - 1))(functools.partial(sweep_step, False, False))
    pl.when(j == n_steps - 1)(functools.partial(sweep_step, False, True))


def _merge(h, g_post, y_a, y_b, gates, w_pa, w_pb, w_o, *, tm, tn):
    t, d = h.shape
    wa = y_a.shape[1]
    wb = y_b.shape[1]
    n_j = d // tn

    def residual_rows(i, j):
        return (jnp.where(j == n_j - 1, i, jnp.maximum(i - 1, 0)), 0)

    return pl.pallas_call(
        _merge_kernel,
        grid=(t // tm, n_j),
        in_specs=[
            pl.BlockSpec((tm, d), residual_rows),
            pl.BlockSpec((1, d), lambda i, j: (0, 0)),
            pl.BlockSpec((tm, wa), lambda i, j: (i, 0)),
            pl.BlockSpec((tm, wb), lambda i, j: (i, 0)),
            pl.BlockSpec((tm, tn), lambda i, j: (i, j)),
            pl.BlockSpec((tm, tn), lambda i, j: (i, d // tn + j)),
            pl.BlockSpec((wa, tn), lambda i, j: (0, j)),
            pl.BlockSpec((wb, tn), lambda i, j: (0, j)),
            pl.BlockSpec((tn, d), lambda i, j: (j, 0)),
        ],
        out_specs=pl.BlockSpec((tm, d), lambda i, j: (i, 0)),
        out_shape=jax.ShapeDtypeStruct((t, d), F32),
        compiler_params=_params("parallel", "arbitrary"),
        name="merge",
    )(h, g_post.reshape(1, d), y_a, y_b, gates, gates, w_pa, w_pb, w_o)


def _ple_kernel(h_ref, gpre_ref, gpost_ref, p_ref, wg_ref, wp_ref, o_ref):
    tm = h_ref.shape[0]
    groups = [slice(r0, r0 + tm // PLE_ROW_GROUPS) for r0 in range(0, tm, tm // PLE_ROW_GROUPS)]
    emb = [_dot(p_ref[r, :].astype(BF16), wp_ref[...]) for r in groups]
    gates = [_dot(_rms(h_ref[r, :], gpre_ref[...]).astype(BF16), wg_ref[...]) for r in groups]
    for r, e, g in zip(groups, emb, gates):
        o_ref[r, :] = h_ref[r, :] + _rms(jax.nn.sigmoid(g) * e, gpost_ref[...])


def _ple(h, g_pre, g_post, p2, w_g, w_p, *, tm):
    t, d = h.shape
    dp = p2.shape[1]
    return pl.pallas_call(
        _ple_kernel,
        grid=(t // tm,),
        in_specs=[
            pl.BlockSpec((tm, d), lambda i: (i, 0)),
            pl.BlockSpec((1, d), lambda i: (0, 0)),
            pl.BlockSpec((1, d), lambda i: (0, 0)),
            pl.BlockSpec((tm, dp), lambda i: (i, 0)),
            pl.BlockSpec((d, d), lambda i: (0, 0)),
            pl.BlockSpec((dp, d), lambda i: (0, 0)),
        ],
        out_specs=pl.BlockSpec((tm, d), lambda i: (i, 0)),
        out_shape=jax.ShapeDtypeStruct((t, d), F32),
        compiler_params=_params("parallel"),
        name="ple",
    )(h, g_pre.reshape(1, d), g_post.reshape(1, d), p2, w_g, w_p)


def _tile(n, want):
    t = min(n, want)
    while n % t:
        t //= 2
    return t


def kernel(x, p, ffn1_pre_g, ffn1_post_g, ffn1_w_gate, ffn1_w_up, ffn1_w_down, mix_pre_g, mix_post_g, mix_w_in, fox_f_bias, hgrn_lb_logits, hgrn_norm_g, mix_w_proj_fox, mix_w_proj_hgrn, mix_w_out, ffn2_pre_g, ffn2_post_g, ffn2_w_gate, ffn2_w_up, ffn2_w_down, ple_pre_g, ple_post_g, ple_w_gate, ple_w_proj):
    b, s, d = x.shape
    t = b * s
    depth = ffn1_pre_g.shape[0]
    assert depth == 1, "the HGRN2 lower bound is evaluated for a single layer"
    nh, dh = NUM_HEADS, HEAD_DIM
    width = nh * dh
    assert mix_w_in.shape[-1] == 3 * width + nh + 4 * width + 2 * d

    tm_big = _tile(t, 1024)
    h = x.reshape(t, d)
    for i in range(depth):
        h = _ffn(h, ffn1_pre_g[i], ffn1_post_g[i], ffn1_w_gate[i].astype(BF16),
                 ffn1_w_up[i].astype(BF16), ffn1_w_down[i].astype(BF16),
                 tm=tm_big, tf=_tile(ffn1_w_gate.shape[-1], 512))

        w_in = mix_w_in[i]
        o_f = 3 * width
        o_b = o_f + nh
        w_bf = w_in.astype(BF16)
        w_rest = w_bf[:, o_b:]

        main, gates, f = _proj_in(h, mix_pre_g[i], w_bf, w_rest, tm=tm_big, width=width,
                                  gate_width=2 * d, q_scale=dh ** -0.5 * LOG2E)
        main4 = main.reshape(6 * nh, b, s, dh)
        f4 = f.reshape(nh + 1, b, s, dh)
        c = _fox_gate(f4, nh, fox_f_bias[i], ts=_tile(s, 512))
        y_a, w2_gate, w2_up = _fox_attn(main4, c.reshape(b, nh, 1, s),
                                        [ffn2_w_gate[i], ffn2_w_up[i]], tq=_tile(s, 512))
        y_b, w2_down, w_pa, w_pb, w_o, w_pg = _hgrn(
            main4, f4, (3 * nh, 4 * nh, 5 * nh), hgrn_lb_logits, hgrn_norm_g[i],
            [ffn2_w_down[i], mix_w_proj_fox[i], mix_w_proj_hgrn[i], mix_w_out[i], ple_w_gate[i]],
            lc=_tile(s, 2048))
        h = _merge(h, mix_post_g[i], y_a.reshape(t, width), y_b.reshape(t, width), gates,
                   w_pa, w_pb, w_o, tm=tm_big, tn=_tile(d, 512))

        h = _ffn(h, ffn2_pre_g[i], ffn2_post_g[i], w2_gate, w2_up, w2_down,
                 tm=tm_big, tf=_tile(ffn2_w_gate.shape[-1], 512))
        h = _ple(h, ple_pre_g[i], ple_post_g[i], p[i].reshape(t, -1),
                 w_pg, ple_w_proj[i].astype(BF16), tm=_tile(t, 512))
    return h.reshape(b, s, d)
```

```python
import functools
import math

import jax
import jax.numpy as jnp
from jax import lax
from jax.experimental import pallas as pl
from jax.experimental.pallas import tpu as pltpu

NORM_EPS = 1e-6
MACARON_SCALE = 0.5
LOG2E = math.log2(math.e)
HEAD_DIM = 128
NUM_HEADS = 8
HGRN_CHUNK = 64
HGRN_SUB = 16
HGRN_BLOCK = 256
HGRN_GROUP = 8
HGRN_SAFE_LOG2 = 100.0
FFN_COL_GROUPS = 2
MERGE_COL_GROUPS = 2
PLE_ROW_GROUPS = 2
NORM_ROWS = 16
ATTN_HEADS_PER_STEP = 4
SUBLANES = 8
BF16_ROWS = 16
NEG_BIG = -1e30

VMEM_LIMIT_BYTES = 60 * 1024 * 1024

BF16 = jnp.bfloat16
F32 = jnp.float32


def _params(*semantics):
    return pltpu.CompilerParams(dimension_semantics=semantics,
                                vmem_limit_bytes=VMEM_LIMIT_BYTES)


def _rms(x, g):
    ms = jnp.mean(x * x, axis=-1, keepdims=True)
    return x * lax.rsqrt(ms + NORM_EPS) * g


def _postnorm_residual_to(o_ref, x_ref, y_ref, g_ref, scale):
    g = g_ref[...] if scale == 1.0 else scale * g_ref[...]
    for r0 in range(0, x_ref.shape[0], NORM_ROWS):
        rows = slice(r0, r0 + NORM_ROWS)
        o_ref[rows, :] = x_ref[rows, :] + _rms(y_ref[rows, :], g)


def _sigmoid(x):
    return 0.5 + 0.5 * jnp.tanh(0.5 * x)


def _silu(x):
    h = 0.5 * x
    return h + h * jnp.tanh(h)


def _dot(a, b):
    return jnp.dot(a, b, preferred_element_type=F32)


def _dot_nt(a, b):
    return lax.dot_general(a, b, (((1,), (1,)), ((), ())), preferred_element_type=F32)


def _split3(x):
    hi = x.astype(BF16)
    r = x - hi.astype(F32)
    mid = r.astype(BF16)
    lo = (r - mid.astype(F32)).astype(BF16)
    return hi, mid, lo


def _with_passengers(body, n_in, n_out, n_pass):
    def kernel(*refs):
        ins, refs = refs[:n_in], refs[n_in:]
        pass_in, refs = refs[:n_pass], refs[n_pass:]
        outs, refs = refs[:n_out], refs[n_out:]
        pass_out, scratch = refs[:n_pass], refs[n_pass:]
        for src, dst in zip(pass_in, pass_out):
            dst[...] = src[...].astype(dst.dtype)
        body(*ins, *outs, *scratch)

    return kernel


def _passenger_specs(weights, n_steps, step_of):
    in_specs, out_specs, out_shapes = [], [], []
    for w in weights:
        r, c = w.shape
        share = 1
        while (r * share) % n_steps or (r * share // n_steps) % BF16_ROWS:
            share *= 2
        spec = pl.BlockSpec((r * share // n_steps, c), lambda *g, share=share: (step_of(*g) // share, 0))
        in_specs.append(spec)
        out_specs.append(spec)
        out_shapes.append(jax.ShapeDtypeStruct((r, c), BF16))
    return in_specs, out_specs, out_shapes


def _ffn_kernel(x_ref, gpre_ref, gpost_ref, wg_ref, wu_ref, wd_ref, o_ref, xn_ref):
    j = pl.program_id(1)

    def sweep_step(first, last):
        if first:
            xn_ref[...] = _rms(x_ref[...], gpre_ref[...]).astype(BF16)
        xn = xn_ref[...]
        tf = wg_ref.shape[1]
        groups = [slice(c0, c0 + tf // FFN_COL_GROUPS) for c0 in range(0, tf, tf // FFN_COL_GROUPS)]
        gu = [(_dot(xn, wg_ref[:, c]), _dot(xn, wu_ref[:, c])) for c in groups]
        acts = [(g * jax.nn.sigmoid(g) * u).astype(BF16) for g, u in gu]
        down = _dot(acts[0], wd_ref[groups[0], :])
        for a, c in zip(acts[1:], groups[1:]):
            down += _dot(a, wd_ref[c, :])
        if first:
            o_ref[...] = down
        else:
            o_ref[...] += down
        if last:
            _postnorm_residual_to(o_ref, x_ref, o_ref, gpost_ref, MACARON_SCALE)

    n_steps = pl.num_programs(1)
    pl.when(j == 0)(functools.partial(sweep_step, True, False))
    pl.when((j > 0) & (j < n_steps - 1))(functools.partial(sweep_step, False, False))
    pl.when(j == n_steps - 1)(functools.partial(sweep_step, False, True))


def _ffn(h, g_pre, g_post, w_gate, w_up, w_down, *, tm, tf):
    t, d = h.shape
    f = w_gate.shape[1]
    return pl.pallas_call(
        _ffn_kernel,
        grid=(t // tm, f // tf),
        in_specs=[
            pl.BlockSpec((tm, d), lambda i, j: (i, 0)),
            pl.BlockSpec((1, d), lambda i, j: (0, 0)),
            pl.BlockSpec((1, d), lambda i, j: (0, 0)),
            pl.BlockSpec((d, tf), lambda i, j: (0, j)),
            pl.BlockSpec((d, tf), lambda i, j: (0, j)),
            pl.BlockSpec((tf, d), lambda i, j: (j, 0)),
        ],
        out_specs=pl.BlockSpec((tm, d), lambda i, j: (i, 0)),
        out_shape=jax.ShapeDtypeStruct((t, d), F32),
        scratch_shapes=[pltpu.VMEM((tm, d), BF16)],
        compiler_params=_params("parallel", "arbitrary"),
        name="ffn",
    )(h, g_pre.reshape(1, d), g_post.reshape(1, d), w_gate, w_up, w_down)


PROJ_FOX_STEPS = 3
PROJ_MAIN_STEPS = 6


def _proj_in_kernel(x_ref, g_ref, wa_ref, wr_ref, wfa_ref, o_ref, og_ref, of_ref, xn_ref, *, q_scale):
    j = pl.program_id(1)
    last = pl.num_programs(1) - 1

    def store_heads(dst_ref, y):
        for hh in range(dst_ref.shape[0]):
            dst_ref[hh] = y[:, hh * HEAD_DIM:(hh + 1) * HEAD_DIM].astype(dst_ref.dtype)

    @pl.when(j == 0)
    def _():
        xn_ref[...] = _rms(x_ref[...], g_ref[...]).astype(BF16)
        store_heads(o_ref, _dot(xn_ref[...], wa_ref[...]) * q_scale)

    @pl.when((j > 0) & (j < PROJ_FOX_STEPS))
    def _():
        store_heads(o_ref, _dot(xn_ref[...], wa_ref[...]))

    @pl.when((j >= PROJ_FOX_STEPS) & (j < PROJ_MAIN_STEPS))
    def _():
        store_heads(o_ref, _dot(xn_ref[...], wr_ref[...]))

    @pl.when((j >= PROJ_MAIN_STEPS) & (j < last))
    def _():
        og_ref[...] = _sigmoid(_dot(xn_ref[...], wr_ref[...])).astype(og_ref.dtype)

    @pl.when(j == last)
    def _():
        nh = wr_ref.shape[1] // HEAD_DIM
        store_heads(of_ref.at[:nh], _dot(xn_ref[...], wr_ref[...]))
        of_ref[nh] = _dot(xn_ref[...], wfa_ref[...])


def _proj_in(h, g, w_fox, w_rest, *, tm, width, gate_width, q_scale):
    t, d = h.shape
    nh = width // HEAD_DIM
    assert w_fox.shape[1] >= PROJ_FOX_STEPS * width + HEAD_DIM
    assert w_rest.shape[1] == 4 * width + gate_width and gate_width % width == 0
    n_gate = gate_width // width
    last = PROJ_MAIN_STEPS + n_gate

    def rest_col(i, j):
        return (0, jnp.where(j <= 3, 0, jnp.where(j == last, 1, j - 2)))

    def gate_step(j):
        return jnp.clip(j - PROJ_MAIN_STEPS, 0, n_gate - 1)

    return pl.pallas_call(
        functools.partial(_proj_in_kernel, q_scale=q_scale),
        grid=(t // tm, last + 1),
        in_specs=[
            pl.BlockSpec((tm, d), lambda i, j: (i, 0)),
            pl.BlockSpec((1, d), lambda i, j: (0, 0)),
            pl.BlockSpec((d, width), lambda i, j: (0, jnp.minimum(j, PROJ_FOX_STEPS - 1))),
            pl.BlockSpec((d, width), rest_col),
            pl.BlockSpec((d, HEAD_DIM), lambda i, j: (0, PROJ_FOX_STEPS * nh)),
        ],
        out_specs=[
            pl.BlockSpec((nh, tm, HEAD_DIM), lambda i, j: (jnp.minimum(j, PROJ_MAIN_STEPS - 1), i, 0)),
            pl.BlockSpec((tm, width), lambda i, j: (i, gate_step(j))),
            pl.BlockSpec((nh + 1, tm, HEAD_DIM), lambda i, j: (0, i, 0)),
        ],
        out_shape=[jax.ShapeDtypeStruct((PROJ_MAIN_STEPS * nh, t, HEAD_DIM), BF16),
                   jax.ShapeDtypeStruct((t, gate_width), BF16),
                   jax.ShapeDtypeStruct((nh + 1, t, HEAD_DIM), F32)],
        scratch_shapes=[pltpu.VMEM((tm, d), BF16)],
        compiler_params=_params("parallel", "arbitrary"),
        name="proj_in",
    )(h, g.reshape(1, d), w_fox, w_rest, w_fox)


def _fox_gate_kernel(f_ref, b_ref, o_ref, carry_ref, *, ts):
    @pl.when(pl.program_id(1) == 0)
    def _():
        carry_ref[...] = jnp.zeros_like(carry_ref)

    z = f_ref[...].T[:NUM_HEADS, :] + b_ref[...]
    lf = (jnp.minimum(z, 0.0) - jnp.log1p(jnp.exp(-jnp.abs(z)))) * LOG2E
    src = lax.broadcasted_iota(jnp.int32, (ts, ts), 0)
    dst = lax.broadcasted_iota(jnp.int32, (ts, ts), 1)
    tri = jnp.where(src <= dst, 1.0, 0.0).astype(BF16)
    hi, mid, lo = _split3(lf)
    c = _dot(hi, tri) + _dot(mid, tri) + _dot(lo, tri) + carry_ref[:, :1]
    o_ref[...] = c
    carry_ref[...] = jnp.broadcast_to(c[:, ts - 1:ts], carry_ref.shape)


def _fox_gate(f4, slab, bias, *, ts):
    _, b, s, _ = f4.shape
    nh = NUM_HEADS
    return pl.pallas_call(
        functools.partial(_fox_gate_kernel, ts=ts),
        grid=(b, s // ts),
        in_specs=[
            pl.BlockSpec((None, None, ts, HEAD_DIM), lambda i, j: (slab, i, j, 0)),
            pl.BlockSpec((nh, 1), lambda i, j: (0, 0)),
        ],
        out_specs=pl.BlockSpec((None, nh, ts), lambda i, j: (i, 0, j)),
        out_shape=jax.ShapeDtypeStruct((b, nh, s), F32),
        scratch_shapes=[pltpu.VMEM((nh, HEAD_DIM), F32)],
        compiler_params=_params("parallel", "arbitrary"),
        name="fox_gate",
    )(f4, bias.reshape(nh, 1))


def _fox_attn_kernel(q_ref, k_ref, v_ref, c_ref, o_ref, m_ref, acc_ref, vone_ref, *, tq):
    heads = q_ref.shape[0]
    nq = q_ref.shape[1] // tq
    reps = tq // HEAD_DIM
    dh = v_ref.shape[2]

    vone_ref[:, :, :dh] = v_ref[...]
    vone_ref[:, :, dh:] = jnp.ones_like(v_ref)
    m_ref[...] = jnp.full_like(m_ref, NEG_BIG)
    acc_ref[...] = jnp.zeros_like(acc_ref)

    def step(g, qi, ki):
        rows = slice(qi * tq, (qi + 1) * tq)
        cols = slice(ki * tq, (ki + 1) * tq)
        c0 = jnp.max(c_ref[g, :, rows], axis=-1, keepdims=True)
        s = _dot_nt(q_ref[g, rows, :], k_ref[g, cols, :]) + (c0 - c_ref[g, :, cols])
        if qi == ki:
            row = lax.broadcasted_iota(jnp.int32, (tq, tq), 0)
            col = lax.broadcasted_iota(jnp.int32, (tq, tq), 1)
            s = jnp.where(col <= row, s, NEG_BIG)
        m_prev = m_ref[g, rows, :]
        m_next = jnp.maximum(m_prev, jnp.max(s.astype(BF16), axis=-1, keepdims=True).astype(F32))
        p = jnp.exp2(s - jnp.concatenate([m_next] * reps, axis=1))
        alpha = jnp.exp2(m_prev - m_next)
        acc_ref[g, rows, :] = (jnp.concatenate([alpha, alpha], axis=1) * acc_ref[g, rows, :]
                               + _dot(p.astype(BF16), vone_ref[g, cols, :]))
        m_ref[g, rows, :] = m_next

    for diag in range(nq):
        for qi in range(diag, nq):
            for g in range(heads):
                step(g, qi, qi - diag)
    for g in range(heads):
        o_ref[:, g * dh:(g + 1) * dh] = (acc_ref[g, :, :dh] / acc_ref[g, :, dh:]).astype(o_ref.dtype)


def _fox_attn(main4, c4, passengers, *, tq):
    _, b, s, _ = main4.shape
    nh, dh, g = NUM_HEADS, HEAD_DIM, ATTN_HEADS_PER_STEP
    ng = nh // g
    p_in, p_out, p_shapes = _passenger_specs(passengers, b * ng, lambda i, h: i * ng + h)
    return pl.pallas_call(
        _with_passengers(functools.partial(_fox_attn_kernel, tq=tq), 4, 1, len(passengers)),
        grid=(b, ng),
        in_specs=[
            pl.BlockSpec((g, None, s, dh), lambda i, h: (h, i, 0, 0)),
            pl.BlockSpec((g, None, s, dh), lambda i, h: (ng + h, i, 0, 0)),
            pl.BlockSpec((g, None, s, dh), lambda i, h: (2 * ng + h, i, 0, 0)),
            pl.BlockSpec((None, g, 1, s), lambda i, h: (i, h, 0, 0)),
        ] + p_in,
        out_specs=[pl.BlockSpec((None, s, g * dh), lambda i, h: (i, 0, h))] + p_out,
        out_shape=[jax.ShapeDtypeStruct((b, s, nh * dh), BF16)] + p_shapes,
        scratch_shapes=[pltpu.VMEM((g, s, dh), F32), pltpu.VMEM((g, s, 2 * dh), F32),
                        pltpu.VMEM((g, s, 2 * dh), BF16)],
        compiler_params=_params("parallel", "parallel"),
        name="fox_attn",
    )(main4, main4, main4, c4, *passengers)


def _hgrn_kernel(q_ref, f_ref, i_ref, g_ref, lbl_ref, ng_ref, o_ref,
                 st_ref, qe_ref, intra_ref, u_ref, dec_ref, a_ref, v_ref, *, lc):
    cs, sb, blk = HGRN_CHUNK, HGRN_SUB, HGRN_BLOCK
    half = sb // 2
    cpb = blk // cs

    @pl.when(pl.program_id(2) == 0)
    def _():
        st_ref[...] = jnp.zeros_like(st_ref)

    logits = lbl_ref[...]
    e = jnp.exp(logits - jnp.max(logits, axis=0, keepdims=True))
    lb = e[0:1, :] / jnp.sum(e, axis=0, keepdims=True)
    ng = ng_ref[...]

    src = lax.broadcasted_iota(jnp.int32, (blk, blk), 1)
    dst = lax.broadcasted_iota(jnp.int32, (blk, blk), 0)
    same_chunk = (src // cs) == (dst // cs)
    tri = jnp.where(same_chunk & (src <= dst), 1.0, 0.0).astype(BF16)
    half_row = lax.broadcasted_iota(jnp.int32, (half, HEAD_DIM), 0)
    blk_col = lax.broadcasted_iota(jnp.int32, (sb, cs), 1)
    chunk_row = lax.broadcasted_iota(jnp.int32, (cs, cs), 0)
    chunk_col = lax.broadcasted_iota(jnp.int32, (cs, cs), 1)

    def prep(rows):
        q = _silu(q_ref[rows, :].astype(F32))
        f = lb + (1.0 - lb) * _sigmoid(f_ref[rows, :])
        kk = 1.0 - f
        hi, mid, lo = _split3(jnp.log2(f))
        cum = _dot(tri, hi) + _dot(tri, mid) + _dot(tri, lo)
        lasts = [cum[c0 + cs - 1:c0 + cs, :] for c0 in range(0, blk, cs)]
        return q, kk, cum, lasts

    def state_increment(ci, kk, cum, lasts, v):
        c0 = ci * cs
        kdec = (kk[c0:c0 + cs, :] * jnp.exp2(lasts[ci] - cum[c0:c0 + cs, :])).astype(BF16)
        return _dot(v[c0:c0 + cs, :].T.astype(BF16), kdec)

    def finish(rows, outs):
        o = _rms(jnp.concatenate(outs, axis=0), ng) * _silu(g_ref[rows, :].astype(F32))
        o_ref[rows, :] = o.astype(o_ref.dtype)

    group = min(HGRN_GROUP, lc // blk)

    def precompute(gi, deepest):
        blocks = [gi * group + k for k in range(group)]
        rows = [pl.ds(pl.multiple_of(bi * blk, blk), blk) for bi in blocks]
        stage1 = [prep(r) for r in rows]
        stage2 = []
        for bi, r, (q, kk, cum, lasts) in zip(blocks, rows, stage1):
            v = i_ref[r, :]
            v32 = v.astype(F32)
            qe = (q * jnp.exp2(cum)).astype(BF16)
            kinv32 = kk * jnp.exp2(-cum)
            kinv = kinv32.astype(BF16)
            qe_ref[r, :] = qe
            scores = [_dot_nt(qe[c0:c0 + cs, :], kinv[c0:c0 + cs, :]) for c0 in range(0, blk, cs)]
            for ci in range(cpb):
                c0 = ci * cs
                dec = jnp.exp2(lasts[ci])
                kdec = (kinv32[c0:c0 + cs, :] * dec).astype(BF16)
                u_ref[bi * cpb + ci] = _dot(v32[c0:c0 + cs, :].T.astype(BF16), kdec)
                dec_ref[pl.ds(pl.multiple_of((bi * cpb + ci) * SUBLANES, SUBLANES), SUBLANES), :] = (
                    jnp.broadcast_to(dec, (SUBLANES, HEAD_DIM)))
                deepest = jnp.maximum(deepest, -lasts[ci])
            stage2.append((v, scores))
        for r, (v, scores) in zip(rows, stage2):
            intra = [_dot(jnp.where(chunk_col <= chunk_row, sc, 0.0).astype(BF16), v[ci * cs:(ci + 1) * cs, :])
                     for ci, sc in enumerate(scores)]
            intra_ref[r, :] = jnp.concatenate(intra, axis=0)
        return deepest

    deepest = lax.fori_loop(0, lc // (blk * group), precompute, jnp.zeros((1, HEAD_DIM), F32))
    depth = jnp.max(deepest)

    @pl.when(depth <= HGRN_SAFE_LOG2)
    def _():
        def recur(bi, carry):
            base = pl.multiple_of(bi * blk, blk)
            st = st_ref[...]
            outs = []
            for ci in range(cpb):
                rows_c = pl.ds(base + ci * cs, cs)
                dec = dec_ref[pl.ds(pl.multiple_of((bi * cpb + ci) * SUBLANES, SUBLANES), SUBLANES), :]
                outs.append(_dot_nt(qe_ref[rows_c, :], st.astype(BF16)) + intra_ref[rows_c, :])
                st = st * dec[0:1, :] + u_ref[bi * cpb + ci]
            st_ref[...] = st
            finish(pl.ds(base, blk), outs)
            return carry

        lax.fori_loop(0, lc // blk, recur, 0, unroll=True)

    @pl.when(jnp.logical_not(depth <= HGRN_SAFE_LOG2))
    def _():
        def general(bi, carry):
            rows = pl.ds(pl.multiple_of(bi * blk, blk), blk)
            q, kk, cum, lasts = prep(rows)
            v = i_ref[rows, :]
            v32 = v.astype(F32)
            a_ref[...] = cum - jnp.log2(kk)
            v_ref[...] = v32
            st = st_ref[...]
            outs = []
            for ci in range(cpb):
                c0 = ci * cs
                cum_c = cum[c0:c0 + cs, :]
                q_c = q[c0:c0 + cs, :]
                kk_c = kk[c0:c0 + cs, :]
                inter = _dot_nt((q_c * jnp.exp2(cum_c)).astype(BF16), st.astype(BF16))
                st = st * jnp.exp2(lasts[ci]) + state_increment(ci, kk, cum, lasts, v32)

                score_rows = [jnp.zeros((sb, cs), F32)]
                for si in range(1, cs // sb):
                    r0 = si * sb
                    ref = cum_c[r0 - 1:r0, :]
                    qt = q_c[r0:r0 + sb, :] * jnp.exp2(cum_c[r0:r0 + sb, :] - ref)
                    kt = kk_c * jnp.exp2(jnp.minimum(ref - cum_c, 0.0))
                    sc = _dot_nt(qt.astype(BF16), kt.astype(BF16))
                    score_rows.append(jnp.where(blk_col < r0, sc, 0.0))
                off = _dot(jnp.concatenate(score_rows, axis=0).astype(BF16), v[c0:c0 + cs, :])

                diag_rows = []
                for si in range(cs // sb):
                    r0 = c0 + si * sb
                    q_lo, q_hi = q[r0:r0 + half, :], q[r0 + half:r0 + sb, :]
                    c_lo, c_hi = cum[r0:r0 + half, :], cum[r0 + half:r0 + sb, :]
                    acc_lo = jnp.zeros((half, HEAD_DIM), F32)
                    acc_hi = jnp.zeros((half, HEAD_DIM), F32)
                    for ti in range(sb):
                        a_s = a_ref[r0 + ti:r0 + ti + 1, :]
                        v_s = v_ref[r0 + ti:r0 + ti + 1, :]
                        if ti < half:
                            d = c_lo - a_s
                            if ti > 0:
                                d = jnp.where(half_row >= ti, d, NEG_BIG)
                            acc_lo = acc_lo + jnp.sum(q_lo * jnp.exp2(d), axis=-1, keepdims=True) * v_s
                            d = c_hi - a_s
                        else:
                            d = c_hi - a_s
                            if ti > half:
                                d = jnp.where(half_row >= ti - half, d, NEG_BIG)
                        acc_hi = acc_hi + jnp.sum(q_hi * jnp.exp2(d), axis=-1, keepdims=True) * v_s
                    diag_rows += [acc_lo, acc_hi]
                outs.append(inter + off + jnp.concatenate(diag_rows, axis=0))
            st_ref[...] = st
            finish(rows, outs)
            return carry

        lax.fori_loop(0, lc // blk, general, 0)


def _hgrn(main4, f4, slabs, lb_logits, norm_g, passengers, *, lc):
    _, b, s, _ = main4.shape
    nh, dh = NUM_HEADS, HEAD_DIM
    nl = lb_logits.shape[0]
    assert lc % (HGRN_BLOCK * min(HGRN_GROUP, lc // HGRN_BLOCK)) == 0
    q_blk, i_blk, g_blk = slabs

    def col(first):
        return pl.BlockSpec((None, None, lc, dh), lambda i, h, j: (first + h, i, j, 0))

    n_lc = s // lc
    p_in, p_out, p_shapes = _passenger_specs(passengers, b * nh * n_lc,
                                             lambda i, h, j: (i * nh + h) * n_lc + j)
    return pl.pallas_call(
        _with_passengers(functools.partial(_hgrn_kernel, lc=lc), 6, 1, len(passengers)),
        grid=(b, nh, n_lc),
        in_specs=[
            col(q_blk), col(0), col(i_blk), col(g_blk),
            pl.BlockSpec((nl, dh), lambda i, h, j: (0, h)),
            pl.BlockSpec((1, dh), lambda i, h, j: (0, 0)),
        ] + p_in,
        out_specs=[pl.BlockSpec((None, lc, dh), lambda i, h, j: (i, j, h))] + p_out,
        out_shape=[jax.ShapeDtypeStruct((b, s, nh * dh), BF16)] + p_shapes,
        scratch_shapes=[
            pltpu.VMEM((dh, dh), F32),
            pltpu.VMEM((lc, dh), BF16),
            pltpu.VMEM((lc, dh), F32),
            pltpu.VMEM((lc // HGRN_CHUNK, dh, dh), F32),
            pltpu.VMEM((lc // HGRN_CHUNK * SUBLANES, dh), F32),
            pltpu.VMEM((HGRN_BLOCK, dh), F32),
            pltpu.VMEM((HGRN_BLOCK, dh), F32),
        ],
        compiler_params=_params("parallel", "parallel", "arbitrary"),
        name="hgrn",
    )(main4, f4, main4, main4, lb_logits, norm_g.reshape(1, dh), *passengers)


def _merge_kernel(h_ref, gpost_ref, ya_ref, yb_ref, ga_ref, gb_ref, wpa_ref, wpb_ref, wo_ref, o_ref):
    j = pl.program_id(1)

    def sweep_step(first, last):
        tn = wo_ref.shape[0]
        groups = [slice(c0, c0 + tn // MERGE_COL_GROUPS) for c0 in range(0, tn, tn // MERGE_COL_GROUPS)]
        stage1 = [(_dot(ya_ref[...], wpa_ref[:, c]), _dot(yb_ref[...], wpb_ref[:, c])) for c in groups]
        merged = [(ga_ref[:, c].astype(F32) * y_a + gb_ref[:, c].astype(F32) * y_b).astype(BF16)
                  for c, (y_a, y_b) in zip(groups, stage1)]
        out = _dot(merged[0], wo_ref[groups[0], :])
        for mg, c in zip(merged[1:], groups[1:]):
            out += _dot(mg, wo_ref[c, :])
        if first:
            o_ref[...] = out
        else:
            o_ref[...] += out
        if last:
            _postnorm_residual_to(o_ref, h_ref, o_ref, gpost_ref, 1.0)

    n_steps = pl.num_programs(1)
    pl.when(j == 0)(functools.partial(sweep_step, True, False))
    pl.when((j > 0) & (j < n_steps - 1))(functools.partial(sweep_step, False, False))
    pl.when(j == n_steps - 1)(functools.partial(sweep_step, False, True))


def _merge(h, g_post, y_a, y_b, gates, w_pa, w_pb, w_o, *, tm, tn):
    t, d = h.shape
    wa = y_a.shape[1]
    wb = y_b.shape[1]
    n_j = d // tn

    def residual_rows(i, j):
        return (jnp.where(j == n_j - 1, i, jnp.maximum(i - 1, 0)), 0)

    return pl.pallas_call(
        _merge_kernel,
        grid=(t // tm, n_j),
        in_specs=[
            pl.BlockSpec((tm, d), residual_rows),
            pl.BlockSpec((1, d), lambda i, j: (0, 0)),
            pl.BlockSpec((tm, wa), lambda i, j: (i, 0)),
            pl.BlockSpec((tm, wb), lambda i, j: (i, 0)),
            pl.BlockSpec((tm, tn), lambda i, j: (i, j)),
            pl.BlockSpec((tm, tn), lambda i, j: (i, d // tn + j)),
            pl.BlockSpec((wa, tn), lambda i, j: (0, j)),
            pl.BlockSpec((wb, tn), lambda i, j: (0, j)),
            pl.BlockSpec((tn, d), lambda i, j: (j, 0)),
        ],
        out_specs=pl.BlockSpec((tm, d), lambda i, j: (i, 0)),
        out_shape=jax.ShapeDtypeStruct((t, d), F32),
        compiler_params=_params("parallel", "arbitrary"),
        name="merge",
    )(h, g_post.reshape(1, d), y_a, y_b, gates, gates, w_pa, w_pb, w_o)


def _ple_kernel(h_ref, gpre_ref, gpost_ref, p_ref, wg_ref, wp_ref, o_ref):
    tm = h_ref.shape[0]
    groups = [slice(r0, r0 + tm // PLE_ROW_GROUPS) for r0 in range(0, tm, tm // PLE_ROW_GROUPS)]
    emb = [_dot(p_ref[r, :].astype(BF16), wp_ref[...]) for r in groups]
    gates = [_dot(_rms(h_ref[r, :], gpre_ref[...]).astype(BF16), wg_ref[...]) for r in groups]
    for r, e, g in zip(groups, emb, gates):
        o_ref[r, :] = h_ref[r, :] + _rms(jax.nn.sigmoid(g) * e, gpost_ref[...])


def _ple(h, g_pre, g_post, p2, w_g, w_p, *, tm):
    t, d = h.shape
    dp = p2.shape[1]
    return pl.pallas_call(
        _ple_kernel,
        grid=(t // tm,),
        in_specs=[
            pl.BlockSpec((tm, d), lambda i: (i, 0)),
            pl.BlockSpec((1, d), lambda i: (0, 0)),
            pl.BlockSpec((1, d), lambda i: (0, 0)),
            pl.BlockSpec((tm, dp), lambda i: (i, 0)),
            pl.BlockSpec((d, d), lambda i: (0, 0)),
            pl.BlockSpec((dp, d), lambda i: (0, 0)),
        ],
        out_specs=pl.BlockSpec((tm, d), lambda i: (i, 0)),
        out_shape=jax.ShapeDtypeStruct((t, d), F32),
        compiler_params=_params("parallel"),
        name="ple",
    )(h, g_pre.reshape(1, d), g_post.reshape(1, d), p2, w_g, w_p)


def _tile(n, want):
    t = min(n, want)
    while n % t:
        t //= 2
    return t


def kernel(x, p, ffn1_pre_g, ffn1_post_g, ffn1_w_gate, ffn1_w_up, ffn1_w_down, mix_pre_g, mix_post_g, mix_w_in, fox_f_bias, hgrn_lb_logits, hgrn_norm_g, mix_w_proj_fox, mix_w_proj_hgrn, mix_w_out, ffn2_pre_g, ffn2_post_g, ffn2_w_gate, ffn2_w_up, ffn2_w_down, ple_pre_g, ple_post_g, ple_w_gate, ple_w_proj):
    b, s, d = x.shape
    t = b * s
    depth = ffn1_pre_g.shape[0]
    assert depth == 1, "the HGRN2 lower bound is evaluated for a single layer"
    nh, dh = NUM_HEADS, HEAD_DIM
    width = nh * dh
    assert mix_w_in.shape[-1] == 3 * width + nh + 4 * width + 2 * d

    tm_big = _tile(t, 1024)
    h = x.reshape(t, d)
    for i in range(depth):
        h = _ffn(h, ffn1_pre_g[i], ffn1_post_g[i], ffn1_w_gate[i].astype(BF16),
                 ffn1_w_up[i].astype(BF16), ffn1_w_down[i].astype(BF16),
                 tm=tm_big, tf=_tile(ffn1_w_gate.shape[-1], 512))

        w_in = mix_w_in[i]
        o_f = 3 * width
        o_b = o_f + nh
        w_bf = w_in.astype(BF16)
        w_rest = w_bf[:, o_b:]

        main, gates, f = _proj_in(h, mix_pre_g[i], w_bf, w_rest, tm=tm_big, width=width,
                                  gate_width=2 * d, q_scale=dh ** -0.5 * LOG2E)
        main4 = main.reshape(6 * nh, b, s, dh)
        f4 = f.reshape(nh + 1, b, s, dh)
        c = _fox_gate(f4, nh, fox_f_bias[i], ts=_tile(s, 512))
        y_a, w2_gate, w2_up, w_pa, w_pb, w_o, w_pg = _fox_attn(
            main4, c.reshape(b, nh, 1, s),
            [ffn2_w_gate[i], ffn2_w_up[i], mix_w_proj_fox[i], mix_w_proj_hgrn[i], mix_w_out[i], ple_w_gate[i]],
            tq=_tile(s, 512))
        y_b, w2_down = _hgrn(main4, f4, (3 * nh, 4 * nh, 5 * nh), hgrn_lb_logits, hgrn_norm_g[i],
                             [ffn2_w_down[i]], lc=_tile(s, 2048))
        h = _merge(h, mix_post_g[i], y_a.reshape(t, width), y_b.reshape(t, width), gates,
                   w_pa, w_pb, w_o, tm=tm_big, tn=_tile(d, 512))

        h = _ffn(h, ffn2_pre_g[i], ffn2_post_g[i], w2_gate, w2_up, w2_down,
                 tm=tm_big, tf=_tile(ffn2_w_gate.shape[-1], 512))
        h = _ple(h, ple_pre_g[i], ple_post_g[i], p[i].reshape(t, -1),
                 w_pg, ple_w_proj[i].astype(BF16), tm=_tile(t, 512))
    return h.reshape(b, s, d)
```
